```python
import jax, jax.numpy as jnp
from jax import lax
import numpy as np


D_MODEL = 4096
BATCH = 32
SEQ = 256
DEPTH = 4
DEC_BATCH = 8
DEC_SEQ = 1024
PAST_LEN = 512

GRID_W = 64
N_MIXERS = 2
N_HEADS = 32
HEAD_DIM = D_MODEL // N_HEADS
WIN_H = 8
WIN_W = 16
POOL_WINDOWS = (2, 4, 8, 16)
N_POOL_GROUPS = 4
POOL_GROUP_DIM = D_MODEL // N_POOL_GROUPS
N_ATTN_LAYERS = (DEPTH + 1) // 2
N_POOL_LAYERS = DEPTH // 2
RMS_EPS = 1e-6
NEG_INF = -1e30

kernel_name = 'na_pool_prefix_dit_step'


def rmsnorm(x, g):
    xf = x.astype(jnp.float32)
    y = xf * lax.rsqrt(jnp.mean(xf * xf, axis=-1, keepdims=True) + RMS_EPS)
    return (y * g.astype(jnp.float32)).astype(x.dtype)


def modulation(cond, w_ada, b_ada):
    m = jax.nn.silu(cond) @ w_ada + b_ada
    return jnp.split(m, 3, axis=-1)


def modulate(x, g, shift, scale):
    return rmsnorm(x, g) * (1 + scale) + shift


def attn_context(h, w_in):
    B, T, _ = h.shape
    q, k, v, z = jnp.split(h @ w_in, 4, axis=-1)
    q = q.reshape(B, T, N_HEADS, HEAD_DIM)
    k = k.reshape(B, T, N_HEADS, HEAD_DIM)
    v = v.reshape(B, T, N_HEADS, HEAD_DIM)
    s = jnp.einsum('bqhd,bkhd->bhqk', q, k).astype(jnp.float32) * (HEAD_DIM ** -0.5)
    p = jax.nn.softmax(s, axis=-1).astype(v.dtype)
    o = jnp.einsum('bhqk,bkhd->bqhd', p, v).reshape(B, T, D_MODEL)
    return o * jax.nn.silu(z), k, v


def na_geometry(rows):
    kh = min(WIN_H, rows)
    r = np.arange(rows)
    rs = np.clip(r - kh // 2, 0, rows - kh)
    row_idx = rs[:, None] + np.arange(kh)[None, :]
    dr = row_idx - r[:, None]
    col = np.arange(GRID_W)
    cs = np.clip(col - WIN_W // 2, 0, GRID_W - WIN_W)
    col_in = (col[None, :] >= cs[:, None]) & (col[None, :] < cs[:, None] + WIN_W)
    dc = np.clip(col[None, :] - col[:, None], -(WIN_W - 1), WIN_W - 1)
    return kh, row_idx, dr, col_in, dc


def attn_latent(h, k_ctx, v_ctx, w_in, rpb):
    B, T, _ = h.shape
    rows = T // GRID_W
    kh, row_idx, dr, col_in, dc = na_geometry(rows)
    q, k, v, z = jnp.split(h @ w_in, 4, axis=-1)
    grid = (B, rows, GRID_W, N_HEADS, HEAD_DIM)
    q = q.reshape(grid)
    k = k.reshape(grid)
    v = v.reshape(grid)
    k_rows = k[:, row_idx]
    v_rows = v[:, row_idx]
    scale = HEAD_DIM ** -0.5
    bias = rpb[:, (dr + WIN_H - 1)[:, None, :, None], (dc + WIN_W - 1)[None, :, None, :]]
    s_win = jnp.einsum('brqhd,brkwhd->bhrqkw', q, k_rows).astype(jnp.float32) * scale + bias.astype(jnp.float32)[None]
    s_win = jnp.where(col_in[None, None, None, :, None, :], s_win, NEG_INF)
    s_ctx = jnp.einsum('brqhd,bchd->bhrqc', q, k_ctx).astype(jnp.float32) * scale
    n_win = kh * GRID_W
    s = jnp.concatenate([s_win.reshape(B, N_HEADS, rows, GRID_W, n_win), s_ctx], axis=-1)
    p = jax.nn.softmax(s, axis=-1).astype(v.dtype)
    p_win = p[..., :n_win].reshape(B, N_HEADS, rows, GRID_W, kh, GRID_W)
    p_ctx = p[..., n_win:]
    o = (jnp.einsum('bhrqkw,brkwhd->brqhd', p_win, v_rows)
         + jnp.einsum('bhrqc,bchd->brqhd', p_ctx, v_ctx.astype(v.dtype)))
    o = o.reshape(B, T, D_MODEL)
    return o * jax.nn.silu(z)


def multiscale_pool(u):
    B, T, _ = u.shape
    uf = u.astype(jnp.float32)
    csum = jnp.concatenate([jnp.zeros((B, 1, D_MODEL), jnp.float32), jnp.cumsum(uf, axis=1)], axis=1)
    t = np.arange(T)
    outs = []
    for g, w in enumerate(POOL_WINDOWS):
        lo = np.clip(t - w // 2, 0, T)
        hi = np.clip(t + w // 2, 0, T)
        sl = slice(g * POOL_GROUP_DIM, (g + 1) * POOL_GROUP_DIM)
        cg = csum[..., sl]
        mean = (cg[:, hi] - cg[:, lo]) / (hi - lo).astype(np.float32)[None, :, None]
        outs.append(mean - uf[..., sl])
    return jnp.stack(outs, axis=2)


def pool_mixer(h, w_in, w_grp, pool_scale):
    B, T, _ = h.shape
    u, z = jnp.split(h @ w_in, 2, axis=-1)
    d = multiscale_pool(u).astype(u.dtype)
    y = jnp.einsum('btgi,gio->btgo', d, w_grp).reshape(B, T, D_MODEL)
    return y * pool_scale * jax.nn.silu(z)


def setup_inputs(seed: int = 0) -> dict:
    key = jax.random.key(seed)
    ks = jax.random.split(key, 18)
    D = D_MODEL
    sd = D ** -0.5
    return {
        'x_prompt': jax.random.normal(ks[0], (BATCH, SEQ, D), jnp.float32),
        'x_sample': jax.random.normal(ks[1], (DEC_BATCH, DEC_SEQ, D), jnp.float32),
        'c': jax.random.normal(ks[2], (DEC_BATCH, D), jnp.float32),
        'cache_k': jax.random.normal(ks[3], (DEC_BATCH, N_ATTN_LAYERS, PAST_LEN, N_HEADS, HEAD_DIM), jnp.float32),
        'cache_v': jax.random.normal(ks[4], (DEC_BATCH, N_ATTN_LAYERS, PAST_LEN, N_HEADS, HEAD_DIM), jnp.float32),
        'c_ctx': jax.random.normal(ks[5], (D,), jnp.float32),
        'norm_g': 1.0 + 0.1 * jax.random.normal(ks[6], (DEPTH, D), jnp.float32),
        'w_ada': 0.5 * sd * jax.random.normal(ks[7], (DEPTH, D, 3 * D), jnp.float32),
        'b_ada': 0.02 * jax.random.normal(ks[8], (DEPTH, 3 * D), jnp.float32),
        'w_in_attn': sd * jax.random.normal(ks[9], (N_ATTN_LAYERS, D, 4 * D), jnp.float32),
        'rpb': 0.5 * jax.random.normal(ks[10], (N_ATTN_LAYERS, N_HEADS, 2 * WIN_H - 1, 2 * WIN_W - 1), jnp.float32),
        'w_out_attn': sd * jax.random.normal(ks[11], (N_ATTN_LAYERS, D, D), jnp.float32),
        'w_in_pool': sd * jax.random.normal(ks[12], (N_POOL_LAYERS, D, 2 * D), jnp.float32),
        'w_grp_pool': (POOL_GROUP_DIM ** -0.5) * jax.random.normal(ks[13], (N_POOL_LAYERS, N_POOL_GROUPS, POOL_GROUP_DIM, POOL_GROUP_DIM), jnp.float32),
        'pool_scale': 1.0 + 0.1 * jax.random.normal(ks[14], (N_POOL_LAYERS, D), jnp.float32),
        'w_out_pool': sd * jax.random.normal(ks[15], (N_POOL_LAYERS, D, D), jnp.float32),
        'final_norm_g': 1.0 + 0.1 * jax.random.normal(ks[16], (D,), jnp.float32),
    }


def reference(x_prompt, x_sample, c, cache_k, cache_v, c_ctx, norm_g, w_ada, b_ada,
              w_in_attn, rpb, w_out_attn, w_in_pool, w_grp_pool, pool_scale, w_out_pool,
              final_norm_g):
    xc = x_prompt
    xl = x_sample
    cond_ctx = c_ctx[None, None, :]
    cond_lat = c[:, None, :]
    new_k = []
    new_v = []
    for i in range(DEPTH):
        j = i // N_MIXERS
        sh_c, sc_c, gt_c = modulation(cond_ctx, w_ada[i], b_ada[i])
        sh_l, sc_l, gt_l = modulation(cond_lat, w_ada[i], b_ada[i])
        hc = modulate(xc, norm_g[i], sh_c, sc_c)
        hl = modulate(xl, norm_g[i], sh_l, sc_l)
        if i % N_MIXERS == 0:
            oc, kc, vc = attn_context(hc, w_in_attn[j])
            new_k.append(kc)
            new_v.append(vc)
            ol = attn_latent(hl, cache_k[:, j], cache_v[:, j], w_in_attn[j], rpb[j])
            w_out = w_out_attn[j]
        else:
            oc = pool_mixer(hc, w_in_pool[j], w_grp_pool[j], pool_scale[j])
            ol = pool_mixer(hl, w_in_pool[j], w_grp_pool[j], pool_scale[j])
            w_out = w_out_pool[j]
        xc = xc + gt_c * (oc @ w_out)
        xl = xl + gt_l * (ol @ w_out)
    y_prompt = rmsnorm(xc, final_norm_g)
    y_sample = rmsnorm(xl, final_norm_g)
    new_cache_k = jnp.stack(new_k, axis=1)
    new_cache_v = jnp.stack(new_v, axis=1)
    return (y_prompt, y_sample, new_cache_k, new_cache_v)
```

```python
import functools

import jax
import jax.numpy as jnp
from jax import lax
from jax.experimental import pallas as pl
from jax.experimental.pallas import tpu as pltpu

F32 = jnp.float32
BF16 = jnp.bfloat16

HEAD_DIM = 128
GRID_W = 64
WIN_H = 8
WIN_W = 16
POOL_WINDOWS = (2, 4, 8, 16)
RMS_EPS = 1e-6
NEG_INF = -1e30

Q_ROWS = 4
Q_BLOCK = Q_ROWS * GRID_W
SLAB_BLOCKS = 3
SLAB_ROWS = SLAB_BLOCKS * Q_ROWS
N_REL_ROWS = 2 * WIN_H
N_REL_COLS = 2 * WIN_W - 1
POOL_PAD = 8

VMEM_LIMIT = 56 * 1024 * 1024
MM_TILE = 1024
RESID_TILE_N = 512


def _cparams(*sem):
    return pltpu.CompilerParams(dimension_semantics=sem, vmem_limit_bytes=VMEM_LIMIT)


def _silu(x):
    return x / (1.0 + jnp.exp(-x))


def _ada_kernel(cond_ref, w_ref, b_ref, o_ref):
    s = _silu(cond_ref[...]).astype(BF16)
    o_ref[...] = jnp.dot(s, w_ref[...].astype(BF16), preferred_element_type=F32) + b_ref[...]


def _ada_modulation(cond, w_ada, b_ada):
    depth, d, d3 = w_ada.shape
    r = cond.shape[0]
    tn = min(512, d3)
    return pl.pallas_call(
        _ada_kernel,
        grid=(depth, d3 // tn),
        in_specs=[
            pl.BlockSpec((r, d), lambda i, n: (0, 0)),
            pl.BlockSpec((None, d, tn), lambda i, n: (i, 0, n)),
            pl.BlockSpec((None, 1, tn), lambda i, n: (i, 0, n)),
        ],
        out_specs=pl.BlockSpec((None, r, tn), lambda i, n: (i, 0, n)),
        out_shape=jax.ShapeDtypeStruct((depth, r, d3), F32),
        compiler_params=_cparams("parallel", "parallel"),
        name="ada_modulation",
    )(cond, w_ada, b_ada.reshape(depth, 1, d3))


def _rms(x):
    return x * lax.rsqrt(jnp.mean(x * x, axis=-1, keepdims=True) + RMS_EPS)


def _modulate_kernel(x_ref, g_ref, sh_ref, sc_ref, o_ref):
    y = _rms(x_ref[...]) * g_ref[...]
    o_ref[...] = (y * (1.0 + sc_ref[...]) + sh_ref[...]).astype(BF16)


def _mod_row_map(layer, part, seq_len, rows_per_step, row0, per_batch, ncol_axis):
    def index(*ids):
        m = ids[0]
        row = row0 + (m * rows_per_step) // seq_len if per_batch else row0
        col = part if ncol_axis is None else part * ncol_axis[1] + ids[ncol_axis[0]]
        return (layer, row, 0, col)
    return index


def _modulate(x2d, g_row, mod4, layer, seq_len, row0, per_batch):
    m, d = x2d.shape
    tt = min(256, seq_len)
    return pl.pallas_call(
        _modulate_kernel,
        grid=(m // tt,),
        in_specs=[
            pl.BlockSpec((tt, d), lambda i: (i, 0)),
            pl.BlockSpec((None, 1, d), lambda i: (layer, 0, 0)),
            pl.BlockSpec((None, None, 1, d), _mod_row_map(layer, 0, seq_len, tt, row0, per_batch, None)),
            pl.BlockSpec((None, None, 1, d), _mod_row_map(layer, 1, seq_len, tt, row0, per_batch, None)),
        ],
        out_specs=pl.BlockSpec((tt, d), lambda i: (i, 0)),
        out_shape=jax.ShapeDtypeStruct((m, d), BF16),
        compiler_params=_cparams("parallel"),
        name="modulate",
    )(x2d, g_row, mod4, mod4)


def _final_norm_kernel(x_ref, g_ref, o_ref):
    o_ref[...] = _rms(x_ref[...]) * g_ref[...]


def _final_norm(x2d, g):
    m, d = x2d.shape
    tt = min(256, m)
    return pl.pallas_call(
        _final_norm_kernel,
        grid=(m // tt,),
        in_specs=[pl.BlockSpec((tt, d), lambda i: (i, 0)), pl.BlockSpec((1, d), lambda i: (0, 0))],
        out_specs=pl.BlockSpec((tt, d), lambda i: (i, 0)),
        out_shape=jax.ShapeDtypeStruct((m, d), F32),
        compiler_params=_cparams("parallel"),
        name="final_norm",
    )(x2d, g.reshape(1, d))


def _dot(a_ref, w_ref):
    return jnp.dot(a_ref[...], w_ref[...], preferred_element_type=F32)


def _mm_cast_kernel(a_ref, w_ref, o_ref, *, scale):
    acc = _dot(a_ref, w_ref)
    if scale is not None:
        acc = acc * scale
    o_ref[...] = acc.astype(o_ref.dtype)


def _mm_silu_kernel(a_ref, w_ref, o_ref):
    o_ref[...] = _silu(_dot(a_ref, w_ref)).astype(o_ref.dtype)


def _window_sum(u, window):
    t, n = u.shape
    zeros = jnp.zeros((POOL_PAD, n), F32)
    up = jnp.concatenate([zeros, u, zeros], axis=0)
    size = t + 2 * POOL_PAD

    def shifted(x, k):
        return pltpu.roll(x, k % size, axis=0)

    s = up + shifted(up, 1)
    half = 1
    while 2 * half < window:
        s = shifted(s, half) + shifted(s, -half)
        half *= 2
    return s[POOL_PAD:POOL_PAD + t, :]


def _mm_pool_kernel(a_ref, w_ref, o_ref, *, seq_len, cols_per_group):
    acc = _dot(a_ref, w_ref)
    tm, tn = acc.shape
    group = (pl.program_id(1) * tn) // cols_per_group
    t = lax.broadcasted_iota(jnp.int32, (seq_len, 1), 0)
    for gi, window in enumerate(POOL_WINDOWS):
        @pl.when(group == gi)
        def _(window=window):
            half = window // 2
            count = jnp.minimum(t + half, seq_len) - jnp.maximum(t - half, 0)
            inv_count = 1.0 / count.astype(F32)
            for s in range(tm // seq_len):
                u = acc[s * seq_len:(s + 1) * seq_len, :]
                d = _window_sum(u, window) * inv_count - u
                o_ref[s * seq_len:(s + 1) * seq_len, :] = d.astype(o_ref.dtype)


def _mm_group_kernel(d_ref, w_ref, ps_ref, sz_ref, o_ref):
    acc = _dot(d_ref, w_ref)
    o_ref[...] = (acc * ps_ref[...] * sz_ref[...].astype(F32)).astype(o_ref.dtype)


def _mm_resid_kernel(a_ref, w_ref, x_ref, gate_ref, o_ref):
    o_ref[...] = x_ref[...] + gate_ref[...] * _dot(a_ref, w_ref)


def _tiles(m, n):
    return min(MM_TILE, m), min(MM_TILE, n)


def _matmul(kern, a, w, w_col0, n_out, out_dtype, name, extra_in=(), extra_specs=(), tn_max=None):
    m, k = a.shape
    tm, tn = _tiles(m, n_out)
    if tn_max is not None:
        tn = min(tn, tn_max)
    nb0 = w_col0 // tn
    return pl.pallas_call(
        kern,
        grid=(m // tm, n_out // tn),
        in_specs=[
            pl.BlockSpec((tm, k), lambda i, j: (i, 0)),
            pl.BlockSpec((k, tn), lambda i, j: (0, nb0 + j)),
            *[spec(tm, tn) for spec in extra_specs],
        ],
        out_specs=pl.BlockSpec((tm, tn), lambda i, j: (i, j)),
        out_shape=jax.ShapeDtypeStruct((m, n_out), out_dtype),
        compiler_params=_cparams("parallel", "parallel"),
        name=name,
    )(a, w, *extra_in)


def _group_matmul(d, w_grp, pool_scale_row, sz):
    m, dm = d.shape
    g, dg, _ = w_grp.shape
    tm = min(MM_TILE, m)
    return pl.pallas_call(
        _mm_group_kernel,
        grid=(m // tm, g),
        in_specs=[
            pl.BlockSpec((tm, dg), lambda i, j: (i, j)),
            pl.BlockSpec((None, dg, dg), lambda i, j: (j, 0, 0)),
            pl.BlockSpec((1, dg), lambda i, j: (0, j)),
            pl.BlockSpec((tm, dg), lambda i, j: (i, j)),
        ],
        out_specs=pl.BlockSpec((tm, dg), lambda i, j: (i, j)),
        out_shape=jax.ShapeDtypeStruct((m, dm), BF16),
        compiler_params=_cparams("parallel", "parallel"),
        name="pool_group_matmul",
    )(d, w_grp, pool_scale_row, sz)


def _out_proj_residual(a, w_out, x2d, mod4, layer, seq_len, row0, per_batch):
    m, d = x2d.shape
    tm, tn = _tiles(m, d)
    x_spec = lambda tm, tn: pl.BlockSpec((tm, tn), lambda i, j: (i, j))
    gate_spec = lambda tm, tn: pl.BlockSpec(
        (None, None, 1, tn), _mod_row_map(layer, 2, seq_len, tm, row0, per_batch, (1, d // tn)))
    return _matmul(_mm_resid_kernel, a, w_out, 0, d, F32, "out_proj_residual",
                   extra_in=(x2d, mod4), extra_specs=(x_spec, gate_spec), tn_max=RESID_TILE_N)


def _softmax_parts(scores):
    m = scores[0].max(axis=-1, keepdims=True)
    for s in scores[1:]:
        m = jnp.maximum(m, s.max(axis=-1, keepdims=True))
    exps = [jnp.exp(s - m) for s in scores]
    total = exps[0].sum(axis=-1, keepdims=True)
    for e in exps[1:]:
        total = total + e.sum(axis=-1, keepdims=True)
    return exps, total


def _qk(q, k):
    return lax.dot_general(q, k, (((1,), (1,)), ((), ())), preferred_element_type=F32)


def _attn_ctx_kernel(q_ref, k_ref, v_ref, z_ref, o_ref, *, heads):
    for hh in range(heads):
        sl = slice(hh * HEAD_DIM, (hh + 1) * HEAD_DIM)
        k = k_ref[:, sl].astype(BF16)
        v = v_ref[:, sl].astype(BF16)
        (e,), total = _softmax_parts([_qk(q_ref[:, sl], k)])
        o = jnp.dot(e.astype(BF16), v, preferred_element_type=F32) / total
        o_ref[:, sl] = (o * _silu(z_ref[:, sl].astype(F32))).astype(o_ref.dtype)


def _heads_per_step(n_heads):
    return min(8, n_heads)


def _attn_context(q, k, v, z, seq_len):
    m, d = q.shape
    hg = _heads_per_step(d // HEAD_DIM)
    wblk = hg * HEAD_DIM
    spec = pl.BlockSpec((seq_len, wblk), lambda b, h: (b, h))
    return pl.pallas_call(
        functools.partial(_attn_ctx_kernel, heads=hg),
        grid=(m // seq_len, d // wblk),
        in_specs=[spec, spec, spec, spec],
        out_specs=spec,
        out_shape=jax.ShapeDtypeStruct((m, d), BF16),
        compiler_params=_cparams("parallel", "parallel"),
        name="attn_context",
    )(q, k, v, z)


def _rel_bias_kernel(rpb_ref, o_ref):
    rows, lanes = o_ref.shape
    lane = lax.broadcasted_iota(jnp.int32, (1, lanes), 1)
    qc = lane // (2 * GRID_W)
    kc = lane % GRID_W
    rel = jnp.clip(kc - qc, -(WIN_W - 1), WIN_W - 1) + (WIN_W - 1)
    start = jnp.clip(qc - WIN_W // 2, 0, GRID_W - WIN_W)
    in_window = (kc >= start) & (kc < start + WIN_W)
    table = rpb_ref[...]
    acc = jnp.zeros((rows, lanes), F32)
    for j in range(N_REL_COLS):
        acc = jnp.where(rel == j, table[:, j:j + 1], acc)
    row = lax.broadcasted_iota(jnp.int32, (rows, 1), 0)
    masked_row = (row % N_REL_ROWS) == (N_REL_ROWS - 1)
    o_ref[...] = jnp.where(in_window & jnp.logical_not(masked_row), acc, NEG_INF)


def _rel_bias_table(rpb_layer):
    h = rpb_layer.shape[0]
    padded = jnp.pad(rpb_layer, ((0, 0), (0, 1), (0, 0))).reshape(h * N_REL_ROWS, N_REL_COLS)
    rows = h * N_REL_ROWS
    tr = min(64, rows)
    lanes = GRID_W * 2 * GRID_W
    flat = pl.pallas_call(
        _rel_bias_kernel,
        grid=(rows // tr,),
        in_specs=[pl.BlockSpec((tr, N_REL_COLS), lambda i: (i, 0))],
        out_specs=pl.BlockSpec((tr, lanes), lambda i: (i, 0)),
        out_shape=jax.ShapeDtypeStruct((rows, lanes), F32),
        compiler_params=_cparams("parallel"),
        name="rel_bias_table",
    )(padded)
    return flat.reshape(h, N_REL_ROWS, GRID_W, 2 * GRID_W)


def _attn_lat_kernel(q_ref, k0_ref, k1_ref, k2_ref, v0_ref, v1_ref, v2_ref, kc_ref, vc_ref, z_ref,
                     tab_ref, o_ref, bias_ref, *, heads, grid_rows):
    rb = pl.program_id(1)
    slab0 = Q_ROWS * jnp.clip(rb - 1, 0, grid_rows // Q_ROWS - SLAB_BLOCKS)

    @pl.when(pl.program_id(2) == 0)
    def _():
        left = lax.broadcasted_iota(jnp.int32, (GRID_W, 2 * GRID_W), 1) < GRID_W
        for rq in range(Q_ROWS):
            r = rb * Q_ROWS + rq
            win0 = jnp.clip(r - WIN_H // 2, 0, grid_rows - WIN_H)

            def entry(kk):
                kr = slab0 + kk
                ok = (kr >= win0) & (kr < win0 + WIN_H)
                return jnp.where(ok, kr - r + (WIN_H - 1), N_REL_ROWS - 1)

            for kk in range(0, SLAB_ROWS, 2):
                i_left, i_right = entry(kk), entry(kk + 1)
                for hh in range(heads):
                    tile = jnp.where(left, tab_ref[hh, i_left], tab_ref[hh, i_right])
                    bias_ref[hh, rq * GRID_W:(rq + 1) * GRID_W, kk * GRID_W:(kk + 2) * GRID_W] = tile

    k_refs = (k0_ref, k1_ref, k2_ref)
    v_refs = (v0_ref, v1_ref, v2_ref)
    for hh in range(heads):
        sl = slice(hh * HEAD_DIM, (hh + 1) * HEAD_DIM)
        q = q_ref[:, sl]
        scores = [_qk(q, k_refs[p][:, sl]) + bias_ref[hh, :, p * Q_BLOCK:(p + 1) * Q_BLOCK]
                  for p in range(SLAB_BLOCKS)]
        scores.append(_qk(q, kc_ref[:, sl].astype(BF16)))
        exps, total = _softmax_parts(scores)
        o = jnp.dot(exps[-1].astype(BF16), vc_ref[:, sl].astype(BF16), preferred_element_type=F32)
        for p in range(SLAB_BLOCKS):
            o = o + jnp.dot(exps[p].astype(BF16), v_refs[p][:, sl], preferred_element_type=F32)
        o = o / total
        o_ref[:, sl] = (o * _silu(z_ref[:, sl].astype(F32))).astype(o_ref.dtype)


def _attn_latent(q, k, v, z, k_ctx, v_ctx, layer, bias_tab, seq_len):
    m, d = q.shape
    batch = m // seq_len
    grid_rows = seq_len // GRID_W
    n_rb = grid_rows // Q_ROWS
    assert grid_rows % Q_ROWS == 0 and n_rb >= SLAB_BLOCKS
    past = k_ctx.shape[2]
    hg = _heads_per_step(d // HEAD_DIM)
    wblk = hg * HEAD_DIM

    q_spec = pl.BlockSpec((Q_BLOCK, wblk), lambda h, rb, b: (b * n_rb + rb, h))

    def slab_spec(p):
        return pl.BlockSpec(
            (Q_BLOCK, wblk),
            lambda h, rb, b: (b * n_rb + jnp.clip(rb - 1, 0, n_rb - SLAB_BLOCKS) + p, h))

    ctx_spec = pl.BlockSpec((None, None, past, wblk), lambda h, rb, b: (b, layer, 0, h))
    tab_spec = pl.BlockSpec((hg, N_REL_ROWS, GRID_W, 2 * GRID_W), lambda h, rb, b: (h, 0, 0, 0))
    return pl.pallas_call(
        functools.partial(_attn_lat_kernel, heads=hg, grid_rows=grid_rows),
        grid=(d // wblk, n_rb, batch),
        in_specs=[q_spec, slab_spec(0), slab_spec(1), slab_spec(2),
                  slab_spec(0), slab_spec(1), slab_spec(2),
                  ctx_spec, ctx_spec, q_spec, tab_spec],
        out_specs=q_spec,
        out_shape=jax.ShapeDtypeStruct((m, d), BF16),
        scratch_shapes=[pltpu.VMEM((hg, Q_BLOCK, SLAB_BLOCKS * Q_BLOCK), F32)],
        compiler_params=_cparams("parallel", "parallel", "arbitrary"),
        name="attn_latent",
    )(q, k, k, k, v, v, v, k_ctx, v_ctx, z, bias_tab)


def _attn_projections(h, w_in, d, kv_dtype):
    scale = HEAD_DIM ** -0.5
    q = _matmul(functools.partial(_mm_cast_kernel, scale=scale), h, w_in, 0, d, BF16, "attn_in_q")
    plain = functools.partial(_mm_cast_kernel, scale=None)
    k = _matmul(plain, h, w_in, d, d, kv_dtype, "attn_in_k")
    v = _matmul(plain, h, w_in, 2 * d, d, kv_dtype, "attn_in_v")
    z = _matmul(plain, h, w_in, 3 * d, d, BF16, "attn_in_z")
    return q, k, v, z


def _pool_branch(h, w_in, w_grp, pool_scale_row, d, seq_len):
    dg = d // len(POOL_WINDOWS)
    pooled = _matmul(functools.partial(_mm_pool_kernel, seq_len=seq_len, cols_per_group=dg),
                     h, w_in, 0, d, BF16, "pool_in_u", tn_max=dg)
    sz = _matmul(_mm_silu_kernel, h, w_in, d, d, BF16, "pool_in_z")
    return _group_matmul(pooled, w_grp, pool_scale_row, sz)


def kernel(x_prompt, x_sample, c, cache_k, cache_v, c_ctx, norm_g, w_ada, b_ada, w_in_attn, rpb,
           w_out_attn, w_in_pool, w_grp_pool, pool_scale, w_out_pool, final_norm_g):
    bc, tc, d = x_prompt.shape
    bl, tl, _ = x_sample.shape
    depth = norm_g.shape[0]
    n_heads = cache_k.shape[3]
    past = cache_k.shape[2]
    n_attn = w_in_attn.shape[0]
    assert n_heads * HEAD_DIM == d and tl % (GRID_W * Q_ROWS) == 0

    n_cond = -(-(1 + bl) // 8) * 8
    cond = jnp.concatenate([c_ctx[None, :], c, jnp.zeros((n_cond - 1 - bl, d), F32)], axis=0)
    mod = _ada_modulation(cond, w_ada, b_ada)
    mod4 = mod.reshape(depth, n_cond, 1, 3 * d)

    xc = x_prompt.reshape(bc * tc, d)
    xl = x_sample.reshape(bl * tl, d)
    g_rows = norm_g.reshape(depth, 1, d)
    k_ctx = cache_k.reshape(bl, n_attn, past, d)
    v_ctx = cache_v.reshape(bl, n_attn, past, d)

    new_k, new_v = [], []
    for i in range(depth):
        j = i // 2
        hc = _modulate(xc, g_rows, mod4, i, tc, 0, False)
        hl = _modulate(xl, g_rows, mod4, i, tl, 1, True)
        if i % 2 == 0:
            w_in = w_in_attn[j].astype(BF16)
            qc, kc, vc, zc = _attn_projections(hc, w_in, d, F32)
            new_k.append(kc.reshape(bc, tc, n_heads, HEAD_DIM))
            new_v.append(vc.reshape(bc, tc, n_heads, HEAD_DIM))
            oc = _attn_context(qc, kc, vc, zc, tc)
            ql, kl, vl, zl = _attn_projections(hl, w_in, d, BF16)
            bias_tab = _rel_bias_table(rpb[j])
            ol = _attn_latent(ql, kl, vl, zl, k_ctx, v_ctx, j, bias_tab, tl)
            w_out = w_out_attn[j].astype(BF16)
        else:
            w_in = w_in_pool[j].astype(BF16)
            w_grp = w_grp_pool[j].astype(BF16)
            ps_row = pool_scale[j].reshape(1, d)
            oc = _pool_branch(hc, w_in, w_grp, ps_row, d, tc)
            ol = _pool_branch(hl, w_in, w_grp, ps_row, d, tl)
            w_out = w_out_pool[j].astype(BF16)
        xc = _out_proj_residual(oc, w_out, xc, mod4, i, tc, 0, False)
        xl = _out_proj_residual(ol, w_out, xl, mod4, i, tl, 1, True)

    y_prompt = _final_norm(xc, final_norm_g).reshape(bc, tc, d)
    y_sample = _final_norm(xl, final_norm_g).reshape(bl, tl, d)
    return (y_prompt, y_sample, jnp.stack(new_k, axis=1), jnp.stack(new_v, axis=1))
```

```python
import functools
from typing import NamedTuple

import jax
import jax.numpy as jnp
from jax import lax
from jax.experimental import pallas as pl
from jax.experimental.pallas import tpu as pltpu

F32 = jnp.float32
BF16 = jnp.bfloat16

LANES = 128
HEAD_DIM = 128
GRID_W = 64
WIN_H = 8
WIN_W = 16
POOL_WINDOWS = (2, 4, 8, 16)
RMS_EPS = 1e-6
NEG_INF = -1e30

Q_ROWS = 4
Q_BLOCK = Q_ROWS * GRID_W
SLAB_BLOCKS = 3
SLAB_ROWS = SLAB_BLOCKS * Q_ROWS
N_REL_ROWS = 2 * WIN_H
N_REL_COLS = 2 * WIN_W - 1
POOL_PAD = 8

VMEM_LIMIT = 58 * 1024 * 1024
MM_TILE = 1024
NORM_ROWS = 16


def _cparams(*sem):
    return pltpu.CompilerParams(dimension_semantics=sem, vmem_limit_bytes=VMEM_LIMIT)


def _silu(x):
    return x / (1.0 + jnp.exp(-x))


def _ada_kernel(cond_ref, w_ref, b_ref, o_ref):
    s = _silu(cond_ref[...]).astype(BF16)
    o_ref[...] = jnp.dot(s, w_ref[...].astype(BF16), preferred_element_type=F32) + b_ref[...]


def _ada_modulation(cond, w_ada, b_ada):
    depth, d, d3 = w_ada.shape
    r = cond.shape[0]
    tn = min(512, d3)
    return pl.pallas_call(
        _ada_kernel,
        grid=(depth, d3 // tn),
        in_specs=[
            pl.BlockSpec((r, d), lambda i, n: (0, 0)),
            pl.BlockSpec((None, d, tn), lambda i, n: (i, 0, n)),
            pl.BlockSpec((None, 1, tn), lambda i, n: (i, 0, n)),
        ],
        out_specs=pl.BlockSpec((None, r, tn), lambda i, n: (i, 0, n)),
        out_shape=jax.ShapeDtypeStruct((depth, r, d3), F32),
        compiler_params=_cparams("parallel", "parallel"),
        name="ada_modulation",
    )(cond, w_ada, b_ada.reshape(depth, 1, d3))


def _rms(x):
    return x * lax.rsqrt(jnp.mean(x * x, axis=-1, keepdims=True) + RMS_EPS)


def _modulate_kernel(x_ref, g_ref, sh_ref, sc_ref, o_ref, rinv_ref, gain_ref, shift_ref):
    x = x_ref[...]
    rinv = lax.rsqrt(jnp.mean(x * x, axis=-1, keepdims=True) + RMS_EPS)
    rinv_ref[...] = jnp.broadcast_to(rinv, rinv_ref.shape)
    gain_ref[...] = jnp.broadcast_to(g_ref[...] * (1.0 + sc_ref[...]), gain_ref.shape)
    shift_ref[...] = jnp.broadcast_to(sh_ref[...], shift_ref.shape)

    def chunk(i, carry):
        rows = pl.ds(pl.multiple_of(i * NORM_ROWS, NORM_ROWS), NORM_ROWS)
        r = rinv_ref[rows, :]
        for c in range(x_ref.shape[1] // LANES):
            cols = slice(c * LANES, (c + 1) * LANES)
            y = x_ref[rows, cols] * r * gain_ref[:, cols] + shift_ref[:, cols]
            o_ref[rows, cols] = y.astype(BF16)
        return carry

    lax.fori_loop(0, x_ref.shape[0] // NORM_ROWS, chunk, 0)


def _mod_row_map(layer, part, seq_len, rows_per_step, row0, per_batch, ncol_axis):
    def index(*ids):
        m = ids[0]
        row = row0 + (m * rows_per_step) // seq_len if per_batch else row0
        col = part if ncol_axis is None else part * ncol_axis[1] + ids[ncol_axis[0]]
        return (layer, row, 0, col)
    return index


def _modulate(x2d, g_row, mod4, layer, seq_len, row0, per_batch):
    m, d = x2d.shape
    tt = min(256, seq_len)
    return pl.pallas_call(
        _modulate_kernel,
        grid=(m // tt,),
        in_specs=[
            pl.BlockSpec((tt, d), lambda i: (i, 0)),
            pl.BlockSpec((None, 1, d), lambda i: (layer, 0, 0)),
            pl.BlockSpec((None, None, 1, d), _mod_row_map(layer, 0, seq_len, tt, row0, per_batch, None)),
            pl.BlockSpec((None, None, 1, d), _mod_row_map(layer, 1, seq_len, tt, row0, per_batch, None)),
        ],
        out_specs=pl.BlockSpec((tt, d), lambda i: (i, 0)),
        out_shape=jax.ShapeDtypeStruct((m, d), BF16),
        scratch_shapes=[pltpu.VMEM((tt, LANES), F32), pltpu.VMEM((NORM_ROWS, d), F32),
                        pltpu.VMEM((NORM_ROWS, d), F32)],
        compiler_params=_cparams("parallel"),
        name="modulate",
    )(x2d, g_row, mod4, mod4)


def _final_norm_kernel(x_ref, g_ref, o_ref):
    o_ref[...] = _rms(x_ref[...]) * g_ref[...]


def _final_norm(x2d, g):
    m, d = x2d.shape
    tt = min(256, m)
    return pl.pallas_call(
        _final_norm_kernel,
        grid=(m // tt,),
        in_specs=[pl.BlockSpec((tt, d), lambda i: (i, 0)), pl.BlockSpec((1, d), lambda i: (0, 0))],
        out_specs=pl.BlockSpec((tt, d), lambda i: (i, 0)),
        out_shape=jax.ShapeDtypeStruct((m, d), F32),
        compiler_params=_cparams("parallel"),
        name="final_norm",
    )(x2d, g.reshape(1, d))


def _dot(a_ref, w_ref):
    return jnp.dot(a_ref[...], w_ref[...], preferred_element_type=F32)


def _mm_cast_kernel(a_ref, w_ref, o_ref, *, scale):
    acc = _dot(a_ref, w_ref)
    if scale is not None:
        acc = acc * scale
    o_ref[...] = acc.astype(o_ref.dtype)


def _mm_silu_kernel(a_ref, w_ref, o_ref):
    o_ref[...] = _silu(_dot(a_ref, w_ref)).astype(o_ref.dtype)


def _window_sum(u, window):
    t, n = u.shape
    zeros = jnp.zeros((POOL_PAD, n), F32)
    up = jnp.concatenate([zeros, u, zeros], axis=0)
    size = t + 2 * POOL_PAD

    def ahead(x, k):
        return pltpu.roll(x, (-k) % size, axis=0)

    fwd, span = up, 1
    while 2 * span < window:
        fwd = fwd + ahead(fwd, span)
        span *= 2
    s = fwd + ahead(fwd, -span)
    return s[POOL_PAD:POOL_PAD + t, :]


def _mm_pool_kernel(a_ref, w_ref, o_ref, *, seq_len, cols_per_group):
    acc = _dot(a_ref, w_ref)
    tm, tn = acc.shape
    group = (pl.program_id(1) * tn) // cols_per_group
    t = lax.broadcasted_iota(jnp.int32, (seq_len, 1), 0)
    for gi, window in enumerate(POOL_WINDOWS):
        @pl.when(group == gi)
        def _(window=window):
            half = window // 2
            count = jnp.minimum(t + half, seq_len) - jnp.maximum(t - half, 0)
            inv_count = 1.0 / count.astype(F32)
            for s in range(tm // seq_len):
                u = acc[s * seq_len:(s + 1) * seq_len, :]
                d = _window_sum(u, window) * inv_count - u
                o_ref[s * seq_len:(s + 1) * seq_len, :] = d.astype(o_ref.dtype)


def _mm_group_kernel(d_ref, w_ref, ps_ref, sz_ref, o_ref):
    acc = _dot(d_ref, w_ref)
    o_ref[...] = (acc * ps_ref[...] * sz_ref[...].astype(F32)).astype(o_ref.dtype)


def _mm_resid_kernel(a_ref, w_ref, x_ref, gate_ref, o_ref):
    o_ref[...] = x_ref[...] + gate_ref[...] * _dot(a_ref, w_ref)


def _tiles(m, n):
    return min(MM_TILE, m), min(MM_TILE, n)


class _Weight(NamedTuple):
    stack: jax.Array
    layer: int


def _matmul(kern, a, w, w_col0, n_out, out_dtype, name, extra_in=(), extra_specs=(), tn_max=None):
    m, k = a.shape
    tm, tn = _tiles(m, n_out)
    if tn_max is not None:
        tn = min(tn, tn_max)
    nb0 = w_col0 // tn
    return pl.pallas_call(
        kern,
        grid=(m // tm, n_out // tn),
        in_specs=[
            pl.BlockSpec((tm, k), lambda i, j: (i, 0)),
            pl.BlockSpec((None, k, tn), lambda i, j: (w.layer, 0, nb0 + j)),
            *[spec(tm, tn) for spec in extra_specs],
        ],
        out_specs=pl.BlockSpec((tm, tn), lambda i, j: (i, j)),
        out_shape=jax.ShapeDtypeStruct((m, n_out), out_dtype),
        compiler_params=_cparams("parallel", "parallel"),
        name=name,
    )(a, w.stack, *extra_in)


def _group_matmul(d, w_grp, pool_scale_row, sz):
    m, dm = d.shape
    _, g, dg, _ = w_grp.stack.shape
    tm = min(MM_TILE, m)
    return pl.pallas_call(
        _mm_group_kernel,
        grid=(m // tm, g),
        in_specs=[
            pl.BlockSpec((tm, dg), lambda i, j: (i, j)),
            pl.BlockSpec((None, None, dg, dg), lambda i, j: (w_grp.layer, j, 0, 0)),
            pl.BlockSpec((1, dg), lambda i, j: (0, j)),
            pl.BlockSpec((tm, dg), lambda i, j: (i, j)),
        ],
        out_specs=pl.BlockSpec((tm, dg), lambda i, j: (i, j)),
        out_shape=jax.ShapeDtypeStruct((m, dm), BF16),
        compiler_params=_cparams("parallel", "parallel"),
        name="pool_group_matmul",
    )(d, w_grp.stack, pool_scale_row, sz)


def _out_proj_residual(a, w_out, x2d, mod4, layer, seq_len, row0, per_batch):
    m, d = x2d.shape
    tm, tn = _tiles(m, d)
    x_spec = lambda tm, tn: pl.BlockSpec((tm, tn), lambda i, j: (i, j))
    gate_spec = lambda tm, tn: pl.BlockSpec(
        (None, None, 1, tn), _mod_row_map(layer, 2, seq_len, tm, row0, per_batch, (1, d // tn)))
    return _matmul(_mm_resid_kernel, a, w_out, 0, d, F32, "out_proj_residual",
                   extra_in=(x2d, mod4), extra_specs=(x_spec, gate_spec))


def _softmax_exps(scores):
    m = scores[0].max(axis=-1, keepdims=True)
    for s in scores[1:]:
        m = jnp.maximum(m, s.max(axis=-1, keepdims=True))
    return [jnp.exp(s - m).astype(BF16) for s in scores]


def _qk(q, k):
    return lax.dot_general(q, k, (((1,), (1,)), ((), ())), preferred_element_type=F32)


def _with_ones(v):
    return jnp.concatenate([v, jnp.ones(v.shape, v.dtype)], axis=1)


def _normalise_gate(o_ext, z):
    o = o_ext[:, :HEAD_DIM] / o_ext[:, HEAD_DIM:]
    return o * _silu(z.astype(F32))


def _attn_ctx_kernel(q_ref, k_ref, v_ref, z_ref, o_ref, *, heads):
    for hh in range(heads):
        sl = slice(hh * HEAD_DIM, (hh + 1) * HEAD_DIM)
        k = k_ref[:, sl].astype(BF16)
        v = _with_ones(v_ref[:, sl].astype(BF16))
        (e,) = _softmax_exps([_qk(q_ref[:, sl], k)])
        o_ext = jnp.dot(e, v, preferred_element_type=F32)
        o_ref[:, sl] = _normalise_gate(o_ext, z_ref[:, sl]).astype(o_ref.dtype)


def _heads_per_step(n_heads):
    return min(8, n_heads)


def _attn_context(q, k, v, z, seq_len):
    m, d = q.shape
    hg = _heads_per_step(d // HEAD_DIM)
    wblk = hg * HEAD_DIM
    spec = pl.BlockSpec((seq_len, wblk), lambda b, h: (b, h))
    return pl.pallas_call(
        functools.partial(_attn_ctx_kernel, heads=hg),
        grid=(m // seq_len, d // wblk),
        in_specs=[spec, spec, spec, spec],
        out_specs=spec,
        out_shape=jax.ShapeDtypeStruct((m, d), BF16),
        compiler_params=_cparams("parallel", "parallel"),
        name="attn_context",
    )(q, k, v, z)


def _rel_bias_kernel(rpb_ref, o_ref):
    rows, lanes = o_ref.shape
    lane = lax.broadcasted_iota(jnp.int32, (1, lanes), 1)
    qc = lane // (2 * GRID_W)
    kc = lane % GRID_W
    rel = jnp.clip(kc - qc, -(WIN_W - 1), WIN_W - 1) + (WIN_W - 1)
    start = jnp.clip(qc - WIN_W // 2, 0, GRID_W - WIN_W)
    in_window = (kc >= start) & (kc < start + WIN_W)
    table = rpb_ref[...]
    acc = jnp.zeros((rows, lanes), F32)
    for j in range(N_REL_COLS):
        acc = jnp.where(rel == j, table[:, j:j + 1], acc)
    row = lax.broadcasted_iota(jnp.int32, (rows, 1), 0)
    masked_row = (row % N_REL_ROWS) == (N_REL_ROWS - 1)
    o_ref[...] = jnp.where(in_window & jnp.logical_not(masked_row), acc, NEG_INF)


def _rel_bias_table(rpb_layer):
    h = rpb_layer.shape[0]
    padded = jnp.pad(rpb_layer, ((0, 0), (0, 1), (0, 0))).reshape(h * N_REL_ROWS, N_REL_COLS)
    rows = h * N_REL_ROWS
    tr = min(64, rows)
    lanes = GRID_W * 2 * GRID_W
    flat = pl.pallas_call(
        _rel_bias_kernel,
        grid=(rows // tr,),
        in_specs=[pl.BlockSpec((tr, N_REL_COLS), lambda i: (i, 0))],
        out_specs=pl.BlockSpec((tr, lanes), lambda i: (i, 0)),
        out_shape=jax.ShapeDtypeStruct((rows, lanes), F32),
        compiler_params=_cparams("parallel"),
        name="rel_bias_table",
    )(padded)
    return flat.reshape(h, N_REL_ROWS, GRID_W, 2 * GRID_W)


def _attn_lat_kernel(q_ref, k0_ref, k1_ref, k2_ref, v0_ref, v1_ref, v2_ref, kc_ref, vc_ref, z_ref,
                     tab_ref, o_ref, bias_ref, *, heads, grid_rows):
    rb = pl.program_id(1)
    slab0 = Q_ROWS * jnp.clip(rb - 1, 0, grid_rows // Q_ROWS - SLAB_BLOCKS)

    @pl.when(pl.program_id(2) == 0)
    def _():
        left = lax.broadcasted_iota(jnp.int32, (GRID_W, 2 * GRID_W), 1) < GRID_W
        for rq in range(Q_ROWS):
            r = rb * Q_ROWS + rq
            win0 = jnp.clip(r - WIN_H // 2, 0, grid_rows - WIN_H)

            def entry(kk):
                kr = slab0 + kk
                ok = (kr >= win0) & (kr < win0 + WIN_H)
                return jnp.where(ok, kr - r + (WIN_H - 1), N_REL_ROWS - 1)

            for kk in range(0, SLAB_ROWS, 2):
                i_left, i_right = entry(kk), entry(kk + 1)
                for hh in range(heads):
                    tile = jnp.where(left, tab_ref[hh, i_left], tab_ref[hh, i_right])
                    bias_ref[hh, rq * GRID_W:(rq + 1) * GRID_W, kk * GRID_W:(kk + 2) * GRID_W] = tile

    k_refs = (k0_ref, k1_ref, k2_ref)
    v_refs = (v0_ref, v1_ref, v2_ref)
    for hh in range(heads):
        sl = slice(hh * HEAD_DIM, (hh + 1) * HEAD_DIM)
        q = q_ref[:, sl]
        scores = [_qk(q, k_refs[p][:, sl]) + bias_ref[hh, :, p * Q_BLOCK:(p + 1) * Q_BLOCK]
                  for p in range(SLAB_BLOCKS)]
        scores.append(_qk(q, kc_ref[:, sl].astype(BF16)))
        exps = _softmax_exps(scores)
        o_ext = jnp.dot(exps[-1], _with_ones(vc_ref[:, sl].astype(BF16)), preferred_element_type=F32)
        for p in range(SLAB_BLOCKS):
            o_ext = o_ext + jnp.dot(exps[p], _with_ones(v_refs[p][:, sl]), preferred_element_type=F32)
        o_ref[:, sl] = _normalise_gate(o_ext, z_ref[:, sl]).astype(o_ref.dtype)


def _attn_latent(q, k, v, z, k_ctx, v_ctx, layer, bias_tab, seq_len):
    m, d = q.shape
    batch = m // seq_len
    grid_rows = seq_len // GRID_W
    n_rb = grid_rows // Q_ROWS
    assert grid_rows % Q_ROWS == 0 and n_rb >= SLAB_BLOCKS
    past = k_ctx.shape[2]
    hg = _heads_per_step(d // HEAD_DIM)
    wblk = hg * HEAD_DIM

    q_spec = pl.BlockSpec((Q_BLOCK, wblk), lambda h, rb, b: (b * n_rb + rb, h))

    def slab_spec(p):
        return pl.BlockSpec(
            (Q_BLOCK, wblk),
            lambda h, rb, b: (b * n_rb + jnp.clip(rb - 1, 0, n_rb - SLAB_BLOCKS) + p, h))

    ctx_spec = pl.BlockSpec((None, None, past, wblk), lambda h, rb, b: (b, layer, 0, h))
    tab_spec = pl.BlockSpec((hg, N_REL_ROWS, GRID_W, 2 * GRID_W), lambda h, rb, b: (h, 0, 0, 0))
    return pl.pallas_call(
        functools.partial(_attn_lat_kernel, heads=hg, grid_rows=grid_rows),
        grid=(d // wblk, n_rb, batch),
        in_specs=[q_spec, slab_spec(0), slab_spec(1), slab_spec(2),
                  slab_spec(0), slab_spec(1), slab_spec(2),
                  ctx_spec, ctx_spec, q_spec, tab_spec],
        out_specs=q_spec,
        out_shape=jax.ShapeDtypeStruct((m, d), BF16),
        scratch_shapes=[pltpu.VMEM((hg, Q_BLOCK, SLAB_BLOCKS * Q_BLOCK), F32)],
        compiler_params=_cparams("parallel", "parallel", "arbitrary"),
        name="attn_latent",
    )(q, k, k, k, v, v, v, k_ctx, v_ctx, z, bias_tab)


def _attn_projections(h, w_in, d, kv_dtype):
    scale = HEAD_DIM ** -0.5
    q = _matmul(functools.partial(_mm_cast_kernel, scale=scale), h, w_in, 0, d, BF16, "attn_in_q")
    plain = functools.partial(_mm_cast_kernel, scale=None)
    k = _matmul(plain, h, w_in, d, d, kv_dtype, "attn_in_k")
    v = _matmul(plain, h, w_in, 2 * d, d, kv_dtype, "attn_in_v")
    z = _matmul(plain, h, w_in, 3 * d, d, BF16, "attn_in_z")
    return q, k, v, z


def _pool_branch(h, w_in, w_grp, pool_scale_row, d, seq_len):
    dg = d // len(POOL_WINDOWS)
    pooled = _matmul(functools.partial(_mm_pool_kernel, seq_len=seq_len, cols_per_group=dg),
                     h, w_in, 0, d, BF16, "pool_in_u", tn_max=dg)
    sz = _matmul(_mm_silu_kernel, h, w_in, d, d, BF16, "pool_in_z")
    return _group_matmul(pooled, w_grp, pool_scale_row, sz)


def kernel(x_prompt, x_sample, c, cache_k, cache_v, c_ctx, norm_g, w_ada, b_ada, w_in_attn, rpb,
           w_out_attn, w_in_pool, w_grp_pool, pool_scale, w_out_pool, final_norm_g):
    bc, tc, d = x_prompt.shape
    bl, tl, _ = x_sample.shape
    depth = norm_g.shape[0]
    n_heads = cache_k.shape[3]
    past = cache_k.shape[2]
    n_attn = w_in_attn.shape[0]
    assert n_heads * HEAD_DIM == d and tl % (GRID_W * Q_ROWS) == 0

    n_cond = -(-(1 + bl) // 8) * 8
    cond = jnp.concatenate([c_ctx[None, :], c, jnp.zeros((n_cond - 1 - bl, d), F32)], axis=0)
    mod = _ada_modulation(cond, w_ada, b_ada)
    mod4 = mod.reshape(depth, n_cond, 1, 3 * d)

    xc = x_prompt.reshape(bc * tc, d)
    xl = x_sample.reshape(bl * tl, d)
    g_rows = norm_g.reshape(depth, 1, d)
    k_ctx = cache_k.reshape(bl, n_attn, past, d)
    v_ctx = cache_v.reshape(bl, n_attn, past, d)

    w_in_attn_b = w_in_attn.astype(BF16)
    w_out_attn_b = w_out_attn.astype(BF16)
    w_in_pool_b = w_in_pool.astype(BF16)
    w_grp_pool_b = w_grp_pool.astype(BF16)
    w_out_pool_b = w_out_pool.astype(BF16)

    new_k, new_v = [], []
    for i in range(depth):
        j = i // 2
        hc = _modulate(xc, g_rows, mod4, i, tc, 0, False)
        hl = _modulate(xl, g_rows, mod4, i, tl, 1, True)
        if i % 2 == 0:
            w_in = _Weight(w_in_attn_b, j)
            qc, kc, vc, zc = _attn_projections(hc, w_in, d, F32)
            new_k.append(kc.reshape(bc, tc, n_heads, HEAD_DIM))
            new_v.append(vc.reshape(bc, tc, n_heads, HEAD_DIM))
            oc = _attn_context(qc, kc, vc, zc, tc)
            ql, kl, vl, zl = _attn_projections(hl, w_in, d, BF16)
            bias_tab = _rel_bias_table(rpb[j])
            ol = _attn_latent(ql, kl, vl, zl, k_ctx, v_ctx, j, bias_tab, tl)
            w_out = _Weight(w_out_attn_b, j)
        else:
            w_in = _Weight(w_in_pool_b, j)
            w_grp = _Weight(w_grp_pool_b, j)
            ps_row = pool_scale[j].reshape(1, d)
            oc = _pool_branch(hc, w_in, w_grp, ps_row, d, tc)
            ol = _pool_branch(hl, w_in, w_grp, ps_row, d, tl)
            w_out = _Weight(w_out_pool_b, j)
        xc = _out_proj_residual(oc, w_out, xc, mod4, i, tc, 0, False)
        xl = _out_proj_residual(ol, w_out, xl, mod4, i, tl, 1, True)

    y_prompt = _final_norm(xc, final_norm_g).reshape(bc, tc, d)
    y_sample = _final_norm(xl, final_norm_g).reshape(bl, tl, d)
    return (y_prompt, y_sample, jnp.stack(new_k, axis=1), jnp.stack(new_v, axis=1))
```

```python
import functools
from typing import NamedTuple

import jax
import jax.numpy as jnp
from jax import lax
from jax.experimental import pallas as pl
from jax.experimental.pallas import tpu as pltpu

F32 = jnp.float32
BF16 = jnp.bfloat16

LANES = 128
HEAD_DIM = 128
GRID_W = 64
WIN_H = 8
WIN_W = 16
POOL_WINDOWS = (2, 4, 8, 16)
RMS_EPS = 1e-6
NEG_INF = -1e30

Q_ROWS = 4
Q_BLOCK = Q_ROWS * GRID_W
SLAB_BLOCKS = 3
SLAB_ROWS = SLAB_BLOCKS * Q_ROWS
N_REL_ROWS = 2 * WIN_H
N_REL_COLS = 2 * WIN_W - 1
POOL_PAD = 8

VMEM_LIMIT = 58 * 1024 * 1024
MM_TILE = 1024
NORM_ROWS = 16


def _cparams(*sem):
    return pltpu.CompilerParams(dimension_semantics=sem, vmem_limit_bytes=VMEM_LIMIT)


def _silu(x):
    return x / (1.0 + jnp.exp(-x))


def _ada_kernel(cond_ref, w_ref, b_ref, o_ref):
    s = _silu(cond_ref[...]).astype(BF16)
    o_ref[...] = jnp.dot(s, w_ref[...].astype(BF16), preferred_element_type=F32) + b_ref[...]


def _ada_modulation(cond, w_ada, b_ada):
    depth, d, d3 = w_ada.shape
    r = cond.shape[0]
    tn = min(512, d3)
    return pl.pallas_call(
        _ada_kernel,
        grid=(depth, d3 // tn),
        in_specs=[
            pl.BlockSpec((r, d), lambda i, n: (0, 0)),
            pl.BlockSpec((None, d, tn), lambda i, n: (i, 0, n)),
            pl.BlockSpec((None, 1, tn), lambda i, n: (i, 0, n)),
        ],
        out_specs=pl.BlockSpec((None, r, tn), lambda i, n: (i, 0, n)),
        out_shape=jax.ShapeDtypeStruct((depth, r, d3), F32),
        compiler_params=_cparams("parallel", "parallel"),
        name="ada_modulation",
    )(cond, w_ada, b_ada.reshape(depth, 1, d3))


def _rms(x):
    return x * lax.rsqrt(jnp.mean(x * x, axis=-1, keepdims=True) + RMS_EPS)


def _modulate_kernel(x_ref, g_ref, sh_ref, sc_ref, o_ref, rinv_ref, gain_ref, shift_ref):
    x = x_ref[...]
    rinv = lax.rsqrt(jnp.mean(x * x, axis=-1, keepdims=True) + RMS_EPS)
    rinv_ref[...] = jnp.broadcast_to(rinv, rinv_ref.shape)
    gain_ref[...] = jnp.broadcast_to(g_ref[...] * (1.0 + sc_ref[...]), gain_ref.shape)
    shift_ref[...] = jnp.broadcast_to(sh_ref[...], shift_ref.shape)

    def chunk(i, carry):
        rows = pl.ds(pl.multiple_of(i * NORM_ROWS, NORM_ROWS), NORM_ROWS)
        r = rinv_ref[rows, :]
        for c in range(x_ref.shape[1] // LANES):
            cols = slice(c * LANES, (c + 1) * LANES)
            y = x_ref[rows, cols] * r * gain_ref[:, cols] + shift_ref[:, cols]
            o_ref[rows, cols] = y.astype(BF16)
        return carry

    lax.fori_loop(0, x_ref.shape[0] // NORM_ROWS, chunk, 0)


def _mod_row_map(layer, part, seq_len, rows_per_step, row0, per_batch, ncol_axis):
    def index(*ids):
        m = ids[0]
        row = row0 + (m * rows_per_step) // seq_len if per_batch else row0
        col = part if ncol_axis is None else part * ncol_axis[1] + ids[ncol_axis[0]]
        return (layer, row, 0, col)
    return index


def _modulate(x2d, g_row, mod4, layer, seq_len, row0, per_batch):
    m, d = x2d.shape
    tt = min(256, seq_len)
    return pl.pallas_call(
        _modulate_kernel,
        grid=(m // tt,),
        in_specs=[
            pl.BlockSpec((tt, d), lambda i: (i, 0)),
            pl.BlockSpec((None, 1, d), lambda i: (layer, 0, 0)),
            pl.BlockSpec((None, None, 1, d), _mod_row_map(layer, 0, seq_len, tt, row0, per_batch, None)),
            pl.BlockSpec((None, None, 1, d), _mod_row_map(layer, 1, seq_len, tt, row0, per_batch, None)),
        ],
        out_specs=pl.BlockSpec((tt, d), lambda i: (i, 0)),
        out_shape=jax.ShapeDtypeStruct((m, d), BF16),
        scratch_shapes=[pltpu.VMEM((tt, LANES), F32), pltpu.VMEM((NORM_ROWS, d), F32),
                        pltpu.VMEM((NORM_ROWS, d), F32)],
        compiler_params=_cparams("parallel"),
        name="modulate",
    )(x2d, g_row, mod4, mod4)


def _final_norm_kernel(x_ref, g_ref, o_ref):
    o_ref[...] = _rms(x_ref[...]) * g_ref[...]


def _final_norm(x2d, g):
    m, d = x2d.shape
    tt = min(256, m)
    return pl.pallas_call(
        _final_norm_kernel,
        grid=(m // tt,),
        in_specs=[pl.BlockSpec((tt, d), lambda i: (i, 0)), pl.BlockSpec((1, d), lambda i: (0, 0))],
        out_specs=pl.BlockSpec((tt, d), lambda i: (i, 0)),
        out_shape=jax.ShapeDtypeStruct((m, d), F32),
        compiler_params=_cparams("parallel"),
        name="final_norm",
    )(x2d, g.reshape(1, d))


def _dot(a_ref, w_ref):
    return jnp.dot(a_ref[...], w_ref[...], preferred_element_type=F32)


def _mm_cast_kernel(a_ref, w_ref, o_ref, *, scale):
    acc = _dot(a_ref, w_ref)
    if scale is not None:
        acc = acc * scale
    o_ref[...] = acc.astype(o_ref.dtype)


def _mm_silu_kernel(a_ref, w_ref, o_ref):
    o_ref[...] = _silu(_dot(a_ref, w_ref)).astype(o_ref.dtype)


def _window_sum(u, window):
    t, n = u.shape
    zeros = jnp.zeros((POOL_PAD, n), F32)
    up = jnp.concatenate([zeros, u, zeros], axis=0)
    size = t + 2 * POOL_PAD

    def ahead(x, k):
        return pltpu.roll(x, (-k) % size, axis=0)

    fwd, span = up, 1
    while 2 * span < window:
        fwd = fwd + ahead(fwd, span)
        span *= 2
    s = fwd + ahead(fwd, -span)
    return s[POOL_PAD:POOL_PAD + t, :]


def _mm_pool_kernel(a_ref, w_ref, o_ref, *, seq_len, cols_per_group):
    acc = _dot(a_ref, w_ref)
    tm, tn = acc.shape
    group = (pl.program_id(1) * tn) // cols_per_group
    t = lax.broadcasted_iota(jnp.int32, (seq_len, 1), 0)
    for gi, window in enumerate(POOL_WINDOWS):
        @pl.when(group == gi)
        def _(window=window):
            half = window // 2
            count = jnp.minimum(t + half, seq_len) - jnp.maximum(t - half, 0)
            inv_count = 1.0 / count.astype(F32)
            for s in range(tm // seq_len):
                u = acc[s * seq_len:(s + 1) * seq_len, :]
                d = _window_sum(u, window) * inv_count - u
                o_ref[s * seq_len:(s + 1) * seq_len, :] = d.astype(o_ref.dtype)


def _mm_group_kernel(d_ref, w_ref, ps_ref, sz_ref, o_ref):
    acc = _dot(d_ref, w_ref)
    o_ref[...] = (acc * ps_ref[...] * sz_ref[...].astype(F32)).astype(o_ref.dtype)


def _mm_resid_kernel(a_ref, w_ref, x_ref, gate_ref, o_ref):
    o_ref[...] = x_ref[...] + gate_ref[...] * _dot(a_ref, w_ref)


def _tiles(m, n):
    return min(MM_TILE, m), min(MM_TILE, n)


class _Weight(NamedTuple):
    stack: jax.Array
    layer: int


def _matmul(kern, a, w, w_col0, n_out, out_dtype, name, extra_in=(), extra_specs=(), tn_max=None):
    m, k = a.shape
    tm, tn = _tiles(m, n_out)
    if tn_max is not None:
        tn = min(tn, tn_max)
    nb0 = w_col0 // tn
    return pl.pallas_call(
        kern,
        grid=(m // tm, n_out // tn),
        in_specs=[
            pl.BlockSpec((tm, k), lambda i, j: (i, 0)),
            pl.BlockSpec((None, k, tn), lambda i, j: (w.layer, 0, nb0 + j)),
            *[spec(tm, tn) for spec in extra_specs],
        ],
        out_specs=pl.BlockSpec((tm, tn), lambda i, j: (i, j)),
        out_shape=jax.ShapeDtypeStruct((m, n_out), out_dtype),
        compiler_params=_cparams("parallel", "parallel"),
        name=name,
    )(a, w.stack, *extra_in)


def _cache_tile_copies(stage, sem, cache_hbm, slot, layer, batch0, head0):
    n_batch, _, tn = stage.shape[1:]
    return [pltpu.make_async_copy(stage.at[slot, :, :, pl.ds(hh * HEAD_DIM, HEAD_DIM)],
                                  cache_hbm.at[pl.ds(batch0, n_batch), layer, :, head0 + hh, :],
                                  sem.at[slot, hh]) for hh in range(tn // HEAD_DIM)]


def _mm_cache_kernel(a_ref, w_ref, cache_in_hbm, lo_ref, cache_hbm, stage, sem, *, layer, n_steps):
    del cache_in_hbm
    n_cols = pl.num_programs(1)
    step = pl.program_id(0) * n_cols + pl.program_id(1)
    slot = step % 2
    n_batch, _, tn = stage.shape[1:]

    def copies(s, which):
        return _cache_tile_copies(stage, sem, cache_hbm, which, layer,
                                  (s // n_cols) * n_batch, (s % n_cols) * (tn // HEAD_DIM))

    @pl.when(step >= 2)
    def _():
        for cp in copies(step - 2, slot):
            cp.wait()

    acc = _dot(a_ref, w_ref)
    lo_ref[...] = acc.astype(lo_ref.dtype)
    stage[slot] = acc.reshape(stage.shape[1:])
    for cp in copies(step, slot):
        cp.start()

    @pl.when(step == n_steps - 1)
    def _():
        if n_steps > 1:
            for cp in copies(step - 1, 1 - slot):
                cp.wait()
        for cp in copies(step, slot):
            cp.wait()


def _matmul_to_cache(a, w, w_col0, cache, layer, name):
    m, k = a.shape
    _, _, seq_len, n_heads, _ = cache.shape
    d = n_heads * HEAD_DIM
    tm, tn = _tiles(m, d)
    assert tm % seq_len == 0 and tn % HEAD_DIM == 0
    grid = (m // tm, d // tn)
    nb0 = w_col0 // tn
    any_spec = pl.BlockSpec(memory_space=pl.ANY)
    heads = tn // HEAD_DIM
    return pl.pallas_call(
        functools.partial(_mm_cache_kernel, layer=layer, n_steps=grid[0] * grid[1]),
        grid=grid,
        in_specs=[pl.BlockSpec((tm, k), lambda i, j: (i, 0)),
                  pl.BlockSpec((None, k, tn), lambda i, j: (w.layer, 0, nb0 + j)),
                  any_spec],
        out_specs=[pl.BlockSpec((tm, tn), lambda i, j: (i, j)), any_spec],
        out_shape=[jax.ShapeDtypeStruct((m, d), BF16), jax.ShapeDtypeStruct(cache.shape, F32)],
        scratch_shapes=[pltpu.VMEM((2, tm // seq_len, seq_len, tn), F32),
                        pltpu.SemaphoreType.DMA((2, heads))],
        input_output_aliases={2: 1},
        compiler_params=_cparams("arbitrary", "arbitrary"),
        name=name,
    )(a, w.stack, cache)


def _group_matmul(d, w_grp, pool_scale_row, sz):
    m, dm = d.shape
    _, g, dg, _ = w_grp.stack.shape
    tm = min(MM_TILE, m)
    return pl.pallas_call(
        _mm_group_kernel,
        grid=(m // tm, g),
        in_specs=[
            pl.BlockSpec((tm, dg), lambda i, j: (i, j)),
            pl.BlockSpec((None, None, dg, dg), lambda i, j: (w_grp.layer, j, 0, 0)),
            pl.BlockSpec((1, dg), lambda i, j: (0, j)),
            pl.BlockSpec((tm, dg), lambda i, j: (i, j)),
        ],
        out_specs=pl.BlockSpec((tm, dg), lambda i, j: (i, j)),
        out_shape=jax.ShapeDtypeStruct((m, dm), BF16),
        compiler_params=_cparams("parallel", "parallel"),
        name="pool_group_matmul",
    )(d, w_grp.stack, pool_scale_row, sz)


def _out_proj_residual(a, w_out, x2d, mod4, layer, seq_len, row0, per_batch):
    m, d = x2d.shape
    tm, tn = _tiles(m, d)
    x_spec = lambda tm, tn: pl.BlockSpec((tm, tn), lambda i, j: (i, j))
    gate_spec = lambda tm, tn: pl.BlockSpec(
        (None, None, 1, tn), _mod_row_map(layer, 2, seq_len, tm, row0, per_batch, (1, d // tn)))
    return _matmul(_mm_resid_kernel, a, w_out, 0, d, F32, "out_proj_residual",
                   extra_in=(x2d, mod4), extra_specs=(x_spec, gate_spec))


def _softmax_exps(scores):
    m = scores[0].max(axis=-1, keepdims=True)
    for s in scores[1:]:
        m = jnp.maximum(m, s.max(axis=-1, keepdims=True))
    return [jnp.exp(s - m).astype(BF16) for s in scores]


def _qk(q, k):
    return lax.dot_general(q, k, (((1,), (1,)), ((), ())), preferred_element_type=F32)


def _with_ones(v):
    return jnp.concatenate([v, jnp.ones(v.shape, v.dtype)], axis=1)


def _normalise_gate(o_ext, z):
    o = o_ext[:, :HEAD_DIM] / o_ext[:, HEAD_DIM:]
    return o * _silu(z.astype(F32))


def _attn_ctx_kernel(q_ref, k_ref, v_ref, z_ref, o_ref, *, heads):
    for hh in range(heads):
        sl = slice(hh * HEAD_DIM, (hh + 1) * HEAD_DIM)
        k = k_ref[:, sl].astype(BF16)
        v = _with_ones(v_ref[:, sl].astype(BF16))
        (e,) = _softmax_exps([_qk(q_ref[:, sl], k)])
        o_ext = jnp.dot(e, v, preferred_element_type=F32)
        o_ref[:, sl] = _normalise_gate(o_ext, z_ref[:, sl]).astype(o_ref.dtype)


def _heads_per_step(n_heads):
    return min(8, n_heads)


def _attn_context(q, k, v, z, seq_len):
    m, d = q.shape
    hg = _heads_per_step(d // HEAD_DIM)
    wblk = hg * HEAD_DIM
    spec = pl.BlockSpec((seq_len, wblk), lambda b, h: (b, h))
    return pl.pallas_call(
        functools.partial(_attn_ctx_kernel, heads=hg),
        grid=(m // seq_len, d // wblk),
        in_specs=[spec, spec, spec, spec],
        out_specs=spec,
        out_shape=jax.ShapeDtypeStruct((m, d), BF16),
        compiler_params=_cparams("parallel", "parallel"),
        name="attn_context",
    )(q, k, v, z)


def _rel_bias_kernel(rpb_ref, o_ref):
    rows, lanes = o_ref.shape
    lane = lax.broadcasted_iota(jnp.int32, (1, lanes), 1)
    qc = lane // (2 * GRID_W)
    kc = lane % GRID_W
    rel = jnp.clip(kc - qc, -(WIN_W - 1), WIN_W - 1) + (WIN_W - 1)
    start = jnp.clip(qc - WIN_W // 2, 0, GRID_W - WIN_W)
    in_window = (kc >= start) & (kc < start + WIN_W)
    table = rpb_ref[...]
    acc = jnp.zeros((rows, lanes), F32)
    for j in range(N_REL_COLS):
        acc = jnp.where(rel == j, table[:, j:j + 1], acc)
    row = lax.broadcasted_iota(jnp.int32, (rows, 1), 0)
    masked_row = (row % N_REL_ROWS) == (N_REL_ROWS - 1)
    o_ref[...] = jnp.where(in_window & jnp.logical_not(masked_row), acc, NEG_INF)


def _rel_bias_table(rpb_layer):
    h = rpb_layer.shape[0]
    padded = jnp.pad(rpb_layer, ((0, 0), (0, 1), (0, 0))).reshape(h * N_REL_ROWS, N_REL_COLS)
    rows = h * N_REL_ROWS
    tr = min(64, rows)
    lanes = GRID_W * 2 * GRID_W
    flat = pl.pallas_call(
        _rel_bias_kernel,
        grid=(rows // tr,),
        in_specs=[pl.BlockSpec((tr, N_REL_COLS), lambda i: (i, 0))],
        out_specs=pl.BlockSpec((tr, lanes), lambda i: (i, 0)),
        out_shape=jax.ShapeDtypeStruct((rows, lanes), F32),
        compiler_params=_cparams("parallel"),
        name="rel_bias_table",
    )(padded)
    return flat.reshape(h, N_REL_ROWS, GRID_W, 2 * GRID_W)


def _cache_copies(cache_hbm, buf, sem, layer, b, head0, slot, heads):
    return [pltpu.make_async_copy(cache_hbm.at[b, layer, :, head0 + hh, :], buf.at[slot, hh],
                                  sem.at[slot, hh]) for hh in range(heads)]


def _attn_lat_kernel(q_ref, k0_ref, k1_ref, k2_ref, v0_ref, v1_ref, v2_ref, kc_hbm, vc_hbm, z_ref,
                     tab_ref, o_ref, bias_ref, kc_buf, vc_buf, kc_sem, vc_sem,
                     *, heads, grid_rows, layer):
    hgi, rb, b = pl.program_id(0), pl.program_id(1), pl.program_id(2)
    n_rb, n_b = pl.num_programs(1), pl.num_programs(2)
    step = (hgi * n_rb + rb) * n_b + b
    n_steps = pl.num_programs(0) * n_rb * n_b
    slot = step % 2

    def cache_copies(head_group, batch, which):
        return (_cache_copies(kc_hbm, kc_buf, kc_sem, layer, batch, head_group * heads, which, heads)
                + _cache_copies(vc_hbm, vc_buf, vc_sem, layer, batch, head_group * heads, which, heads))

    @pl.when(step == 0)
    def _():
        for cp in cache_copies(hgi, b, slot):
            cp.start()

    @pl.when(step + 1 < n_steps)
    def _():
        last_b = b == n_b - 1
        next_b = jnp.where(last_b, 0, b + 1)
        next_hgi = jnp.where(last_b & (rb == n_rb - 1), hgi + 1, hgi)
        for cp in cache_copies(next_hgi, next_b, 1 - slot):
            cp.start()

    for cp in cache_copies(hgi, b, slot):
        cp.wait()

    slab0 = Q_ROWS * jnp.clip(rb - 1, 0, grid_rows // Q_ROWS - SLAB_BLOCKS)

    @pl.when(pl.program_id(2) == 0)
    def _():
        left = lax.broadcasted_iota(jnp.int32, (GRID_W, 2 * GRID_W), 1) < GRID_W
        for rq in range(Q_ROWS):
            r = rb * Q_ROWS + rq
            win0 = jnp.clip(r - WIN_H // 2, 0, grid_rows - WIN_H)

            def entry(kk):
                kr = slab0 + kk
                ok = (kr >= win0) & (kr < win0 + WIN_H)
                return jnp.where(ok, kr - r + (WIN_H - 1), N_REL_ROWS - 1)

            for kk in range(0, SLAB_ROWS, 2):
                i_left, i_right = entry(kk), entry(kk + 1)
                for hh in range(heads):
                    tile = jnp.where(left, tab_ref[hh, i_left], tab_ref[hh, i_right])
                    bias_ref[hh, rq * GRID_W:(rq + 1) * GRID_W, kk * GRID_W:(kk + 2) * GRID_W] = tile

    k_refs = (k0_ref, k1_ref, k2_ref)
    v_refs = (v0_ref, v1_ref, v2_ref)
    for hh in range(heads):
        sl = slice(hh * HEAD_DIM, (hh + 1) * HEAD_DIM)
        q = q_ref[:, sl]
        scores = [_qk(q, k_refs[p][:, sl]) + bias_ref[hh, :, p * Q_BLOCK:(p + 1) * Q_BLOCK]
                  for p in range(SLAB_BLOCKS)]
        scores.append(_qk(q, kc_buf[slot, hh].astype(BF16)))
        exps = _softmax_exps(scores)
        o_ext = jnp.dot(exps[-1], _with_ones(vc_buf[slot, hh].astype(BF16)), preferred_element_type=F32)
        for p in range(SLAB_BLOCKS):
            o_ext = o_ext + jnp.dot(exps[p], _with_ones(v_refs[p][:, sl]), preferred_element_type=F32)
        o_ref[:, sl] = _normalise_gate(o_ext, z_ref[:, sl]).astype(o_ref.dtype)


def _attn_latent(q, k, v, z, k_ctx, v_ctx, layer, bias_tab, seq_len):
    m, d = q.shape
    batch = m // seq_len
    grid_rows = seq_len // GRID_W
    n_rb = grid_rows // Q_ROWS
    assert grid_rows % Q_ROWS == 0 and n_rb >= SLAB_BLOCKS
    past = k_ctx.shape[2]
    hg = _heads_per_step(d // HEAD_DIM)
    wblk = hg * HEAD_DIM

    q_spec = pl.BlockSpec((Q_BLOCK, wblk), lambda h, rb, b: (b * n_rb + rb, h))

    def slab_spec(p):
        return pl.BlockSpec(
            (Q_BLOCK, wblk),
            lambda h, rb, b: (b * n_rb + jnp.clip(rb - 1, 0, n_rb - SLAB_BLOCKS) + p, h))

    ctx_spec = pl.BlockSpec(memory_space=pl.ANY)
    tab_spec = pl.BlockSpec((hg, N_REL_ROWS, GRID_W, 2 * GRID_W), lambda h, rb, b: (h, 0, 0, 0))
    return pl.pallas_call(
        functools.partial(_attn_lat_kernel, heads=hg, grid_rows=grid_rows, layer=layer),
        grid=(d // wblk, n_rb, batch),
        in_specs=[q_spec, slab_spec(0), slab_spec(1), slab_spec(2),
                  slab_spec(0), slab_spec(1), slab_spec(2),
                  ctx_spec, ctx_spec, q_spec, tab_spec],
        out_specs=q_spec,
        out_shape=jax.ShapeDtypeStruct((m, d), BF16),
        scratch_shapes=[pltpu.VMEM((hg, Q_BLOCK, SLAB_BLOCKS * Q_BLOCK), F32),
                        pltpu.VMEM((2, hg, past, HEAD_DIM), F32),
                        pltpu.VMEM((2, hg, past, HEAD_DIM), F32),
                        pltpu.SemaphoreType.DMA((2, hg)),
                        pltpu.SemaphoreType.DMA((2, hg))],
        compiler_params=_cparams("arbitrary", "arbitrary", "arbitrary"),
        name="attn_latent",
    )(q, k, k, k, v, v, v, k_ctx, v_ctx, z, bias_tab)


def _attn_projections(h, w_in, d, new_caches=None, layer=0):
    scale = HEAD_DIM ** -0.5
    q = _matmul(functools.partial(_mm_cast_kernel, scale=scale), h, w_in, 0, d, BF16, "attn_in_q")
    plain = functools.partial(_mm_cast_kernel, scale=None)
    if new_caches is None:
        k = _matmul(plain, h, w_in, d, d, BF16, "attn_in_k")
        v = _matmul(plain, h, w_in, 2 * d, d, BF16, "attn_in_v")
    else:
        k, new_k = _matmul_to_cache(h, w_in, d, new_caches[0], layer, "attn_in_k_cache")
        v, new_v = _matmul_to_cache(h, w_in, 2 * d, new_caches[1], layer, "attn_in_v_cache")
        new_caches = (new_k, new_v)
    z = _matmul(plain, h, w_in, 3 * d, d, BF16, "attn_in_z")
    return q, k, v, z, new_caches


def _pool_branch(h, w_in, w_grp, pool_scale_row, d, seq_len):
    dg = d // len(POOL_WINDOWS)
    pooled = _matmul(functools.partial(_mm_pool_kernel, seq_len=seq_len, cols_per_group=dg),
                     h, w_in, 0, d, BF16, "pool_in_u", tn_max=dg)
    sz = _matmul(_mm_silu_kernel, h, w_in, d, d, BF16, "pool_in_z")
    return _group_matmul(pooled, w_grp, pool_scale_row, sz)


def kernel(x_prompt, x_sample, c, cache_k, cache_v, c_ctx, norm_g, w_ada, b_ada, w_in_attn, rpb,
           w_out_attn, w_in_pool, w_grp_pool, pool_scale, w_out_pool, final_norm_g):
    bc, tc, d = x_prompt.shape
    bl, tl, _ = x_sample.shape
    depth = norm_g.shape[0]
    n_heads = cache_k.shape[3]
    past = cache_k.shape[2]
    n_attn = w_in_attn.shape[0]
    assert n_heads * HEAD_DIM == d and tl % (GRID_W * Q_ROWS) == 0

    n_cond = -(-(1 + bl) // 8) * 8
    cond = jnp.concatenate([c_ctx[None, :], c, jnp.zeros((n_cond - 1 - bl, d), F32)], axis=0)
    mod = _ada_modulation(cond, w_ada, b_ada)
    mod4 = mod.reshape(depth, n_cond, 1, 3 * d)

    xc = x_prompt.reshape(bc * tc, d)
    xl = x_sample.reshape(bl * tl, d)
    g_rows = norm_g.reshape(depth, 1, d)

    w_in_attn_b = w_in_attn.astype(BF16)
    w_out_attn_b = w_out_attn.astype(BF16)
    w_in_pool_b = w_in_pool.astype(BF16)
    w_grp_pool_b = w_grp_pool.astype(BF16)
    w_out_pool_b = w_out_pool.astype(BF16)

    cache_shape = (bc, n_attn, tc, n_heads, HEAD_DIM)
    new_caches = (jnp.zeros(cache_shape, F32), jnp.zeros(cache_shape, F32))
    for i in range(depth):
        j = i // 2
        hc = _modulate(xc, g_rows, mod4, i, tc, 0, False)
        hl = _modulate(xl, g_rows, mod4, i, tl, 1, True)
        if i % 2 == 0:
            w_in = _Weight(w_in_attn_b, j)
            qc, kc, vc, zc, new_caches = _attn_projections(hc, w_in, d, new_caches, j)
            oc = _attn_context(qc, kc, vc, zc, tc)
            ql, kl, vl, zl, _ = _attn_projections(hl, w_in, d)
            bias_tab = _rel_bias_table(rpb[j])
            ol = _attn_latent(ql, kl, vl, zl, cache_k, cache_v, j, bias_tab, tl)
            w_out = _Weight(w_out_attn_b, j)
        else:
            w_in = _Weight(w_in_pool_b, j)
            w_grp = _Weight(w_grp_pool_b, j)
            ps_row = pool_scale[j].reshape(1, d)
            oc = _pool_branch(hc, w_in, w_grp, ps_row, d, tc)
            ol = _pool_branch(hl, w_in, w_grp, ps_row, d, tl)
            w_out = _Weight(w_out_pool_b, j)
        xc = _out_proj_residual(oc, w_out, xc, mod4, i, tc, 0, False)
        xl = _out_proj_residual(ol, w_out, xl, mod4, i, tl, 1, True)

    y_prompt = _final_norm(xc, final_norm_g).reshape(bc, tc, d)
    y_sample = _final_norm(xl, final_norm_g).reshape(bl, tl, d)
    return (y_prompt, y_sample, new_caches[0], new_caches[1])
```

```python
import functools
from typing import NamedTuple

import jax
import jax.numpy as jnp
from jax import lax
from jax.experimental import pallas as pl
from jax.experimental.pallas import tpu as pltpu

F32 = jnp.float32
BF16 = jnp.bfloat16

LANES = 128
HEAD_DIM = 128
GRID_W = 64
WIN_H = 8
WIN_W = 16
POOL_WINDOWS = (2, 4, 8, 16)
RMS_EPS = 1e-6
NEG_INF = -1e30

Q_ROWS = 4
Q_BLOCK = Q_ROWS * GRID_W
SLAB_BLOCKS = 3
SLAB_ROWS = SLAB_BLOCKS * Q_ROWS
N_REL_ROWS = 2 * WIN_H
N_REL_COLS = 2 * WIN_W - 1
POOL_PAD = 8

VMEM_LIMIT = 58 * 1024 * 1024
MM_TILE = 1024
NORM_ROWS = 16


def _cparams(*sem):
    return pltpu.CompilerParams(dimension_semantics=sem, vmem_limit_bytes=VMEM_LIMIT)


def _silu(x):
    return x / (1.0 + jnp.exp(-x))


def _ada_kernel(cond_ref, w_ref, b_ref, o_ref):
    s = _silu(cond_ref[...]).astype(BF16)
    o_ref[...] = jnp.dot(s, w_ref[...].astype(BF16), preferred_element_type=F32) + b_ref[...]


def _ada_modulation(cond, w_ada, b_ada):
    depth, d, d3 = w_ada.shape
    r = cond.shape[0]
    tn = min(512, d3)
    return pl.pallas_call(
        _ada_kernel,
        grid=(depth, d3 // tn),
        in_specs=[
            pl.BlockSpec((r, d), lambda i, n: (0, 0)),
            pl.BlockSpec((None, d, tn), lambda i, n: (i, 0, n)),
            pl.BlockSpec((None, 1, tn), lambda i, n: (i, 0, n)),
        ],
        out_specs=pl.BlockSpec((None, r, tn), lambda i, n: (i, 0, n)),
        out_shape=jax.ShapeDtypeStruct((depth, r, d3), F32),
        compiler_params=_cparams("parallel", "parallel"),
        name="ada_modulation",
    )(cond, w_ada, b_ada.reshape(depth, 1, d3))


def _rms(x):
    return x * lax.rsqrt(jnp.mean(x * x, axis=-1, keepdims=True) + RMS_EPS)


def _modulate_kernel(x_ref, g_ref, sh_ref, sc_ref, o_ref, rinv_ref, gain_ref, shift_ref):
    x = x_ref[...]
    rinv = lax.rsqrt(jnp.mean(x * x, axis=-1, keepdims=True) + RMS_EPS)
    rinv_ref[...] = jnp.broadcast_to(rinv, rinv_ref.shape)
    gain_ref[...] = jnp.broadcast_to(g_ref[...] * (1.0 + sc_ref[...]), gain_ref.shape)
    shift_ref[...] = jnp.broadcast_to(sh_ref[...], shift_ref.shape)

    def chunk(i, carry):
        rows = pl.ds(pl.multiple_of(i * NORM_ROWS, NORM_ROWS), NORM_ROWS)
        r = rinv_ref[rows, :]
        for c in range(x_ref.shape[1] // LANES):
            cols = slice(c * LANES, (c + 1) * LANES)
            y = x_ref[rows, cols] * r * gain_ref[:, cols] + shift_ref[:, cols]
            o_ref[rows, cols] = y.astype(BF16)
        return carry

    lax.fori_loop(0, x_ref.shape[0] // NORM_ROWS, chunk, 0)


def _mod_row_map(layer, part, seq_len, rows_per_step, row0, per_batch, ncol_axis):
    def index(*ids):
        m = ids[0]
        row = row0 + (m * rows_per_step) // seq_len if per_batch else row0
        col = part if ncol_axis is None else part * ncol_axis[1] + ids[ncol_axis[0]]
        return (layer, row, 0, col)
    return index


def _modulate(x2d, g_row, mod4, layer, seq_len, row0, per_batch):
    m, d = x2d.shape
    tt = min(256, seq_len)
    return pl.pallas_call(
        _modulate_kernel,
        grid=(m // tt,),
        in_specs=[
            pl.BlockSpec((tt, d), lambda i: (i, 0)),
            pl.BlockSpec((None, 1, d), lambda i: (layer, 0, 0)),
            pl.BlockSpec((None, None, 1, d), _mod_row_map(layer, 0, seq_len, tt, row0, per_batch, None)),
            pl.BlockSpec((None, None, 1, d), _mod_row_map(layer, 1, seq_len, tt, row0, per_batch, None)),
        ],
        out_specs=pl.BlockSpec((tt, d), lambda i: (i, 0)),
        out_shape=jax.ShapeDtypeStruct((m, d), BF16),
        scratch_shapes=[pltpu.VMEM((tt, LANES), F32), pltpu.VMEM((NORM_ROWS, d), F32),
                        pltpu.VMEM((NORM_ROWS, d), F32)],
        compiler_params=_cparams("parallel"),
        name="modulate",
    )(x2d, g_row, mod4, mod4)


def _final_norm_kernel(x_ref, g_ref, o_ref):
    o_ref[...] = _rms(x_ref[...]) * g_ref[...]


def _final_norm(x2d, g):
    m, d = x2d.shape
    tt = min(256, m)
    return pl.pallas_call(
        _final_norm_kernel,
        grid=(m // tt,),
        in_specs=[pl.BlockSpec((tt, d), lambda i: (i, 0)), pl.BlockSpec((1, d), lambda i: (0, 0))],
        out_specs=pl.BlockSpec((tt, d), lambda i: (i, 0)),
        out_shape=jax.ShapeDtypeStruct((m, d), F32),
        compiler_params=_cparams("parallel"),
        name="final_norm",
    )(x2d, g.reshape(1, d))


def _dot(a_ref, w_ref):
    return jnp.dot(a_ref[...], w_ref[...], preferred_element_type=F32)


def _mm_cast_kernel(a_ref, w_ref, o_ref, *, scale):
    acc = _dot(a_ref, w_ref)
    if scale is not None:
        acc = acc * scale
    o_ref[...] = acc.astype(o_ref.dtype)


def _mm_silu_kernel(a_ref, w_ref, o_ref):
    o_ref[...] = _silu(_dot(a_ref, w_ref)).astype(o_ref.dtype)


def _window_sum(u, window):
    t, n = u.shape
    zeros = jnp.zeros((POOL_PAD, n), F32)
    up = jnp.concatenate([zeros, u, zeros], axis=0)
    size = t + 2 * POOL_PAD

    def ahead(x, k):
        return pltpu.roll(x, (-k) % size, axis=0)

    fwd, span = up, 1
    while 2 * span < window:
        fwd = fwd + ahead(fwd, span)
        span *= 2
    s = fwd + ahead(fwd, -span)
    return s[POOL_PAD:POOL_PAD + t, :]


def _mm_pool_kernel(a_ref, w_ref, o_ref, *, seq_len, cols_per_group):
    tm, tn = o_ref.shape
    group = (pl.program_id(1) * tn) // cols_per_group
    t = lax.broadcasted_iota(jnp.int32, (seq_len, 1), 0)
    for gi, window in enumerate(POOL_WINDOWS):
        @pl.when(group == gi)
        def _(window=window):
            acc = _dot(a_ref, w_ref)
            half = window // 2
            count = jnp.minimum(t + half, seq_len) - jnp.maximum(t - half, 0)
            inv_count = 1.0 / count.astype(F32)
            for s in range(tm // seq_len):
                u = acc[s * seq_len:(s + 1) * seq_len, :]
                d = _window_sum(u, window) * inv_count - u
                o_ref[s * seq_len:(s + 1) * seq_len, :] = d.astype(o_ref.dtype)


def _mm_group_kernel(d_ref, w_ref, ps_ref, sz_ref, o_ref):
    acc = _dot(d_ref, w_ref)
    o_ref[...] = (acc * ps_ref[...] * sz_ref[...].astype(F32)).astype(o_ref.dtype)


def _mm_resid_kernel(a_ref, w_ref, x_ref, gate_ref, o_ref):
    o_ref[...] = x_ref[...] + gate_ref[...] * _dot(a_ref, w_ref)


def _tiles(m, n):
    return min(MM_TILE, m), min(MM_TILE, n)


class _Weight(NamedTuple):
    stack: jax.Array
    layer: int


def _matmul(kern, a, w, w_col0, n_out, out_dtype, name, extra_in=(), extra_specs=(), tn_max=None):
    m, k = a.shape
    tm, tn = _tiles(m, n_out)
    if tn_max is not None:
        tn = min(tn, tn_max)
    nb0 = w_col0 // tn
    return pl.pallas_call(
        kern,
        grid=(m // tm, n_out // tn),
        in_specs=[
            pl.BlockSpec((tm, k), lambda i, j: (i, 0)),
            pl.BlockSpec((None, k, tn), lambda i, j: (w.layer, 0, nb0 + j)),
            *[spec(tm, tn) for spec in extra_specs],
        ],
        out_specs=pl.BlockSpec((tm, tn), lambda i, j: (i, j)),
        out_shape=jax.ShapeDtypeStruct((m, n_out), out_dtype),
        compiler_params=_cparams("parallel", "parallel"),
        name=name,
    )(a, w.stack, *extra_in)


def _cache_tile_copies(stage, zeros, sem, cache_hbm, slot, layer, fill_layers, batch0, head0):
    n_batch, _, tn = stage.shape[1:]
    copies = []
    for hh in range(tn // HEAD_DIM):
        dst = lambda lyr: cache_hbm.at[pl.ds(batch0, n_batch), lyr, :, head0 + hh, :]
        copies.append(pltpu.make_async_copy(stage.at[slot, :, :, pl.ds(hh * HEAD_DIM, HEAD_DIM)],
                                            dst(layer), sem.at[slot, 0, hh]))
        for f, lyr in enumerate(fill_layers):
            copies.append(pltpu.make_async_copy(zeros, dst(lyr), sem.at[slot, 1 + f, hh]))
    return copies


def _mm_cache_kernel(*refs, layer, fill_layers, n_steps):
    a_ref, w_ref = refs[:2]
    lo_ref, cache_hbm, stage, zeros, sem = refs[-5:]
    n_cols = pl.num_programs(1)
    step = pl.program_id(0) * n_cols + pl.program_id(1)
    slot = step % 2
    n_batch, _, tn = stage.shape[1:]

    def copies(s, which):
        return _cache_tile_copies(stage, zeros, sem, cache_hbm, which, layer, fill_layers,
                                  (s // n_cols) * n_batch, (s % n_cols) * (tn // HEAD_DIM))

    if fill_layers:
        @pl.when(step == 0)
        def _():
            zeros[...] = jnp.zeros(zeros.shape, zeros.dtype)

    @pl.when(step >= 2)
    def _():
        for cp in copies(step - 2, slot):
            cp.wait()

    acc = _dot(a_ref, w_ref)
    lo_ref[...] = acc.astype(lo_ref.dtype)
    stage[slot] = acc.reshape(stage.shape[1:])
    for cp in copies(step, slot):
        cp.start()

    @pl.when(step == n_steps - 1)
    def _():
        if n_steps > 1:
            for cp in copies(step - 1, 1 - slot):
                cp.wait()
        for cp in copies(step, slot):
            cp.wait()


def _matmul_to_cache(a, w, w_col0, cache, cache_shape, layer, name):
    m, k = a.shape
    _, n_layers, seq_len, n_heads, _ = cache_shape
    d = n_heads * HEAD_DIM
    tm, tn = _tiles(m, d)
    assert tm % seq_len == 0 and tn % HEAD_DIM == 0
    grid = (m // tm, d // tn)
    nb0 = w_col0 // tn
    any_spec = pl.BlockSpec(memory_space=pl.ANY)
    fresh = cache is None
    fill_layers = tuple(l for l in range(n_layers) if l != layer) if fresh else ()
    n_batch = tm // seq_len
    return pl.pallas_call(
        functools.partial(_mm_cache_kernel, layer=layer, fill_layers=fill_layers,
                          n_steps=grid[0] * grid[1]),
        grid=grid,
        in_specs=[pl.BlockSpec((tm, k), lambda i, j: (i, 0)),
                  pl.BlockSpec((None, k, tn), lambda i, j: (w.layer, 0, nb0 + j))]
                 + ([] if fresh else [any_spec]),
        out_specs=[pl.BlockSpec((tm, tn), lambda i, j: (i, j)), any_spec],
        out_shape=[jax.ShapeDtypeStruct((m, d), BF16), jax.ShapeDtypeStruct(cache_shape, F32)],
        scratch_shapes=[pltpu.VMEM((2, n_batch, seq_len, tn), F32),
                        pltpu.VMEM((n_batch, seq_len, HEAD_DIM), F32),
                        pltpu.SemaphoreType.DMA((2, 1 + len(fill_layers), tn // HEAD_DIM))],
        input_output_aliases={} if fresh else {2: 1},
        compiler_params=_cparams("arbitrary", "arbitrary"),
        name=name,
    )(a, w.stack, *([] if fresh else [cache]))


def _group_matmul(d, w_grp, pool_scale_row, sz):
    m, dm = d.shape
    _, g, dg, _ = w_grp.stack.shape
    tm = min(MM_TILE, m)
    return pl.pallas_call(
        _mm_group_kernel,
        grid=(m // tm, g),
        in_specs=[
            pl.BlockSpec((tm, dg), lambda i, j: (i, j)),
            pl.BlockSpec((None, None, dg, dg), lambda i, j: (w_grp.layer, j, 0, 0)),
            pl.BlockSpec((1, dg), lambda i, j: (0, j)),
            pl.BlockSpec((tm, dg), lambda i, j: (i, j)),
        ],
        out_specs=pl.BlockSpec((tm, dg), lambda i, j: (i, j)),
        out_shape=jax.ShapeDtypeStruct((m, dm), BF16),
        compiler_params=_cparams("parallel", "parallel"),
        name="pool_group_matmul",
    )(d, w_grp.stack, pool_scale_row, sz)


def _out_proj_residual(a, w_out, x2d, mod4, layer, seq_len, row0, per_batch):
    m, d = x2d.shape
    tm, tn = _tiles(m, d)
    x_spec = lambda tm, tn: pl.BlockSpec((tm, tn), lambda i, j: (i, j))
    gate_spec = lambda tm, tn: pl.BlockSpec(
        (None, None, 1, tn), _mod_row_map(layer, 2, seq_len, tm, row0, per_batch, (1, d // tn)))
    return _matmul(_mm_resid_kernel, a, w_out, 0, d, F32, "out_proj_residual",
                   extra_in=(x2d, mod4), extra_specs=(x_spec, gate_spec))


def _softmax_exps(scores):
    m = scores[0].max(axis=-1, keepdims=True)
    for s in scores[1:]:
        m = jnp.maximum(m, s.max(axis=-1, keepdims=True))
    return [jnp.exp(s - m).astype(BF16) for s in scores]


def _qk(q, k):
    return lax.dot_general(q, k, (((1,), (1,)), ((), ())), preferred_element_type=F32)


def _with_ones(v):
    return jnp.concatenate([v, jnp.ones(v.shape, v.dtype)], axis=1)


def _normalise_gate(o_ext, z):
    o = o_ext[:, :HEAD_DIM] / o_ext[:, HEAD_DIM:]
    return o * _silu(z.astype(F32))


def _attn_ctx_kernel(q_ref, k_ref, v_ref, z_ref, o_ref, *, heads):
    for hh in range(heads):
        sl = slice(hh * HEAD_DIM, (hh + 1) * HEAD_DIM)
        k = k_ref[:, sl].astype(BF16)
        v = _with_ones(v_ref[:, sl].astype(BF16))
        (e,) = _softmax_exps([_qk(q_ref[:, sl], k)])
        o_ext = jnp.dot(e, v, preferred_element_type=F32)
        o_ref[:, sl] = _normalise_gate(o_ext, z_ref[:, sl]).astype(o_ref.dtype)


def _heads_per_step(n_heads):
    return min(8, n_heads)


def _attn_context(q, k, v, z, seq_len):
    m, d = q.shape
    hg = _heads_per_step(d // HEAD_DIM)
    wblk = hg * HEAD_DIM
    spec = pl.BlockSpec((seq_len, wblk), lambda b, h: (b, h))
    return pl.pallas_call(
        functools.partial(_attn_ctx_kernel, heads=hg),
        grid=(m // seq_len, d // wblk),
        in_specs=[spec, spec, spec, spec],
        out_specs=spec,
        out_shape=jax.ShapeDtypeStruct((m, d), BF16),
        compiler_params=_cparams("parallel", "parallel"),
        name="attn_context",
    )(q, k, v, z)


def _rel_bias_kernel(rpb_ref, o_ref):
    rows, lanes = o_ref.shape
    lane = lax.broadcasted_iota(jnp.int32, (1, lanes), 1)
    qc = lane // (2 * GRID_W)
    kc = lane % GRID_W
    rel = jnp.clip(kc - qc, -(WIN_W - 1), WIN_W - 1) + (WIN_W - 1)
    start = jnp.clip(qc - WIN_W // 2, 0, GRID_W - WIN_W)
    in_window = (kc >= start) & (kc < start + WIN_W)
    table = rpb_ref[...]
    acc = jnp.zeros((rows, lanes), F32)
    for j in range(N_REL_COLS):
        acc = jnp.where(rel == j, table[:, j:j + 1], acc)
    row = lax.broadcasted_iota(jnp.int32, (rows, 1), 0)
    masked_row = (row % N_REL_ROWS) == (N_REL_ROWS - 1)
    o_ref[...] = jnp.where(in_window & jnp.logical_not(masked_row), acc, NEG_INF)


def _rel_bias_table(rpb_layer):
    h = rpb_layer.shape[0]
    padded = jnp.pad(rpb_layer, ((0, 0), (0, 1), (0, 0))).reshape(h * N_REL_ROWS, N_REL_COLS)
    rows = h * N_REL_ROWS
    tr = min(64, rows)
    lanes = GRID_W * 2 * GRID_W
    flat = pl.pallas_call(
        _rel_bias_kernel,
        grid=(rows // tr,),
        in_specs=[pl.BlockSpec((tr, N_REL_COLS), lambda i: (i, 0))],
        out_specs=pl.BlockSpec((tr, lanes), lambda i: (i, 0)),
        out_shape=jax.ShapeDtypeStruct((rows, lanes), F32),
        compiler_params=_cparams("parallel"),
        name="rel_bias_table",
    )(padded)
    return flat.reshape(h, N_REL_ROWS, GRID_W, 2 * GRID_W)


def _cache_copies(cache_hbm, buf, sem, layer, b, head0, slot, heads):
    return [pltpu.make_async_copy(cache_hbm.at[b, layer, :, head0 + hh, :], buf.at[slot, hh],
                                  sem.at[slot, hh]) for hh in range(heads)]


def _attn_lat_kernel(q_ref, k0_ref, k1_ref, k2_ref, v0_ref, v1_ref, v2_ref, kc_hbm, vc_hbm, z_ref,
                     tab_ref, o_ref, bias_ref, kc_buf, vc_buf, kc_sem, vc_sem,
                     *, heads, grid_rows, layer):
    hgi, rb, b = pl.program_id(0), pl.program_id(1), pl.program_id(2)
    n_rb, n_b = pl.num_programs(1), pl.num_programs(2)
    step = (hgi * n_rb + rb) * n_b + b
    n_steps = pl.num_programs(0) * n_rb * n_b
    slot = step % 2

    def cache_copies(head_group, batch, which):
        return (_cache_copies(kc_hbm, kc_buf, kc_sem, layer, batch, head_group * heads, which, heads)
                + _cache_copies(vc_hbm, vc_buf, vc_sem, layer, batch, head_group * heads, which, heads))

    @pl.when(step == 0)
    def _():
        for cp in cache_copies(hgi, b, slot):
            cp.start()

    @pl.when(step + 1 < n_steps)
    def _():
        last_b = b == n_b - 1
        next_b = jnp.where(last_b, 0, b + 1)
        next_hgi = jnp.where(last_b & (rb == n_rb - 1), hgi + 1, hgi)
        for cp in cache_copies(next_hgi, next_b, 1 - slot):
            cp.start()

    for cp in cache_copies(hgi, b, slot):
        cp.wait()

    slab0 = Q_ROWS * jnp.clip(rb - 1, 0, grid_rows // Q_ROWS - SLAB_BLOCKS)

    @pl.when(pl.program_id(2) == 0)
    def _():
        left = lax.broadcasted_iota(jnp.int32, (GRID_W, 2 * GRID_W), 1) < GRID_W
        for rq in range(Q_ROWS):
            r = rb * Q_ROWS + rq
            win0 = jnp.clip(r - WIN_H // 2, 0, grid_rows - WIN_H)

            def entry(kk):
                kr = slab0 + kk
                ok = (kr >= win0) & (kr < win0 + WIN_H)
                return jnp.where(ok, kr - r + (WIN_H - 1), N_REL_ROWS - 1)

            for kk in range(0, SLAB_ROWS, 2):
                i_left, i_right = entry(kk), entry(kk + 1)
                for hh in range(heads):
                    tile = jnp.where(left, tab_ref[hh, i_left], tab_ref[hh, i_right])
                    bias_ref[hh, rq * GRID_W:(rq + 1) * GRID_W, kk * GRID_W:(kk + 2) * GRID_W] = tile

    k_refs = (k0_ref, k1_ref, k2_ref)
    v_refs = (v0_ref, v1_ref, v2_ref)

    def attend(pieces):
        for hh in range(heads):
            sl = slice(hh * HEAD_DIM, (hh + 1) * HEAD_DIM)
            q = q_ref[:, sl]
            scores = [_qk(q, k_refs[p][:, sl]) + bias_ref[hh, :, p * Q_BLOCK:(p + 1) * Q_BLOCK]
                      for p in pieces]
            scores.append(_qk(q, kc_buf[slot, hh].astype(BF16)))
            exps = _softmax_exps(scores)
            o_ext = jnp.dot(exps[-1], _with_ones(vc_buf[slot, hh].astype(BF16)),
                            preferred_element_type=F32)
            for e, p in zip(exps, pieces):
                o_ext = o_ext + jnp.dot(e, _with_ones(v_refs[p][:, sl]), preferred_element_type=F32)
            o_ref[:, sl] = _normalise_gate(o_ext, z_ref[:, sl]).astype(o_ref.dtype)

    all_pieces = tuple(range(SLAB_BLOCKS))
    pl.when(rb == 0)(lambda: attend(all_pieces[:-1]))
    pl.when(rb == n_rb - 1)(lambda: attend(all_pieces[1:]))
    pl.when((rb > 0) & (rb < n_rb - 1))(lambda: attend(all_pieces))


def _attn_latent(q, k, v, z, k_ctx, v_ctx, layer, bias_tab, seq_len):
    m, d = q.shape
    batch = m // seq_len
    grid_rows = seq_len // GRID_W
    n_rb = grid_rows // Q_ROWS
    assert grid_rows % Q_ROWS == 0 and n_rb >= SLAB_BLOCKS
    past = k_ctx.shape[2]
    hg = _heads_per_step(d // HEAD_DIM)
    wblk = hg * HEAD_DIM

    q_spec = pl.BlockSpec((Q_BLOCK, wblk), lambda h, rb, b: (b * n_rb + rb, h))

    def slab_spec(p):
        return pl.BlockSpec(
            (Q_BLOCK, wblk),
            lambda h, rb, b: (b * n_rb + jnp.clip(rb - 1, 0, n_rb - SLAB_BLOCKS) + p, h))

    ctx_spec = pl.BlockSpec(memory_space=pl.ANY)
    tab_spec = pl.BlockSpec((hg, N_REL_ROWS, GRID_W, 2 * GRID_W), lambda h, rb, b: (h, 0, 0, 0))
    return pl.pallas_call(
        functools.partial(_attn_lat_kernel, heads=hg, grid_rows=grid_rows, layer=layer),
        grid=(d // wblk, n_rb, batch),
        in_specs=[q_spec, slab_spec(0), slab_spec(1), slab_spec(2),
                  slab_spec(0), slab_spec(1), slab_spec(2),
                  ctx_spec, ctx_spec, q_spec, tab_spec],
        out_specs=q_spec,
        out_shape=jax.ShapeDtypeStruct((m, d), BF16),
        scratch_shapes=[pltpu.VMEM((hg, Q_BLOCK, SLAB_BLOCKS * Q_BLOCK), F32),
                        pltpu.VMEM((2, hg, past, HEAD_DIM), F32),
                        pltpu.VMEM((2, hg, past, HEAD_DIM), F32),
                        pltpu.SemaphoreType.DMA((2, hg)),
                        pltpu.SemaphoreType.DMA((2, hg))],
        compiler_params=_cparams("arbitrary", "arbitrary", "arbitrary"),
        name="attn_latent",
    )(q, k, k, k, v, v, v, k_ctx, v_ctx, z, bias_tab)


def _attn_projections(h, w_in, d, new_caches=None, cache_shape=None, layer=0):
    scale = HEAD_DIM ** -0.5
    q = _matmul(functools.partial(_mm_cast_kernel, scale=scale), h, w_in, 0, d, BF16, "attn_in_q")
    plain = functools.partial(_mm_cast_kernel, scale=None)
    if cache_shape is None:
        k = _matmul(plain, h, w_in, d, d, BF16, "attn_in_k")
        v = _matmul(plain, h, w_in, 2 * d, d, BF16, "attn_in_v")
    else:
        k, new_k = _matmul_to_cache(h, w_in, d, new_caches[0], cache_shape, layer, "attn_in_k_cache")
        v, new_v = _matmul_to_cache(h, w_in, 2 * d, new_caches[1], cache_shape, layer, "attn_in_v_cache")
        new_caches = (new_k, new_v)
    z = _matmul(plain, h, w_in, 3 * d, d, BF16, "attn_in_z")
    return q, k, v, z, new_caches


def _pool_branch(h, w_in, w_grp, pool_scale_row, d, seq_len):
    dg = d // len(POOL_WINDOWS)
    pooled = _matmul(functools.partial(_mm_pool_kernel, seq_len=seq_len, cols_per_group=dg),
                     h, w_in, 0, d, BF16, "pool_in_u", tn_max=dg)
    sz = _matmul(_mm_silu_kernel, h, w_in, d, d, BF16, "pool_in_z")
    return _group_matmul(pooled, w_grp, pool_scale_row, sz)


def kernel(x_prompt, x_sample, c, cache_k, cache_v, c_ctx, norm_g, w_ada, b_ada, w_in_attn, rpb,
           w_out_attn, w_in_pool, w_grp_pool, pool_scale, w_out_pool, final_norm_g):
    bc, tc, d = x_prompt.shape
    bl, tl, _ = x_sample.shape
    depth = norm_g.shape[0]
    n_heads = cache_k.shape[3]
    past = cache_k.shape[2]
    n_attn = w_in_attn.shape[0]
    assert n_heads * HEAD_DIM == d and tl % (GRID_W * Q_ROWS) == 0

    n_cond = -(-(1 + bl) // 8) * 8
    cond = jnp.concatenate([c_ctx[None, :], c, jnp.zeros((n_cond - 1 - bl, d), F32)], axis=0)
    mod = _ada_modulation(cond, w_ada, b_ada)
    mod4 = mod.reshape(depth, n_cond, 1, 3 * d)

    xc = x_prompt.reshape(bc * tc, d)
    xl = x_sample.reshape(bl * tl, d)
    g_rows = norm_g.reshape(depth, 1, d)

    w_in_attn_b = w_in_attn.astype(BF16)
    w_out_attn_b = w_out_attn.astype(BF16)
    w_in_pool_b = w_in_pool.astype(BF16)
    w_grp_pool_b = w_grp_pool.astype(BF16)
    w_out_pool_b = w_out_pool.astype(BF16)

    cache_shape = (bc, n_attn, tc, n_heads, HEAD_DIM)
    new_caches = (None, None)
    for i in range(depth):
        j = i // 2
        hc = _modulate(xc, g_rows, mod4, i, tc, 0, False)
        hl = _modulate(xl, g_rows, mod4, i, tl, 1, True)
        if i % 2 == 0:
            w_in = _Weight(w_in_attn_b, j)
            qc, kc, vc, zc, new_caches = _attn_projections(hc, w_in, d, new_caches, cache_shape, j)
            oc = _attn_context(qc, kc, vc, zc, tc)
            ql, kl, vl, zl, _ = _attn_projections(hl, w_in, d)
            bias_tab = _rel_bias_table(rpb[j])
            ol = _attn_latent(ql, kl, vl, zl, cache_k, cache_v, j, bias_tab, tl)
            w_out = _Weight(w_out_attn_b, j)
        else:
            w_in = _Weight(w_in_pool_b, j)
            w_grp = _Weight(w_grp_pool_b, j)
            ps_row = pool_scale[j].reshape(1, d)
            oc = _pool_branch(hc, w_in, w_grp, ps_row, d, tc)
            ol = _pool_branch(hl, w_in, w_grp, ps_row, d, tl)
            w_out = _Weight(w_out_pool_b, j)
        xc = _out_proj_residual(oc, w_out, xc, mod4, i, tc, 0, False)
        xl = _out_proj_residual(ol, w_out, xl, mod4, i, tl, 1, True)

    y_prompt = _final_norm(xc, final_norm_g).reshape(bc, tc, d)
    y_sample = _final_norm(xl, final_norm_g).reshape(bl, tl, d)
    return (y_prompt, y_sample, new_caches[0], new_caches[1])
```

```python
import functools
from typing import NamedTuple

import jax
import jax.numpy as jnp
from jax import lax
from jax.experimental import pallas as pl
from jax.experimental.pallas import tpu as pltpu

F32 = jnp.float32
BF16 = jnp.bfloat16

LANES = 128
HEAD_DIM = 128
GRID_W = 64
WIN_H = 8
WIN_W = 16
POOL_WINDOWS = (2, 4, 8, 16)
RMS_EPS = 1e-6
NEG_INF = -1e30

Q_ROWS = 4
Q_BLOCK = Q_ROWS * GRID_W
SLAB_BLOCKS = 3
SLAB_ROWS = SLAB_BLOCKS * Q_ROWS
N_REL_ROWS = 2 * WIN_H
N_REL_COLS = 2 * WIN_W - 1
POOL_PAD = 8

VMEM_LIMIT = 58 * 1024 * 1024
MM_TILE = 1024
CTX_HEADS_PER_STEP = 16
PACKED_ROWS = 16
NORM_ROWS = PACKED_ROWS


def _cparams(*sem):
    return pltpu.CompilerParams(dimension_semantics=sem, vmem_limit_bytes=VMEM_LIMIT)


def _silu(x):
    return x / (1.0 + jnp.exp(-x))


def _ada_kernel(cond_ref, w_ref, b_ref, o_ref):
    s = _silu(cond_ref[...]).astype(BF16)
    o_ref[...] = jnp.dot(s, w_ref[...].astype(BF16), preferred_element_type=F32) + b_ref[...]


def _ada_modulation(cond, w_ada, b_ada):
    depth, d, d3 = w_ada.shape
    r = cond.shape[0]
    tn = min(512, d3)
    return pl.pallas_call(
        _ada_kernel,
        grid=(depth, d3 // tn),
        in_specs=[
            pl.BlockSpec((r, d), lambda i, n: (0, 0)),
            pl.BlockSpec((None, d, tn), lambda i, n: (i, 0, n)),
            pl.BlockSpec((None, 1, tn), lambda i, n: (i, 0, n)),
        ],
        out_specs=pl.BlockSpec((None, r, tn), lambda i, n: (i, 0, n)),
        out_shape=jax.ShapeDtypeStruct((depth, r, d3), F32),
        compiler_params=_cparams("parallel", "parallel"),
        name="ada_modulation",
    )(cond, w_ada, b_ada.reshape(depth, 1, d3))


def _rms(x):
    return x * lax.rsqrt(jnp.mean(x * x, axis=-1, keepdims=True) + RMS_EPS)


def _modulate_kernel(x_ref, g_ref, sh_ref, sc_ref, o_ref, rinv_ref, gain_ref, shift_ref):
    x = x_ref[...]
    rinv = lax.rsqrt(jnp.mean(x * x, axis=-1, keepdims=True) + RMS_EPS)
    rinv_ref[...] = jnp.broadcast_to(rinv, rinv_ref.shape)
    gain_ref[...] = jnp.broadcast_to(g_ref[...] * (1.0 + sc_ref[...]), gain_ref.shape)
    shift_ref[...] = jnp.broadcast_to(sh_ref[...], shift_ref.shape)

    def chunk(i, carry):
        rows = pl.ds(pl.multiple_of(i * NORM_ROWS, NORM_ROWS), NORM_ROWS)
        r = rinv_ref[rows, :]
        for c in range(x_ref.shape[1] // LANES):
            cols = slice(c * LANES, (c + 1) * LANES)
            y = x_ref[rows, cols] * r * gain_ref[:, cols] + shift_ref[:, cols]
            o_ref[rows, cols] = y.astype(BF16)
        return carry

    lax.fori_loop(0, x_ref.shape[0] // NORM_ROWS, chunk, 0)


def _mod_row_map(layer, part, seq_len, rows_per_step, row0, per_batch, ncol_axis):
    def index(*ids):
        m = ids[0]
        row = row0 + (m * rows_per_step) // seq_len if per_batch else row0
        col = part if ncol_axis is None else part * ncol_axis[1] + ids[ncol_axis[0]]
        return (layer, row, 0, col)
    return index


def _modulate(x2d, g_row, mod4, layer, seq_len, row0, per_batch):
    m, d = x2d.shape
    tt = min(256, seq_len)
    return pl.pallas_call(
        _modulate_kernel,
        grid=(m // tt,),
        in_specs=[
            pl.BlockSpec((tt, d), lambda i: (i, 0)),
            pl.BlockSpec((None, 1, d), lambda i: (layer, 0, 0)),
            pl.BlockSpec((None, None, 1, d), _mod_row_map(layer, 0, seq_len, tt, row0, per_batch, None)),
            pl.BlockSpec((None, None, 1, d), _mod_row_map(layer, 1, seq_len, tt, row0, per_batch, None)),
        ],
        out_specs=pl.BlockSpec((tt, d), lambda i: (i, 0)),
        out_shape=jax.ShapeDtypeStruct((m, d), BF16),
        scratch_shapes=[pltpu.VMEM((tt, LANES), F32), pltpu.VMEM((NORM_ROWS, d), F32),
                        pltpu.VMEM((NORM_ROWS, d), F32)],
        compiler_params=_cparams("parallel"),
        name="modulate",
    )(x2d, g_row, mod4, mod4)


def _final_norm_kernel(x_ref, g_ref, o_ref):
    o_ref[...] = _rms(x_ref[...]) * g_ref[...]


def _final_norm(x2d, g):
    m, d = x2d.shape
    tt = min(256, m)
    return pl.pallas_call(
        _final_norm_kernel,
        grid=(m // tt,),
        in_specs=[pl.BlockSpec((tt, d), lambda i: (i, 0)), pl.BlockSpec((1, d), lambda i: (0, 0))],
        out_specs=pl.BlockSpec((tt, d), lambda i: (i, 0)),
        out_shape=jax.ShapeDtypeStruct((m, d), F32),
        compiler_params=_cparams("parallel"),
        name="final_norm",
    )(x2d, g.reshape(1, d))


def _dot(a_ref, w_ref):
    return jnp.dot(a_ref[...], w_ref[...], preferred_element_type=F32)


def _mm_cast_kernel(a_ref, w_ref, o_ref, *, scale):
    acc = _dot(a_ref, w_ref)
    if scale is not None:
        acc = acc * scale
    o_ref[...] = acc.astype(o_ref.dtype)


def _window_sum(u, window):
    t, n = u.shape
    zeros = jnp.zeros((POOL_PAD, n), F32)
    up = jnp.concatenate([zeros, u, zeros], axis=0)
    size = t + 2 * POOL_PAD

    def ahead(x, k):
        return pltpu.roll(x, (-k) % size, axis=0)

    fwd, span = up, 1
    while 2 * span < window:
        fwd = fwd + ahead(fwd, span)
        span *= 2
    s = fwd + ahead(fwd, -span)
    return s[POOL_PAD:POOL_PAD + t, :]


def _mm_pool_kernel(a_ref, w_ref, o_ref, *, seq_len, cols_per_group):
    tm, tn = o_ref.shape
    group = (pl.program_id(1) * tn) // cols_per_group
    t = lax.broadcasted_iota(jnp.int32, (seq_len, 1), 0)
    for gi, window in enumerate(POOL_WINDOWS):
        @pl.when(group == gi)
        def _(window=window):
            acc = _dot(a_ref, w_ref)
            half = window // 2
            count = jnp.minimum(t + half, seq_len) - jnp.maximum(t - half, 0)
            inv_count = 1.0 / count.astype(F32)
            for s in range(tm // seq_len):
                u = acc[s * seq_len:(s + 1) * seq_len, :]
                d = _window_sum(u, window) * inv_count - u
                o_ref[s * seq_len:(s + 1) * seq_len, :] = d.astype(o_ref.dtype)


def _mm_gate_group_kernel(h_ref, wz_ref, d_ref, wg_ref, ps_ref, o_ref):
    z = _dot(h_ref, wz_ref)
    y = _dot(d_ref, wg_ref)
    o_ref[...] = (y * ps_ref[...] * _silu(z)).astype(o_ref.dtype)


def _mm_resid_kernel(a_ref, w_ref, x_ref, gate_ref, o_ref):
    o_ref[...] = x_ref[...] + gate_ref[...] * _dot(a_ref, w_ref)


def _tiles(m, n):
    return min(MM_TILE, m), min(MM_TILE, n)


class _Weight(NamedTuple):
    stack: jax.Array
    layer: int


def _matmul(kern, a, w, w_col0, n_out, out_dtype, name, extra_in=(), extra_specs=(), tn_max=None):
    m, k = a.shape
    tm, tn = _tiles(m, n_out)
    if tn_max is not None:
        tn = min(tn, tn_max)
    nb0 = w_col0 // tn
    return pl.pallas_call(
        kern,
        grid=(m // tm, n_out // tn),
        in_specs=[
            pl.BlockSpec((tm, k), lambda i, j: (i, 0)),
            pl.BlockSpec((None, k, tn), lambda i, j: (w.layer, 0, nb0 + j)),
            *[spec(tm, tn) for spec in extra_specs],
        ],
        out_specs=pl.BlockSpec((tm, tn), lambda i, j: (i, j)),
        out_shape=jax.ShapeDtypeStruct((m, n_out), out_dtype),
        compiler_params=_cparams("parallel", "parallel"),
        name=name,
    )(a, w.stack, *extra_in)


def _cache_tile_copies(stage, zeros, sem, cache_hbm, slot, layer, fill_layers, batch0, head0):
    n_batch, _, tn = stage.shape[1:]
    copies = []
    for hh in range(tn // HEAD_DIM):
        dst = lambda lyr: cache_hbm.at[pl.ds(batch0, n_batch), lyr, :, head0 + hh, :]
        copies.append(pltpu.make_async_copy(stage.at[slot, :, :, pl.ds(hh * HEAD_DIM, HEAD_DIM)],
                                            dst(layer), sem.at[slot, 0, hh]))
        for f, lyr in enumerate(fill_layers):
            copies.append(pltpu.make_async_copy(zeros, dst(lyr), sem.at[slot, 1 + f, hh]))
    return copies


def _mm_cache_kernel(*refs, layer, fill_layers, n_steps):
    a_ref, w_ref = refs[:2]
    lo_ref, cache_hbm, stage, zeros, sem = refs[-5:]
    n_cols = pl.num_programs(1)
    step = pl.program_id(0) * n_cols + pl.program_id(1)
    slot = step % 2
    n_batch, _, tn = stage.shape[1:]

    def copies(s, which):
        return _cache_tile_copies(stage, zeros, sem, cache_hbm, which, layer, fill_layers,
                                  (s // n_cols) * n_batch, (s % n_cols) * (tn // HEAD_DIM))

    if fill_layers:
        @pl.when(step == 0)
        def _():
            zeros[...] = jnp.zeros(zeros.shape, zeros.dtype)

    @pl.when(step >= 2)
    def _():
        for cp in copies(step - 2, slot):
            cp.wait()

    acc = _dot(a_ref, w_ref)
    lo_ref[...] = acc.astype(lo_ref.dtype)
    stage[slot] = acc.reshape(stage.shape[1:])
    for cp in copies(step, slot):
        cp.start()

    @pl.when(step == n_steps - 1)
    def _():
        if n_steps > 1:
            for cp in copies(step - 1, 1 - slot):
                cp.wait()
        for cp in copies(step, slot):
            cp.wait()


def _matmul_to_cache(a, w, w_col0, cache, cache_shape, layer, name):
    m, k = a.shape
    _, n_layers, seq_len, n_heads, _ = cache_shape
    d = n_heads * HEAD_DIM
    tm, tn = _tiles(m, d)
    assert tm % seq_len == 0 and tn % HEAD_DIM == 0
    grid = (m // tm, d // tn)
    nb0 = w_col0 // tn
    any_spec = pl.BlockSpec(memory_space=pl.ANY)
    fresh = cache is None
    fill_layers = tuple(l for l in range(n_layers) if l != layer) if fresh else ()
    n_batch = tm // seq_len
    return pl.pallas_call(
        functools.partial(_mm_cache_kernel, layer=layer, fill_layers=fill_layers,
                          n_steps=grid[0] * grid[1]),
        grid=grid,
        in_specs=[pl.BlockSpec((tm, k), lambda i, j: (i, 0)),
                  pl.BlockSpec((None, k, tn), lambda i, j: (w.layer, 0, nb0 + j))]
                 + ([] if fresh else [any_spec]),
        out_specs=[pl.BlockSpec((tm, tn), lambda i, j: (i, j)), any_spec],
        out_shape=[jax.ShapeDtypeStruct((m, d), BF16), jax.ShapeDtypeStruct(cache_shape, F32)],
        scratch_shapes=[pltpu.VMEM((2, n_batch, seq_len, tn), F32),
                        pltpu.VMEM((n_batch, seq_len, HEAD_DIM), F32),
                        pltpu.SemaphoreType.DMA((2, 1 + len(fill_layers), tn // HEAD_DIM))],
        input_output_aliases={} if fresh else {2: 1},
        compiler_params=_cparams("arbitrary", "arbitrary"),
        name=name,
    )(a, w.stack, *([] if fresh else [cache]))


def _gate_group_matmul(h, w_in, z_col0, d, w_grp, pool_scale_row):
    m, dm = d.shape
    k = h.shape[1]
    _, g, dg, _ = w_grp.stack.shape
    tm = min(MM_TILE, m)
    zb0 = z_col0 // dg
    return pl.pallas_call(
        _mm_gate_group_kernel,
        grid=(m // tm, g),
        in_specs=[
            pl.BlockSpec((tm, k), lambda i, j: (i, 0)),
            pl.BlockSpec((None, k, dg), lambda i, j: (w_in.layer, 0, zb0 + j)),
            pl.BlockSpec((tm, dg), lambda i, j: (i, j)),
            pl.BlockSpec((None, None, dg, dg), lambda i, j: (w_grp.layer, j, 0, 0)),
            pl.BlockSpec((1, dg), lambda i, j: (0, j)),
        ],
        out_specs=pl.BlockSpec((tm, dg), lambda i, j: (i, j)),
        out_shape=jax.ShapeDtypeStruct((m, dm), BF16),
        compiler_params=_cparams("parallel", "parallel"),
        name="pool_gate_group",
    )(h, w_in.stack, d, w_grp.stack, pool_scale_row)


def _out_proj_residual(a, w_out, x2d, mod4, layer, seq_len, row0, per_batch):
    m, d = x2d.shape
    tm, tn = _tiles(m, d)
    x_spec = lambda tm, tn: pl.BlockSpec((tm, tn), lambda i, j: (i, j))
    gate_spec = lambda tm, tn: pl.BlockSpec(
        (None, None, 1, tn), _mod_row_map(layer, 2, seq_len, tm, row0, per_batch, (1, d // tn)))
    return _matmul(_mm_resid_kernel, a, w_out, 0, d, F32, "out_proj_residual",
                   extra_in=(x2d, mod4), extra_specs=(x_spec, gate_spec))


def _softmax_exps(scores):
    m = scores[0].max(axis=-1, keepdims=True)
    for s in scores[1:]:
        m = jnp.maximum(m, s.max(axis=-1, keepdims=True))
    return [jnp.exp(s - m).astype(BF16) for s in scores]


def _qk(q, k):
    return lax.dot_general(q, k, (((1,), (1,)), ((), ())), preferred_element_type=F32)


def _with_ones(v):
    return jnp.concatenate([v, jnp.ones(v.shape, v.dtype)], axis=1)


def _normalise_gate(o_ext, z):
    o = o_ext[:, :HEAD_DIM] / o_ext[:, HEAD_DIM:]
    return o * _silu(z.astype(F32))


def _attn_ctx_kernel(q_ref, k_ref, v_ref, z_ref, o_ref, *, heads):
    for hh in range(heads):
        sl = slice(hh * HEAD_DIM, (hh + 1) * HEAD_DIM)
        k = k_ref[:, sl].astype(BF16)
        v = _with_ones(v_ref[:, sl].astype(BF16))
        (e,) = _softmax_exps([_qk(q_ref[:, sl], k)])
        o_ext = jnp.dot(e, v, preferred_element_type=F32)
        o_ref[:, sl] = _normalise_gate(o_ext, z_ref[:, sl]).astype(o_ref.dtype)


def _heads_per_step(n_heads, most=8):
    return min(most, n_heads)


def _attn_context(q, k, v, z, seq_len):
    m, d = q.shape
    hg = _heads_per_step(d // HEAD_DIM, most=CTX_HEADS_PER_STEP)
    wblk = hg * HEAD_DIM
    spec = pl.BlockSpec((seq_len, wblk), lambda b, h: (b, h))
    return pl.pallas_call(
        functools.partial(_attn_ctx_kernel, heads=hg),
        grid=(m // seq_len, d // wblk),
        in_specs=[spec, spec, spec, spec],
        out_specs=spec,
        out_shape=jax.ShapeDtypeStruct((m, d), BF16),
        compiler_params=_cparams("parallel", "parallel"),
        name="attn_context",
    )(q, k, v, z)


def _rel_bias_kernel(rpb_ref, o_ref):
    rows, lanes = o_ref.shape
    lane = lax.broadcasted_iota(jnp.int32, (1, lanes), 1)
    qc = lane // (2 * GRID_W)
    kc = lane % GRID_W
    rel = jnp.clip(kc - qc, -(WIN_W - 1), WIN_W - 1) + (WIN_W - 1)
    start = jnp.clip(qc - WIN_W // 2, 0, GRID_W - WIN_W)
    in_window = (kc >= start) & (kc < start + WIN_W)
    table = rpb_ref[...]
    acc = jnp.zeros((rows, lanes), F32)
    for j in range(N_REL_COLS):
        acc = jnp.where(rel == j, table[:, j:j + 1], acc)
    row = lax.broadcasted_iota(jnp.int32, (rows, 1), 0)
    masked_row = (row % N_REL_ROWS) == (N_REL_ROWS - 1)
    o_ref[...] = jnp.where(in_window & jnp.logical_not(masked_row), acc, NEG_INF)


def _rel_bias_table(rpb_layer):
    h = rpb_layer.shape[0]
    padded = jnp.pad(rpb_layer, ((0, 0), (0, 1), (0, 0))).reshape(h * N_REL_ROWS, N_REL_COLS)
    rows = h * N_REL_ROWS
    tr = min(64, rows)
    lanes = GRID_W * 2 * GRID_W
    flat = pl.pallas_call(
        _rel_bias_kernel,
        grid=(rows // tr,),
        in_specs=[pl.BlockSpec((tr, N_REL_COLS), lambda i: (i, 0))],
        out_specs=pl.BlockSpec((tr, lanes), lambda i: (i, 0)),
        out_shape=jax.ShapeDtypeStruct((rows, lanes), F32),
        compiler_params=_cparams("parallel"),
        name="rel_bias_table",
    )(padded)
    return flat.reshape(h, N_REL_ROWS, GRID_W, 2 * GRID_W)


def _cache_copies(cache_hbm, buf, sem, layer, b, head0, slot, heads):
    return [pltpu.make_async_copy(cache_hbm.at[b, layer, :, head0 + hh, :], buf.at[slot, hh],
                                  sem.at[slot, hh]) for hh in range(heads)]


class _CastJob(NamedTuple):
    src: jax.Array
    row0: int
    rows: int


def _cast_jobs_step(jobs, srcs, dsts, bufs, sem, step, n_steps):
    slot = step % 2
    per_step = [job.rows // n_steps for job in jobs]

    def load(j, s, which):
        rows = pl.ds(jobs[j].row0 + s * per_step[j], per_step[j])
        return pltpu.make_async_copy(srcs[j].at[rows, :], bufs[j][0].at[which], sem.at[j, 0, which])

    def store(j, s, which):
        rows = pl.ds(s * per_step[j], per_step[j])
        return pltpu.make_async_copy(bufs[j][1].at[which], dsts[j].at[rows, :], sem.at[j, 1, which])

    n_jobs = range(len(jobs))

    @pl.when(step == 0)
    def _():
        for j in n_jobs:
            load(j, step, slot).start()

    @pl.when(step + 1 < n_steps)
    def _():
        for j in n_jobs:
            load(j, step + 1, 1 - slot).start()

    @pl.when(step >= 2)
    def _():
        for j in n_jobs:
            store(j, step - 2, slot).wait()

    for j in n_jobs:
        load(j, step, slot).wait()
        bufs[j][1][slot] = bufs[j][0][slot].astype(BF16)
        store(j, step, slot).start()

    @pl.when(step == n_steps - 1)
    def _():
        for j in n_jobs:
            if n_steps > 1:
                store(j, step - 1, 1 - slot).wait()
            store(j, step, slot).wait()


def _attn_lat_kernel(*refs, heads, grid_rows, layer, jobs, n_steps):
    n_jobs = len(jobs)
    (q_ref, k0_ref, k1_ref, k2_ref, v0_ref, v1_ref, v2_ref, kc_hbm, vc_hbm, z_ref, tab_ref) = refs[:11]
    job_srcs = refs[11:11 + n_jobs]
    o_ref = refs[11 + n_jobs]
    job_dsts = refs[12 + n_jobs:12 + 2 * n_jobs]
    bias_ref, kc_buf, vc_buf, kc_sem, vc_sem = refs[12 + 2 * n_jobs:17 + 2 * n_jobs]
    job_scratch = refs[17 + 2 * n_jobs:]
    hgi, rb, b = pl.program_id(0), pl.program_id(1), pl.program_id(2)
    n_rb, n_b = pl.num_programs(1), pl.num_programs(2)
    step = (hgi * n_rb + rb) * n_b + b
    slot = step % 2

    if n_jobs:
        bufs = [job_scratch[2 * j:2 * j + 2] for j in range(n_jobs)]
        _cast_jobs_step(jobs, job_srcs, job_dsts, bufs, job_scratch[-1], step, n_steps)

    def cache_copies(head_group, batch, which):
        return (_cache_copies(kc_hbm, kc_buf, kc_sem, layer, batch, head_group * heads, which, heads)
                + _cache_copies(vc_hbm, vc_buf, vc_sem, layer, batch, head_group * heads, which, heads))

    @pl.when(step == 0)
    def _():
        for cp in cache_copies(hgi, b, slot):
            cp.start()

    @pl.when(step + 1 < n_steps)
    def _():
        last_b = b == n_b - 1
        next_b = jnp.where(last_b, 0, b + 1)
        next_hgi = jnp.where(last_b & (rb == n_rb - 1), hgi + 1, hgi)
        for cp in cache_copies(next_hgi, next_b, 1 - slot):
            cp.start()

    for cp in cache_copies(hgi, b, slot):
        cp.wait()

    slab0 = Q_ROWS * jnp.clip(rb - 1, 0, grid_rows // Q_ROWS - SLAB_BLOCKS)

    @pl.when(pl.program_id(2) == 0)
    def _():
        left = lax.broadcasted_iota(jnp.int32, (GRID_W, 2 * GRID_W), 1) < GRID_W
        for rq in range(Q_ROWS):
            r = rb * Q_ROWS + rq
            win0 = jnp.clip(r - WIN_H // 2, 0, grid_rows - WIN_H)

            def entry(kk):
                kr = slab0 + kk
                ok = (kr >= win0) & (kr < win0 + WIN_H)
                return jnp.where(ok, kr - r + (WIN_H - 1), N_REL_ROWS - 1)

            for kk in range(0, SLAB_ROWS, 2):
                i_left, i_right = entry(kk), entry(kk + 1)
                for hh in range(heads):
                    tile = jnp.where(left, tab_ref[hh, i_left], tab_ref[hh, i_right])
                    bias_ref[hh, rq * GRID_W:(rq + 1) * GRID_W, kk * GRID_W:(kk + 2) * GRID_W] = tile

    k_refs = (k0_ref, k1_ref, k2_ref)
    v_refs = (v0_ref, v1_ref, v2_ref)

    def attend(pieces):
        for hh in range(heads):
            sl = slice(hh * HEAD_DIM, (hh + 1) * HEAD_DIM)
            q = q_ref[:, sl]
            scores = [_qk(q, k_refs[p][:, sl]) + bias_ref[hh, :, p * Q_BLOCK:(p + 1) * Q_BLOCK]
                      for p in pieces]
            scores.append(_qk(q, kc_buf[slot, hh].astype(BF16)))
            exps = _softmax_exps(scores)
            o_ext = jnp.dot(exps[-1], _with_ones(vc_buf[slot, hh].astype(BF16)),
                            preferred_element_type=F32)
            for e, p in zip(exps, pieces):
                o_ext = o_ext + jnp.dot(e, _with_ones(v_refs[p][:, sl]), preferred_element_type=F32)
            o_ref[:, sl] = _normalise_gate(o_ext, z_ref[:, sl]).astype(o_ref.dtype)

    all_pieces = tuple(range(SLAB_BLOCKS))
    pl.when(rb == 0)(lambda: attend(all_pieces[:-1]))
    pl.when(rb == n_rb - 1)(lambda: attend(all_pieces[1:]))
    pl.when((rb > 0) & (rb < n_rb - 1))(lambda: attend(all_pieces))


def _latent_steps(m, d, seq_len):
    return (d // (_heads_per_step(d // HEAD_DIM) * HEAD_DIM)) * (seq_len // Q_BLOCK) * (m // seq_len)


def _attn_latent(q, k, v, z, k_ctx, v_ctx, layer, bias_tab, seq_len, jobs=()):
    m, d = q.shape
    batch = m // seq_len
    grid_rows = seq_len // GRID_W
    n_rb = grid_rows // Q_ROWS
    assert grid_rows % Q_ROWS == 0 and n_rb >= SLAB_BLOCKS
    past = k_ctx.shape[2]
    hg = _heads_per_step(d // HEAD_DIM)
    wblk = hg * HEAD_DIM

    q_spec = pl.BlockSpec((Q_BLOCK, wblk), lambda h, rb, b: (b * n_rb + rb, h))

    def slab_spec(p):
        return pl.BlockSpec(
            (Q_BLOCK, wblk),
            lambda h, rb, b: (b * n_rb + jnp.clip(rb - 1, 0, n_rb - SLAB_BLOCKS) + p, h))

    any_spec = pl.BlockSpec(memory_space=pl.ANY)
    tab_spec = pl.BlockSpec((hg, N_REL_ROWS, GRID_W, 2 * GRID_W), lambda h, rb, b: (h, 0, 0, 0))
    n_steps = _latent_steps(m, d, seq_len)
    job_scratch = []
    for job in jobs:
        chunk = (2, job.rows // n_steps, job.src.shape[1])
        job_scratch += [pltpu.VMEM(chunk, F32), pltpu.VMEM(chunk, BF16)]
    if jobs:
        job_scratch.append(pltpu.SemaphoreType.DMA((len(jobs), 2, 2)))
    out, *converted = pl.pallas_call(
        functools.partial(_attn_lat_kernel, heads=hg, grid_rows=grid_rows, layer=layer,
                          jobs=tuple(job._replace(src=None) for job in jobs), n_steps=n_steps),
        grid=(d // wblk, n_rb, batch),
        in_specs=[q_spec, slab_spec(0), slab_spec(1), slab_spec(2),
                  slab_spec(0), slab_spec(1), slab_spec(2),
                  any_spec, any_spec, q_spec, tab_spec] + [any_spec] * len(jobs),
        out_specs=[q_spec] + [any_spec] * len(jobs),
        out_shape=[jax.ShapeDtypeStruct((m, d), BF16)]
                  + [jax.ShapeDtypeStruct((job.rows, job.src.shape[1]), BF16) for job in jobs],
        scratch_shapes=[pltpu.VMEM((hg, Q_BLOCK, SLAB_BLOCKS * Q_BLOCK), F32),
                        pltpu.VMEM((2, hg, past, HEAD_DIM), F32),
                        pltpu.VMEM((2, hg, past, HEAD_DIM), F32),
                        pltpu.SemaphoreType.DMA((2, hg)),
                        pltpu.SemaphoreType.DMA((2, hg))] + job_scratch,
        compiler_params=_cparams("arbitrary", "arbitrary", "arbitrary"),
        name="attn_latent",
    )(q, k, k, k, v, v, v, k_ctx, v_ctx, z, bias_tab, *[job.src for job in jobs])
    return out, converted


def _attn_projections(h, w_in, d, new_caches=None, cache_shape=None, layer=0):
    scale = HEAD_DIM ** -0.5
    q = _matmul(functools.partial(_mm_cast_kernel, scale=scale), h, w_in, 0, d, BF16, "attn_in_q")
    plain = functools.partial(_mm_cast_kernel, scale=None)
    if cache_shape is None:
        k = _matmul(plain, h, w_in, d, d, BF16, "attn_in_k")
        v = _matmul(plain, h, w_in, 2 * d, d, BF16, "attn_in_v")
    else:
        k, new_k = _matmul_to_cache(h, w_in, d, new_caches[0], cache_shape, layer, "attn_in_k_cache")
        v, new_v = _matmul_to_cache(h, w_in, 2 * d, new_caches[1], cache_shape, layer, "attn_in_v_cache")
        new_caches = (new_k, new_v)
    z = _matmul(plain, h, w_in, 3 * d, d, BF16, "attn_in_z")
    return q, k, v, z, new_caches


def _pool_branch(h, w_in, w_grp, pool_scale_row, d, seq_len):
    dg = d // len(POOL_WINDOWS)
    pooled = _matmul(functools.partial(_mm_pool_kernel, seq_len=seq_len, cols_per_group=dg),
                     h, w_in, 0, d, BF16, "pool_in_u", tn_max=dg)
    return _gate_group_matmul(h, w_in, d, pooled, w_grp, pool_scale_row)


def kernel(x_prompt, x_sample, c, cache_k, cache_v, c_ctx, norm_g, w_ada, b_ada, w_in_attn, rpb,
           w_out_attn, w_in_pool, w_grp_pool, pool_scale, w_out_pool, final_norm_g):
    bc, tc, d = x_prompt.shape
    bl, tl, _ = x_sample.shape
    depth = norm_g.shape[0]
    n_heads = cache_k.shape[3]
    past = cache_k.shape[2]
    n_attn = w_in_attn.shape[0]
    assert n_heads * HEAD_DIM == d and tl % (GRID_W * Q_ROWS) == 0

    n_cond = -(-(1 + bl) // 8) * 8
    cond = jnp.concatenate([c_ctx[None, :], c, jnp.zeros((n_cond - 1 - bl, d), F32)], axis=0)
    mod = _ada_modulation(cond, w_ada, b_ada)
    mod4 = mod.reshape(depth, n_cond, 1, 3 * d)

    xc = x_prompt.reshape(bc * tc, d)
    xl = x_sample.reshape(bl * tl, d)
    g_rows = norm_g.reshape(depth, 1, d)

    n_steps = _latent_steps(bl * tl, d, tl)
    stacks = {"in_attn": (w_in_attn, 1), "out_attn": (w_out_attn, 0), "in_pool": (w_in_pool, 0),
              "grp_pool": (w_grp_pool, 0), "out_pool": (w_out_pool, 0)}
    w_first = w_in_attn[:1].astype(BF16)
    wb, jobs = {}, {}
    for name, (stack, first_layer) in stacks.items():
        flat = stack.reshape(-1, stack.shape[-1])
        row0 = first_layer * (flat.shape[0] // stack.shape[0])
        rows = flat.shape[0] - row0
        if rows % (n_steps * PACKED_ROWS) == 0 and rows > 0:
            jobs[name] = _CastJob(flat, row0, rows)
        else:
            wb[name] = stack[first_layer:].astype(BF16)

    cache_shape = (bc, n_attn, tc, n_heads, HEAD_DIM)
    new_caches = (None, None)
    for i in range(depth):
        j = i // 2
        hc = _modulate(xc, g_rows, mod4, i, tc, 0, False)
        hl = _modulate(xl, g_rows, mod4, i, tl, 1, True)
        if i % 2 == 0:
            w_in = _Weight(w_first, 0) if j == 0 else _Weight(wb["in_attn"], j - 1)
            qc, kc, vc, zc, new_caches = _attn_projections(hc, w_in, d, new_caches, cache_shape, j)
            oc = _attn_context(qc, kc, vc, zc, tc)
            ql, kl, vl, zl, _ = _attn_projections(hl, w_in, d)
            bias_tab = _rel_bias_table(rpb[j])
            ol, converted = _attn_latent(ql, kl, vl, zl, cache_k, cache_v, j, bias_tab, tl,
                                         jobs=tuple(jobs.values()) if j == 0 else ())
            if j == 0:
                for name, flat_b in zip(jobs, converted):
                    stack, first_layer = stacks[name]
                    wb[name] = flat_b.reshape((stack.shape[0] - first_layer,) + stack.shape[1:])
            w_out = _Weight(wb["out_attn"], j)
        else:
            w_in = _Weight(wb["in_pool"], j)
            w_grp = _Weight(wb["grp_pool"], j)
            ps_row = pool_scale[j].reshape(1, d)
            oc = _pool_branch(hc, w_in, w_grp, ps_row, d, tc)
            ol = _pool_branch(hl, w_in, w_grp, ps_row, d, tl)
            w_out = _Weight(wb["out_pool"], j)
        xc = _out_proj_residual(oc, w_out, xc, mod4, i, tc, 0, False)
        xl = _out_proj_residual(ol, w_out, xl, mod4, i, tl, 1, True)

    y_prompt = _final_norm(xc, final_norm_g).reshape(bc, tc, d)
    y_sample = _final_norm(xl, final_norm_g).reshape(bl, tl, d)
    return (y_prompt, y_sample, new_caches[0], new_caches[1])
```

```python
import functools
from typing import NamedTuple

import jax
import jax.numpy as jnp
from jax import lax
from jax.experimental import pallas as pl
from jax.experimental.pallas import tpu as pltpu

F32 = jnp.float32
BF16 = jnp.bfloat16

LANES = 128
HEAD_DIM = 128
GRID_W = 64
WIN_H = 8
WIN_W = 16
POOL_WINDOWS = (2, 4, 8, 16)
RMS_EPS = 1e-6
NEG_INF = -1e30

Q_ROWS = 4
Q_BLOCK = Q_ROWS * GRID_W
SLAB_BLOCKS = 3
SLAB_ROWS = SLAB_BLOCKS * Q_ROWS
N_REL_ROWS = 2 * WIN_H
N_REL_COLS = 2 * WIN_W - 1
POOL_PAD = 8

VMEM_LIMIT = 58 * 1024 * 1024
MM_TILE = 1024
CTX_HEADS_PER_STEP = 16
PACKED_ROWS = 16
NORM_ROWS = PACKED_ROWS


def _cparams(*sem):
    return pltpu.CompilerParams(dimension_semantics=sem, vmem_limit_bytes=VMEM_LIMIT)


def _silu(x):
    return x / (1.0 + jnp.exp(-x))


def _ada_kernel(cond_ref, w_ref, b_ref, o_ref):
    s = _silu(cond_ref[...]).astype(BF16)
    o_ref[...] = jnp.dot(s, w_ref[...].astype(BF16), preferred_element_type=F32) + b_ref[...]


def _ada_modulation(cond, w_ada, b_ada):
    depth, d, d3 = w_ada.shape
    r = cond.shape[0]
    tn = min(512, d3)
    return pl.pallas_call(
        _ada_kernel,
        grid=(depth, d3 // tn),
        in_specs=[
            pl.BlockSpec((r, d), lambda i, n: (0, 0)),
            pl.BlockSpec((None, d, tn), lambda i, n: (i, 0, n)),
            pl.BlockSpec((None, 1, tn), lambda i, n: (i, 0, n)),
        ],
        out_specs=pl.BlockSpec((None, r, tn), lambda i, n: (i, 0, n)),
        out_shape=jax.ShapeDtypeStruct((depth, r, d3), F32),
        compiler_params=_cparams("parallel", "parallel"),
        name="ada_modulation",
    )(cond, w_ada, b_ada.reshape(depth, 1, d3))


def _rms(x):
    return x * lax.rsqrt(jnp.mean(x * x, axis=-1, keepdims=True) + RMS_EPS)


def _modulate_kernel(x_ref, g_ref, sh_ref, sc_ref, o_ref, rinv_ref, gain_ref, shift_ref):
    x = x_ref[...]
    rinv = lax.rsqrt(jnp.mean(x * x, axis=-1, keepdims=True) + RMS_EPS)
    rinv_ref[...] = jnp.broadcast_to(rinv, rinv_ref.shape)
    gain_ref[...] = jnp.broadcast_to(g_ref[...] * (1.0 + sc_ref[...]), gain_ref.shape)
    shift_ref[...] = jnp.broadcast_to(sh_ref[...], shift_ref.shape)

    def chunk(i, carry):
        rows = pl.ds(pl.multiple_of(i * NORM_ROWS, NORM_ROWS), NORM_ROWS)
        r = rinv_ref[rows, :]
        for c in range(x_ref.shape[1] // LANES):
            cols = slice(c * LANES, (c + 1) * LANES)
            y = x_ref[rows, cols] * r * gain_ref[:, cols] + shift_ref[:, cols]
            o_ref[rows, cols] = y.astype(BF16)
        return carry

    lax.fori_loop(0, x_ref.shape[0] // NORM_ROWS, chunk, 0)


def _mod_row_map(layer, part, seq_len, rows_per_step, row0, per_batch, ncol_axis):
    def index(*ids):
        m = ids[0]
        row = row0 + (m * rows_per_step) // seq_len if per_batch else row0
        col = part if ncol_axis is None else part * ncol_axis[1] + ids[ncol_axis[0]]
        return (layer, row, 0, col)
    return index


def _modulate(x2d, g_row, mod4, layer, seq_len, row0, per_batch):
    m, d = x2d.shape
    tt = min(256, seq_len)
    return pl.pallas_call(
        _modulate_kernel,
        grid=(m // tt,),
        in_specs=[
            pl.BlockSpec((tt, d), lambda i: (i, 0)),
            pl.BlockSpec((None, 1, d), lambda i: (layer, 0, 0)),
            pl.BlockSpec((None, None, 1, d), _mod_row_map(layer, 0, seq_len, tt, row0, per_batch, None)),
            pl.BlockSpec((None, None, 1, d), _mod_row_map(layer, 1, seq_len, tt, row0, per_batch, None)),
        ],
        out_specs=pl.BlockSpec((tt, d), lambda i: (i, 0)),
        out_shape=jax.ShapeDtypeStruct((m, d), BF16),
        scratch_shapes=[pltpu.VMEM((tt, LANES), F32), pltpu.VMEM((NORM_ROWS, d), F32),
                        pltpu.VMEM((NORM_ROWS, d), F32)],
        compiler_params=_cparams("parallel"),
        name="modulate",
    )(x2d, g_row, mod4, mod4)


def _final_norm_kernel(x_ref, g_ref, o_ref):
    o_ref[...] = _rms(x_ref[...]) * g_ref[...]


def _final_norm(x2d, g):
    m, d = x2d.shape
    tt = min(256, m)
    return pl.pallas_call(
        _final_norm_kernel,
        grid=(m // tt,),
        in_specs=[pl.BlockSpec((tt, d), lambda i: (i, 0)), pl.BlockSpec((1, d), lambda i: (0, 0))],
        out_specs=pl.BlockSpec((tt, d), lambda i: (i, 0)),
        out_shape=jax.ShapeDtypeStruct((m, d), F32),
        compiler_params=_cparams("parallel"),
        name="final_norm",
    )(x2d, g.reshape(1, d))


def _dot(a_ref, w_ref):
    return jnp.dot(a_ref[...], w_ref[...], preferred_element_type=F32)


def _mm_cast_kernel(a_ref, w_ref, o_ref, *, scale):
    acc = _dot(a_ref, w_ref)
    if scale is not None:
        acc = acc * scale
    o_ref[...] = acc.astype(o_ref.dtype)


def _window_sum(u, window):
    t, n = u.shape
    zeros = jnp.zeros((POOL_PAD, n), F32)
    up = jnp.concatenate([zeros, u, zeros], axis=0)
    size = t + 2 * POOL_PAD

    def ahead(x, k):
        return pltpu.roll(x, (-k) % size, axis=0)

    fwd, span = up, 1
    while 2 * span < window:
        fwd = fwd + ahead(fwd, span)
        span *= 2
    s = fwd + ahead(fwd, -span)
    return s[POOL_PAD:POOL_PAD + t, :]


def _mm_pool_kernel(a_ref, w_ref, o_ref, *, seq_len, cols_per_group):
    tm, tn = o_ref.shape
    group = (pl.program_id(1) * tn) // cols_per_group
    t = lax.broadcasted_iota(jnp.int32, (seq_len, 1), 0)
    for gi, window in enumerate(POOL_WINDOWS):
        @pl.when(group == gi)
        def _(window=window):
            acc = _dot(a_ref, w_ref)
            half = window // 2
            count = jnp.minimum(t + half, seq_len) - jnp.maximum(t - half, 0)
            inv_count = 1.0 / count.astype(F32)
            for s in range(tm // seq_len):
                u = acc[s * seq_len:(s + 1) * seq_len, :]
                d = _window_sum(u, window) * inv_count - u
                o_ref[s * seq_len:(s + 1) * seq_len, :] = d.astype(o_ref.dtype)


def _mm_gate_group_kernel(h_ref, wz_ref, d_ref, wg_ref, ps_ref, o_ref):
    z = _dot(h_ref, wz_ref)
    y = _dot(d_ref, wg_ref)
    o_ref[...] = (y * ps_ref[...] * _silu(z)).astype(o_ref.dtype)


def _mm_resid_kernel(a_ref, w_ref, x_ref, gate_ref, o_ref):
    o_ref[...] = x_ref[...] + gate_ref[...] * _dot(a_ref, w_ref)


def _tiles(m, n):
    return min(MM_TILE, m), min(MM_TILE, n)


class _Weight(NamedTuple):
    stack: jax.Array
    layer: int


def _mm_with_jobs_kernel(*refs, kern, n_in, jobs, n_steps):
    n_jobs = len(jobs)
    ins, srcs = refs[:n_in], refs[n_in:n_in + n_jobs]
    o_ref, dsts = refs[n_in + n_jobs], refs[n_in + n_jobs + 1:n_in + 2 * n_jobs + 1]
    scratch = refs[n_in + 2 * n_jobs + 1:]
    step = pl.program_id(0) * pl.num_programs(1) + pl.program_id(1)
    bufs = [scratch[2 * j:2 * j + 2] for j in range(n_jobs)]
    _cast_jobs_step(jobs, srcs, dsts, bufs, scratch[-1], step, n_steps)
    kern(*ins, o_ref)


def _matmul_steps(m, n_out):
    tm, tn = _tiles(m, n_out)
    return (m // tm) * (n_out // tn)


def _matmul(kern, a, w, w_col0, n_out, out_dtype, name, extra_in=(), extra_specs=(), tn_max=None,
            jobs=()):
    m, k = a.shape
    tm, tn = _tiles(m, n_out)
    if tn_max is not None:
        tn = min(tn, tn_max)
    nb0 = w_col0 // tn
    grid = (m // tm, n_out // tn)
    in_specs = [
        pl.BlockSpec((tm, k), lambda i, j: (i, 0)),
        pl.BlockSpec((None, k, tn), lambda i, j: (w.layer, 0, nb0 + j)),
        *[spec(tm, tn) for spec in extra_specs],
    ]
    out_spec = pl.BlockSpec((tm, tn), lambda i, j: (i, j))
    out_shape = jax.ShapeDtypeStruct((m, n_out), out_dtype)
    if not jobs:
        return pl.pallas_call(
            kern, grid=grid, in_specs=in_specs, out_specs=out_spec, out_shape=out_shape,
            compiler_params=_cparams("parallel", "parallel"), name=name,
        )(a, w.stack, *extra_in)
    n_steps = grid[0] * grid[1]
    any_spec = pl.BlockSpec(memory_space=pl.ANY)
    out, *converted = pl.pallas_call(
        functools.partial(_mm_with_jobs_kernel, kern=kern, n_in=len(in_specs),
                          jobs=tuple(job._replace(src=None) for job in jobs), n_steps=n_steps),
        grid=grid,
        in_specs=in_specs + [any_spec] * len(jobs),
        out_specs=[out_spec] + [any_spec] * len(jobs),
        out_shape=[out_shape] + _job_out_shapes(jobs),
        scratch_shapes=_job_scratch(jobs, n_steps),
        compiler_params=_cparams("arbitrary", "arbitrary"),
        name=name + "_cast",
    )(a, w.stack, *extra_in, *[job.src for job in jobs])
    return out, converted


def _cache_tile_copies(stage, zeros, sem, cache_hbm, slot, layer, fill_layers, batch0, head0):
    n_batch, _, tn = stage.shape[1:]
    copies = []
    for hh in range(tn // HEAD_DIM):
        dst = lambda lyr: cache_hbm.at[pl.ds(batch0, n_batch), lyr, :, head0 + hh, :]
        copies.append(pltpu.make_async_copy(stage.at[slot, :, :, pl.ds(hh * HEAD_DIM, HEAD_DIM)],
                                            dst(layer), sem.at[slot, 0, hh]))
        for f, lyr in enumerate(fill_layers):
            copies.append(pltpu.make_async_copy(zeros, dst(lyr), sem.at[slot, 1 + f, hh]))
    return copies


def _mm_cache_kernel(*refs, layer, fill_layers, n_steps):
    a_ref, w_ref = refs[:2]
    lo_ref, cache_hbm, stage, zeros, sem = refs[-5:]
    n_cols = pl.num_programs(1)
    step = pl.program_id(0) * n_cols + pl.program_id(1)
    slot = step % 2
    n_batch, _, tn = stage.shape[1:]

    def copies(s, which):
        return _cache_tile_copies(stage, zeros, sem, cache_hbm, which, layer, fill_layers,
                                  (s // n_cols) * n_batch, (s % n_cols) * (tn // HEAD_DIM))

    if fill_layers:
        @pl.when(step == 0)
        def _():
            zeros[...] = jnp.zeros(zeros.shape, zeros.dtype)

    @pl.when(step >= 2)
    def _():
        for cp in copies(step - 2, slot):
            cp.wait()

    acc = _dot(a_ref, w_ref)
    lo_ref[...] = acc.astype(lo_ref.dtype)
    stage[slot] = acc.reshape(stage.shape[1:])
    for cp in copies(step, slot):
        cp.start()

    @pl.when(step == n_steps - 1)
    def _():
        if n_steps > 1:
            for cp in copies(step - 1, 1 - slot):
                cp.wait()
        for cp in copies(step, slot):
            cp.wait()


def _matmul_to_cache(a, w, w_col0, cache, cache_shape, layer, name):
    m, k = a.shape
    _, n_layers, seq_len, n_heads, _ = cache_shape
    d = n_heads * HEAD_DIM
    tm, tn = _tiles(m, d)
    assert tm % seq_len == 0 and tn % HEAD_DIM == 0
    grid = (m // tm, d // tn)
    nb0 = w_col0 // tn
    any_spec = pl.BlockSpec(memory_space=pl.ANY)
    fresh = cache is None
    fill_layers = tuple(l for l in range(n_layers) if l != layer) if fresh else ()
    n_batch = tm // seq_len
    return pl.pallas_call(
        functools.partial(_mm_cache_kernel, layer=layer, fill_layers=fill_layers,
                          n_steps=grid[0] * grid[1]),
        grid=grid,
        in_specs=[pl.BlockSpec((tm, k), lambda i, j: (i, 0)),
                  pl.BlockSpec((None, k, tn), lambda i, j: (w.layer, 0, nb0 + j))]
                 + ([] if fresh else [any_spec]),
        out_specs=[pl.BlockSpec((tm, tn), lambda i, j: (i, j)), any_spec],
        out_shape=[jax.ShapeDtypeStruct((m, d), BF16), jax.ShapeDtypeStruct(cache_shape, F32)],
        scratch_shapes=[pltpu.VMEM((2, n_batch, seq_len, tn), F32),
                        pltpu.VMEM((n_batch, seq_len, HEAD_DIM), F32),
                        pltpu.SemaphoreType.DMA((2, 1 + len(fill_layers), tn // HEAD_DIM))],
        input_output_aliases={} if fresh else {2: 1},
        compiler_params=_cparams("arbitrary", "arbitrary"),
        name=name,
    )(a, w.stack, *([] if fresh else [cache]))


def _gate_group_matmul(h, w_in, z_col0, d, w_grp, pool_scale_row):
    m, dm = d.shape
    k = h.shape[1]
    _, g, dg, _ = w_grp.stack.shape
    tm = min(MM_TILE, m)
    zb0 = z_col0 // dg
    return pl.pallas_call(
        _mm_gate_group_kernel,
        grid=(m // tm, g),
        in_specs=[
            pl.BlockSpec((tm, k), lambda i, j: (i, 0)),
            pl.BlockSpec((None, k, dg), lambda i, j: (w_in.layer, 0, zb0 + j)),
            pl.BlockSpec((tm, dg), lambda i, j: (i, j)),
            pl.BlockSpec((None, None, dg, dg), lambda i, j: (w_grp.layer, j, 0, 0)),
            pl.BlockSpec((1, dg), lambda i, j: (0, j)),
        ],
        out_specs=pl.BlockSpec((tm, dg), lambda i, j: (i, j)),
        out_shape=jax.ShapeDtypeStruct((m, dm), BF16),
        compiler_params=_cparams("parallel", "parallel"),
        name="pool_gate_group",
    )(h, w_in.stack, d, w_grp.stack, pool_scale_row)


def _out_proj_residual(a, w_out, x2d, mod4, layer, seq_len, row0, per_batch):
    m, d = x2d.shape
    tm, tn = _tiles(m, d)
    x_spec = lambda tm, tn: pl.BlockSpec((tm, tn), lambda i, j: (i, j))
    gate_spec = lambda tm, tn: pl.BlockSpec(
        (None, None, 1, tn), _mod_row_map(layer, 2, seq_len, tm, row0, per_batch, (1, d // tn)))
    return _matmul(_mm_resid_kernel, a, w_out, 0, d, F32, "out_proj_residual",
                   extra_in=(x2d, mod4), extra_specs=(x_spec, gate_spec))


def _softmax_exps(scores):
    m = scores[0].max(axis=-1, keepdims=True)
    for s in scores[1:]:
        m = jnp.maximum(m, s.max(axis=-1, keepdims=True))
    return [jnp.exp(s - m).astype(BF16) for s in scores]


def _qk(q, k):
    return lax.dot_general(q, k, (((1,), (1,)), ((), ())), preferred_element_type=F32)


def _with_ones(v):
    return jnp.concatenate([v, jnp.ones(v.shape, v.dtype)], axis=1)


def _normalise_gate(o_ext, z):
    o = o_ext[:, :HEAD_DIM] / o_ext[:, HEAD_DIM:]
    return o * _silu(z.astype(F32))


def _attn_ctx_kernel(q_ref, k_ref, v_ref, z_ref, o_ref, *, heads):
    for hh in range(heads):
        sl = slice(hh * HEAD_DIM, (hh + 1) * HEAD_DIM)
        k = k_ref[:, sl].astype(BF16)
        v = _with_ones(v_ref[:, sl].astype(BF16))
        (e,) = _softmax_exps([_qk(q_ref[:, sl], k)])
        o_ext = jnp.dot(e, v, preferred_element_type=F32)
        o_ref[:, sl] = _normalise_gate(o_ext, z_ref[:, sl]).astype(o_ref.dtype)


def _heads_per_step(n_heads, most=8):
    return min(most, n_heads)


def _attn_context(q, k, v, z, seq_len):
    m, d = q.shape
    hg = _heads_per_step(d // HEAD_DIM, most=CTX_HEADS_PER_STEP)
    wblk = hg * HEAD_DIM
    spec = pl.BlockSpec((seq_len, wblk), lambda b, h: (b, h))
    return pl.pallas_call(
        functools.partial(_attn_ctx_kernel, heads=hg),
        grid=(m // seq_len, d // wblk),
        in_specs=[spec, spec, spec, spec],
        out_specs=spec,
        out_shape=jax.ShapeDtypeStruct((m, d), BF16),
        compiler_params=_cparams("parallel", "parallel"),
        name="attn_context",
    )(q, k, v, z)


def _rel_bias_kernel(rpb_ref, o_ref):
    rows, lanes = o_ref.shape
    lane = lax.broadcasted_iota(jnp.int32, (1, lanes), 1)
    qc = lane // (2 * GRID_W)
    kc = lane % GRID_W
    rel = jnp.clip(kc - qc, -(WIN_W - 1), WIN_W - 1) + (WIN_W - 1)
    start = jnp.clip(qc - WIN_W // 2, 0, GRID_W - WIN_W)
    in_window = (kc >= start) & (kc < start + WIN_W)
    table = rpb_ref[...]
    acc = jnp.zeros((rows, lanes), F32)
    for j in range(N_REL_COLS):
        acc = jnp.where(rel == j, table[:, j:j + 1], acc)
    row = lax.broadcasted_iota(jnp.int32, (rows, 1), 0)
    masked_row = (row % N_REL_ROWS) == (N_REL_ROWS - 1)
    o_ref[...] = jnp.where(in_window & jnp.logical_not(masked_row), acc, NEG_INF)


def _rel_bias_table(rpb_layer):
    h = rpb_layer.shape[0]
    padded = jnp.pad(rpb_layer, ((0, 0), (0, 1), (0, 0))).reshape(h * N_REL_ROWS, N_REL_COLS)
    rows = h * N_REL_ROWS
    tr = min(64, rows)
    lanes = GRID_W * 2 * GRID_W
    flat = pl.pallas_call(
        _rel_bias_kernel,
        grid=(rows // tr,),
        in_specs=[pl.BlockSpec((tr, N_REL_COLS), lambda i: (i, 0))],
        out_specs=pl.BlockSpec((tr, lanes), lambda i: (i, 0)),
        out_shape=jax.ShapeDtypeStruct((rows, lanes), F32),
        compiler_params=_cparams("parallel"),
        name="rel_bias_table",
    )(padded)
    return flat.reshape(h, N_REL_ROWS, GRID_W, 2 * GRID_W)


def _cache_copies(cache_hbm, buf, sem, layer, b, head0, slot, heads):
    return [pltpu.make_async_copy(cache_hbm.at[b, layer, :, head0 + hh, :], buf.at[slot, hh],
                                  sem.at[slot, hh]) for hh in range(heads)]


class _CastJob(NamedTuple):
    src: jax.Array
    row0: int
    rows: int
    col0: int
    cols: int


def _plan_cast(src, row0, rows, col0, cols, n_steps):
    even = rows > 0 and rows % (n_steps * PACKED_ROWS) == 0 and cols % LANES == 0 and col0 % LANES == 0
    return _CastJob(src, row0, rows, col0, cols) if even else None


def _cast_now(src, row0, rows, col0, cols):
    return src[row0:row0 + rows, col0:col0 + cols].astype(BF16)


def _job_scratch(jobs, n_steps):
    scratch = []
    for job in jobs:
        chunk = (2, job.rows // n_steps, job.cols)
        scratch += [pltpu.VMEM(chunk, F32), pltpu.VMEM(chunk, BF16)]
    if jobs:
        scratch.append(pltpu.SemaphoreType.DMA((len(jobs), 2, 2)))
    return scratch


def _job_out_shapes(jobs):
    return [jax.ShapeDtypeStruct((job.rows, job.cols), BF16) for job in jobs]


def _cast_jobs_step(jobs, srcs, dsts, bufs, sem, step, n_steps):
    slot = step % 2
    per_step = [job.rows // n_steps for job in jobs]

    def load(j, s, which):
        rows = pl.ds(jobs[j].row0 + s * per_step[j], per_step[j])
        cols = pl.ds(jobs[j].col0, jobs[j].cols)
        return pltpu.make_async_copy(srcs[j].at[rows, cols], bufs[j][0].at[which], sem.at[j, 0, which])

    def store(j, s, which):
        rows = pl.ds(s * per_step[j], per_step[j])
        return pltpu.make_async_copy(bufs[j][1].at[which], dsts[j].at[rows, :], sem.at[j, 1, which])

    n_jobs = range(len(jobs))

    @pl.when(step == 0)
    def _():
        for j in n_jobs:
            load(j, step, slot).start()

    @pl.when(step + 1 < n_steps)
    def _():
        for j in n_jobs:
            load(j, step + 1, 1 - slot).start()

    @pl.when(step >= 2)
    def _():
        for j in n_jobs:
            store(j, step - 2, slot).wait()

    for j in n_jobs:
        load(j, step, slot).wait()
        bufs[j][1][slot] = bufs[j][0][slot].astype(BF16)
        store(j, step, slot).start()

    @pl.when(step == n_steps - 1)
    def _():
        for j in n_jobs:
            if n_steps > 1:
                store(j, step - 1, 1 - slot).wait()
            store(j, step, slot).wait()


def _attn_lat_kernel(*refs, heads, grid_rows, layer, jobs, n_steps):
    n_jobs = len(jobs)
    (q_ref, k0_ref, k1_ref, k2_ref, v0_ref, v1_ref, v2_ref, kc_hbm, vc_hbm, z_ref, tab_ref) = refs[:11]
    job_srcs = refs[11:11 + n_jobs]
    o_ref = refs[11 + n_jobs]
    job_dsts = refs[12 + n_jobs:12 + 2 * n_jobs]
    bias_ref, kc_buf, vc_buf, kc_sem, vc_sem = refs[12 + 2 * n_jobs:17 + 2 * n_jobs]
    job_scratch = refs[17 + 2 * n_jobs:]
    hgi, rb, b = pl.program_id(0), pl.program_id(1), pl.program_id(2)
    n_rb, n_b = pl.num_programs(1), pl.num_programs(2)
    step = (hgi * n_rb + rb) * n_b + b
    slot = step % 2

    if n_jobs:
        bufs = [job_scratch[2 * j:2 * j + 2] for j in range(n_jobs)]
        _cast_jobs_step(jobs, job_srcs, job_dsts, bufs, job_scratch[-1], step, n_steps)

    def cache_copies(head_group, batch, which):
        return (_cache_copies(kc_hbm, kc_buf, kc_sem, layer, batch, head_group * heads, which, heads)
                + _cache_copies(vc_hbm, vc_buf, vc_sem, layer, batch, head_group * heads, which, heads))

    @pl.when(step == 0)
    def _():
        for cp in cache_copies(hgi, b, slot):
            cp.start()

    @pl.when(step + 1 < n_steps)
    def _():
        last_b = b == n_b - 1
        next_b = jnp.where(last_b, 0, b + 1)
        next_hgi = jnp.where(last_b & (rb == n_rb - 1), hgi + 1, hgi)
        for cp in cache_copies(next_hgi, next_b, 1 - slot):
            cp.start()

    for cp in cache_copies(hgi, b, slot):
        cp.wait()

    slab0 = Q_ROWS * jnp.clip(rb - 1, 0, grid_rows // Q_ROWS - SLAB_BLOCKS)

    @pl.when(pl.program_id(2) == 0)
    def _():
        left = lax.broadcasted_iota(jnp.int32, (GRID_W, 2 * GRID_W), 1) < GRID_W
        for rq in range(Q_ROWS):
            r = rb * Q_ROWS + rq
            win0 = jnp.clip(r - WIN_H // 2, 0, grid_rows - WIN_H)

            def entry(kk):
                kr = slab0 + kk
                ok = (kr >= win0) & (kr < win0 + WIN_H)
                return jnp.where(ok, kr - r + (WIN_H - 1), N_REL_ROWS - 1)

            for kk in range(0, SLAB_ROWS, 2):
                i_left, i_right = entry(kk), entry(kk + 1)
                for hh in range(heads):
                    tile = jnp.where(left, tab_ref[hh, i_left], tab_ref[hh, i_right])
                    bias_ref[hh, rq * GRID_W:(rq + 1) * GRID_W, kk * GRID_W:(kk + 2) * GRID_W] = tile

    k_refs = (k0_ref, k1_ref, k2_ref)
    v_refs = (v0_ref, v1_ref, v2_ref)

    def attend(pieces):
        for hh in range(heads):
            sl = slice(hh * HEAD_DIM, (hh + 1) * HEAD_DIM)
            q = q_ref[:, sl]
            scores = [_qk(q, k_refs[p][:, sl]) + bias_ref[hh, :, p * Q_BLOCK:(p + 1) * Q_BLOCK]
                      for p in pieces]
            scores.append(_qk(q, kc_buf[slot, hh].astype(BF16)))
            exps = _softmax_exps(scores)
            o_ext = jnp.dot(exps[-1], _with_ones(vc_buf[slot, hh].astype(BF16)),
                            preferred_element_type=F32)
            for e, p in zip(exps, pieces):
                o_ext = o_ext + jnp.dot(e, _with_ones(v_refs[p][:, sl]), preferred_element_type=F32)
            o_ref[:, sl] = _normalise_gate(o_ext, z_ref[:, sl]).astype(o_ref.dtype)

    all_pieces = tuple(range(SLAB_BLOCKS))
    pl.when(rb == 0)(lambda: attend(all_pieces[:-1]))
    pl.when(rb == n_rb - 1)(lambda: attend(all_pieces[1:]))
    pl.when((rb > 0) & (rb < n_rb - 1))(lambda: attend(all_pieces))


def _latent_steps(m, d, seq_len):
    return (d // (_heads_per_step(d // HEAD_DIM) * HEAD_DIM)) * (seq_len // Q_BLOCK) * (m // seq_len)


def _attn_latent(q, k, v, z, k_ctx, v_ctx, layer, bias_tab, seq_len, jobs=()):
    m, d = q.shape
    batch = m // seq_len
    grid_rows = seq_len // GRID_W
    n_rb = grid_rows // Q_ROWS
    assert grid_rows % Q_ROWS == 0 and n_rb >= SLAB_BLOCKS
    past = k_ctx.shape[2]
    hg = _heads_per_step(d // HEAD_DIM)
    wblk = hg * HEAD_DIM

    q_spec = pl.BlockSpec((Q_BLOCK, wblk), lambda h, rb, b: (b * n_rb + rb, h))

    def slab_spec(p):
        return pl.BlockSpec(
            (Q_BLOCK, wblk),
            lambda h, rb, b: (b * n_rb + jnp.clip(rb - 1, 0, n_rb - SLAB_BLOCKS) + p, h))

    any_spec = pl.BlockSpec(memory_space=pl.ANY)
    tab_spec = pl.BlockSpec((hg, N_REL_ROWS, GRID_W, 2 * GRID_W), lambda h, rb, b: (h, 0, 0, 0))
    n_steps = _latent_steps(m, d, seq_len)
    out, *converted = pl.pallas_call(
        functools.partial(_attn_lat_kernel, heads=hg, grid_rows=grid_rows, layer=layer,
                          jobs=tuple(job._replace(src=None) for job in jobs), n_steps=n_steps),
        grid=(d // wblk, n_rb, batch),
        in_specs=[q_spec, slab_spec(0), slab_spec(1), slab_spec(2),
                  slab_spec(0), slab_spec(1), slab_spec(2),
                  any_spec, any_spec, q_spec, tab_spec] + [any_spec] * len(jobs),
        out_specs=[q_spec] + [any_spec] * len(jobs),
        out_shape=[jax.ShapeDtypeStruct((m, d), BF16)]
                  + _job_out_shapes(jobs),
        scratch_shapes=[pltpu.VMEM((hg, Q_BLOCK, SLAB_BLOCKS * Q_BLOCK), F32),
                        pltpu.VMEM((2, hg, past, HEAD_DIM), F32),
                        pltpu.VMEM((2, hg, past, HEAD_DIM), F32),
                        pltpu.SemaphoreType.DMA((2, hg)),
                        pltpu.SemaphoreType.DMA((2, hg))] + _job_scratch(jobs, n_steps),
        compiler_params=_cparams("arbitrary", "arbitrary", "arbitrary"),
        name="attn_latent",
    )(q, k, k, k, v, v, v, k_ctx, v_ctx, z, bias_tab, *[job.src for job in jobs])
    return out, converted


class _CastLedger:
    def __init__(self):
        self.ready = {}
        self.pending = {}

    def want(self, name, src, row0, rows, col0, cols, host, host_steps):
        job = _plan_cast(src, row0, rows, col0, cols, host_steps)
        if job is None:
            self.ready[name] = _cast_now(src, row0, rows, col0, cols)
        else:
            self.pending.setdefault(host, []).append((name, job))

    def jobs(self, host):
        return tuple(job for _, job in self.pending.get(host, ()))

    def done(self, host, converted):
        for (name, _), array in zip(self.pending.pop(host), converted):
            self.ready[name] = array


def _attn_projections(h, weight_of, d, tag, ledger, new_caches=None, cache_shape=None, layer=0):
    def project(part, kern):
        w, col0 = weight_of(part)
        host = f"{part}_{tag}"
        jobs = ledger.jobs(host)
        out = _matmul(kern, h, w, col0, d, BF16, "attn_in_" + part, jobs=jobs)
        if jobs:
            out, converted = out
            ledger.done(host, converted)
        return out

    plain = functools.partial(_mm_cast_kernel, scale=None)
    q = project("q", functools.partial(_mm_cast_kernel, scale=HEAD_DIM ** -0.5))
    if cache_shape is None:
        k, v = project("k", plain), project("v", plain)
    else:
        (wk, ck), (wv, cv) = weight_of("k"), weight_of("v")
        k, new_k = _matmul_to_cache(h, wk, ck, new_caches[0], cache_shape, layer, "attn_in_k_cache")
        v, new_v = _matmul_to_cache(h, wv, cv, new_caches[1], cache_shape, layer, "attn_in_v_cache")
        new_caches = (new_k, new_v)
    z = project("z", plain)
    return q, k, v, z, new_caches


def _pool_branch(h, w_in, w_grp, pool_scale_row, d, seq_len):
    dg = d // len(POOL_WINDOWS)
    pooled = _matmul(functools.partial(_mm_pool_kernel, seq_len=seq_len, cols_per_group=dg),
                     h, w_in, 0, d, BF16, "pool_in_u", tn_max=dg)
    return _gate_group_matmul(h, w_in, d, pooled, w_grp, pool_scale_row)


def kernel(x_prompt, x_sample, c, cache_k, cache_v, c_ctx, norm_g, w_ada, b_ada, w_in_attn, rpb,
           w_out_attn, w_in_pool, w_grp_pool, pool_scale, w_out_pool, final_norm_g):
    bc, tc, d = x_prompt.shape
    bl, tl, _ = x_sample.shape
    depth = norm_g.shape[0]
    n_heads = cache_k.shape[3]
    past = cache_k.shape[2]
    n_attn = w_in_attn.shape[0]
    assert n_heads * HEAD_DIM == d and tl % (GRID_W * Q_ROWS) == 0

    n_cond = -(-(1 + bl) // 8) * 8
    cond = jnp.concatenate([c_ctx[None, :], c, jnp.zeros((n_cond - 1 - bl, d), F32)], axis=0)
    mod = _ada_modulation(cond, w_ada, b_ada)
    mod4 = mod.reshape(depth, n_cond, 1, 3 * d)

    xc = x_prompt.reshape(bc * tc, d)
    xl = x_sample.reshape(bl * tl, d)
    g_rows = norm_g.reshape(depth, 1, d)

    n_pool = w_in_pool.shape[0]
    flat = lambda w: w.reshape(-1, w.shape[-1])
    steps_c, steps_l = _matmul_steps(bc * tc, d), _matmul_steps(bl * tl, d)
    ledger = _CastLedger()
    ledger.ready["attn0_q"] = _cast_now(flat(w_in_attn), 0, d, 0, d)
    ledger.ready["attn0_z"] = _cast_now(flat(w_in_attn), 0, d, 3 * d, d)
    ledger.want("attn0_k", flat(w_in_attn), 0, d, d, d, "q_c", steps_c)
    ledger.want("attn0_v", flat(w_in_attn), 0, d, 2 * d, d, "q_c", steps_c)
    ledger.want("out_attn", flat(w_out_attn), 0, n_attn * d, 0, d, "z_c", steps_c)
    pool_a = (n_pool // 2) * d if n_pool > 1 else n_pool * d
    ledger.want("in_pool_a", flat(w_in_pool), 0, pool_a, 0, 2 * d, "q_l", steps_l)
    ledger.want("in_pool_b", flat(w_in_pool), pool_a, n_pool * d - pool_a, 0, 2 * d, "k_l", steps_l)
    ledger.want("attn_rest_a", flat(w_in_attn), d, (n_attn - 1) * d, 0, 2 * d, "v_l", steps_l)
    ledger.want("attn_rest_b", flat(w_in_attn), d, (n_attn - 1) * d, 2 * d, 2 * d, "z_l", steps_l)
    lat_steps = _latent_steps(bl * tl, d, tl)
    ledger.want("out_pool", flat(w_out_pool), 0, n_pool * d, 0, d, "lat", lat_steps)
    ledger.want("grp_pool", flat(w_grp_pool), 0, flat(w_grp_pool).shape[0], 0, w_grp_pool.shape[-1],
                "lat", lat_steps)

    def attn_weight_of(j):
        def weight_of(part):
            if j == 0:
                return _Weight(ledger.ready["attn0_" + part][None], 0), 0
            half = ledger.ready["attn_rest_a" if part in "qk" else "attn_rest_b"]
            return _Weight(half.reshape(n_attn - 1, d, 2 * d), j - 1), (0 if part in "qv" else d)
        return weight_of

    def pool_weight(j):
        if j * d < pool_a:
            return _Weight(ledger.ready["in_pool_a"].reshape(-1, d, 2 * d), j)
        return _Weight(ledger.ready["in_pool_b"].reshape(-1, d, 2 * d), j - pool_a // d)

    cache_shape = (bc, n_attn, tc, n_heads, HEAD_DIM)
    new_caches = (None, None)
    for i in range(depth):
        j = i // 2
        hc = _modulate(xc, g_rows, mod4, i, tc, 0, False)
        hl = _modulate(xl, g_rows, mod4, i, tl, 1, True)
        if i % 2 == 0:
            weight_of = attn_weight_of(j)
            qc, kc, vc, zc, new_caches = _attn_projections(hc, weight_of, d, "c", ledger, new_caches,
                                                           cache_shape, j)
            oc = _attn_context(qc, kc, vc, zc, tc)
            ql, kl, vl, zl, _ = _attn_projections(hl, weight_of, d, "l", ledger)
            bias_tab = _rel_bias_table(rpb[j])
            jobs = ledger.jobs("lat")
            ol, converted = _attn_latent(ql, kl, vl, zl, cache_k, cache_v, j, bias_tab, tl, jobs=jobs)
            if jobs:
                ledger.done("lat", converted)
            w_out = _Weight(ledger.ready["out_attn"].reshape(n_attn, d, d), j)
        else:
            w_in = pool_weight(j)
            w_grp = _Weight(ledger.ready["grp_pool"].reshape(w_grp_pool.shape), j)
            ps_row = pool_scale[j].reshape(1, d)
            oc = _pool_branch(hc, w_in, w_grp, ps_row, d, tc)
            ol = _pool_branch(hl, w_in, w_grp, ps_row, d, tl)
            w_out = _Weight(ledger.ready["out_pool"].reshape(n_pool, d, d), j)
        xc = _out_proj_residual(oc, w_out, xc, mod4, i, tc, 0, False)
        xl = _out_proj_residual(ol, w_out, xl, mod4, i, tl, 1, True)

    y_prompt = _final_norm(xc, final_norm_g).reshape(bc, tc, d)
    y_sample = _final_norm(xl, final_norm_g).reshape(bl, tl, d)
    return (y_prompt, y_sample, new_caches[0], new_caches[1])
```

```python
import functools
from typing import NamedTuple

import jax
import jax.numpy as jnp
from jax import lax
from jax.experimental import pallas as pl
from jax.experimental.pallas import tpu as pltpu

F32 = jnp.float32
BF16 = jnp.bfloat16

LANES = 128
HEAD_DIM = 128
GRID_W = 64
WIN_H = 8
WIN_W = 16
POOL_WINDOWS = (2, 4, 8, 16)
RMS_EPS = 1e-6
NEG_INF = -1e30

Q_ROWS = 4
Q_BLOCK = Q_ROWS * GRID_W
SLAB_BLOCKS = 3
SLAB_ROWS = SLAB_BLOCKS * Q_ROWS
N_REL_ROWS = 2 * WIN_H
N_REL_COLS = 2 * WIN_W - 1
POOL_PAD = 8

VMEM_LIMIT = 58 * 1024 * 1024
MM_TILE = 1024
CTX_HEADS_PER_STEP = 16
PACKED_ROWS = 16
NORM_ROWS = PACKED_ROWS
NORM_BLOCK_ROWS = 512


def _cparams(*sem):
    return pltpu.CompilerParams(dimension_semantics=sem, vmem_limit_bytes=VMEM_LIMIT)


def _silu(x):
    return x / (1.0 + jnp.exp(-x))


def _ada_kernel(cond_ref, w_ref, b_ref, o_ref):
    s = _silu(cond_ref[...]).astype(BF16)
    o_ref[...] = jnp.dot(s, w_ref[...].astype(BF16), preferred_element_type=F32) + b_ref[...]


def _ada_modulation(cond, w_ada, b_ada):
    depth, d, d3 = w_ada.shape
    r = cond.shape[0]
    tn = min(512, d3)
    return pl.pallas_call(
        _ada_kernel,
        grid=(depth, d3 // tn),
        in_specs=[
            pl.BlockSpec((r, d), lambda i, n: (0, 0)),
            pl.BlockSpec((None, d, tn), lambda i, n: (i, 0, n)),
            pl.BlockSpec((None, 1, tn), lambda i, n: (i, 0, n)),
        ],
        out_specs=pl.BlockSpec((None, r, tn), lambda i, n: (i, 0, n)),
        out_shape=jax.ShapeDtypeStruct((depth, r, d3), F32),
        compiler_params=_cparams("parallel", "parallel"),
        name="ada_modulation",
    )(cond, w_ada, b_ada.reshape(depth, 1, d3))


def _rms(x):
    return x * lax.rsqrt(jnp.mean(x * x, axis=-1, keepdims=True) + RMS_EPS)


def _modulate_kernel(x_ref, g_ref, sh_ref, sc_ref, o_ref, rinv_ref, gain_ref, shift_ref):
    x = x_ref[...]
    rinv = lax.rsqrt(jnp.mean(x * x, axis=-1, keepdims=True) + RMS_EPS)
    rinv_ref[...] = jnp.broadcast_to(rinv, rinv_ref.shape)
    gain_ref[...] = jnp.broadcast_to(g_ref[...] * (1.0 + sc_ref[...]), gain_ref.shape)
    shift_ref[...] = jnp.broadcast_to(sh_ref[...], shift_ref.shape)

    def chunk(i, carry):
        rows = pl.ds(pl.multiple_of(i * NORM_ROWS, NORM_ROWS), NORM_ROWS)
        r = rinv_ref[rows, :]
        for c in range(x_ref.shape[1] // LANES):
            cols = slice(c * LANES, (c + 1) * LANES)
            y = x_ref[rows, cols] * r * gain_ref[:, cols] + shift_ref[:, cols]
            o_ref[rows, cols] = y.astype(BF16)
        return carry

    lax.fori_loop(0, x_ref.shape[0] // NORM_ROWS, chunk, 0)


def _mod_row_map(layer, part, seq_len, rows_per_step, row0, per_batch, ncol_axis):
    def index(*ids):
        m = ids[0]
        row = row0 + (m * rows_per_step) // seq_len if per_batch else row0
        col = part if ncol_axis is None else part * ncol_axis[1] + ids[ncol_axis[0]]
        return (layer, row, 0, col)
    return index


def _modulate(x2d, g_row, mod4, layer, seq_len, row0, per_batch):
    m, d = x2d.shape
    tt = min(NORM_BLOCK_ROWS, seq_len if per_batch else m)
    return pl.pallas_call(
        _modulate_kernel,
        grid=(m // tt,),
        in_specs=[
            pl.BlockSpec((tt, d), lambda i: (i, 0)),
            pl.BlockSpec((None, 1, d), lambda i: (layer, 0, 0)),
            pl.BlockSpec((None, None, 1, d), _mod_row_map(layer, 0, seq_len, tt, row0, per_batch, None)),
            pl.BlockSpec((None, None, 1, d), _mod_row_map(layer, 1, seq_len, tt, row0, per_batch, None)),
        ],
        out_specs=pl.BlockSpec((tt, d), lambda i: (i, 0)),
        out_shape=jax.ShapeDtypeStruct((m, d), BF16),
        scratch_shapes=[pltpu.VMEM((tt, LANES), F32), pltpu.VMEM((NORM_ROWS, d), F32),
                        pltpu.VMEM((NORM_ROWS, d), F32)],
        compiler_params=_cparams("parallel"),
        name="modulate",
    )(x2d, g_row, mod4, mod4)


def _final_norm_kernel(x_ref, g_ref, o_ref):
    o_ref[...] = _rms(x_ref[...]) * g_ref[...]


def _final_norm(x2d, g):
    m, d = x2d.shape
    tt = min(256, m)
    return pl.pallas_call(
        _final_norm_kernel,
        grid=(m // tt,),
        in_specs=[pl.BlockSpec((tt, d), lambda i: (i, 0)), pl.BlockSpec((1, d), lambda i: (0, 0))],
        out_specs=pl.BlockSpec((tt, d), lambda i: (i, 0)),
        out_shape=jax.ShapeDtypeStruct((m, d), F32),
        compiler_params=_cparams("parallel"),
        name="final_norm",
    )(x2d, g.reshape(1, d))


def _dot(a_ref, w_ref):
    return jnp.dot(a_ref[...], w_ref[...], preferred_element_type=F32)


def _mm_cast_kernel(a_ref, w_ref, o_ref, *, scale):
    acc = _dot(a_ref, w_ref)
    if scale is not None:
        acc = acc * scale
    o_ref[...] = acc.astype(o_ref.dtype)


def _window_sum(u, window):
    t, n = u.shape
    zeros = jnp.zeros((POOL_PAD, n), F32)
    up = jnp.concatenate([zeros, u, zeros], axis=0)
    size = t + 2 * POOL_PAD

    def ahead(x, k):
        return pltpu.roll(x, (-k) % size, axis=0)

    fwd, span = up, 1
    while 2 * span < window:
        fwd = fwd + ahead(fwd, span)
        span *= 2
    s = fwd + ahead(fwd, -span)
    return s[POOL_PAD:POOL_PAD + t, :]


def _mm_pool_kernel(a_ref, w_ref, o_ref, *, seq_len, cols_per_group):
    tm, tn = o_ref.shape
    group = (pl.program_id(1) * tn) // cols_per_group
    t = lax.broadcasted_iota(jnp.int32, (seq_len, 1), 0)
    for gi, window in enumerate(POOL_WINDOWS):
        @pl.when(group == gi)
        def _(window=window):
            acc = _dot(a_ref, w_ref)
            half = window // 2
            count = jnp.minimum(t + half, seq_len) - jnp.maximum(t - half, 0)
            inv_count = 1.0 / count.astype(F32)
            for s in range(tm // seq_len):
                u = acc[s * seq_len:(s + 1) * seq_len, :]
                d = _window_sum(u, window) * inv_count - u
                o_ref[s * seq_len:(s + 1) * seq_len, :] = d.astype(o_ref.dtype)


def _mm_gate_group_kernel(h_ref, wz_ref, d_ref, wg_ref, ps_ref, o_ref):
    z = _dot(h_ref, wz_ref)
    y = _dot(d_ref, wg_ref)
    o_ref[...] = (y * ps_ref[...] * _silu(z)).astype(o_ref.dtype)


def _mm_resid_kernel(a_ref, w_ref, x_ref, gate_ref, o_ref):
    o_ref[...] = x_ref[...] + gate_ref[...] * _dot(a_ref, w_ref)


def _tiles(m, n):
    return min(MM_TILE, m), min(MM_TILE, n)


class _Weight(NamedTuple):
    stack: jax.Array
    layer: int


def _mm_with_jobs_kernel(*refs, kern, n_in, jobs, n_steps):
    n_jobs = len(jobs)
    ins, srcs = refs[:n_in], refs[n_in:n_in + n_jobs]
    o_ref, dsts = refs[n_in + n_jobs], refs[n_in + n_jobs + 1:n_in + 2 * n_jobs + 1]
    scratch = refs[n_in + 2 * n_jobs + 1:]
    step = pl.program_id(0) * pl.num_programs(1) + pl.program_id(1)
    bufs = [scratch[2 * j:2 * j + 2] for j in range(n_jobs)]
    _cast_jobs_step(jobs, srcs, dsts, bufs, scratch[-1], step, n_steps)
    kern(*ins, o_ref)


def _matmul_steps(m, n_out):
    tm, tn = _tiles(m, n_out)
    return (m // tm) * (n_out // tn)


def _matmul(kern, a, w, w_col0, n_out, out_dtype, name, extra_in=(), extra_specs=(), tn_max=None,
            jobs=()):
    m, k = a.shape
    tm, tn = _tiles(m, n_out)
    if tn_max is not None:
        tn = min(tn, tn_max)
    nb0 = w_col0 // tn
    grid = (m // tm, n_out // tn)
    in_specs = [
        pl.BlockSpec((tm, k), lambda i, j: (i, 0)),
        pl.BlockSpec((None, k, tn), lambda i, j: (w.layer, 0, nb0 + j)),
        *[spec(tm, tn) for spec in extra_specs],
    ]
    out_spec = pl.BlockSpec((tm, tn), lambda i, j: (i, j))
    out_shape = jax.ShapeDtypeStruct((m, n_out), out_dtype)
    if not jobs:
        return pl.pallas_call(
            kern, grid=grid, in_specs=in_specs, out_specs=out_spec, out_shape=out_shape,
            compiler_params=_cparams("parallel", "parallel"), name=name,
        )(a, w.stack, *extra_in)
    n_steps = grid[0] * grid[1]
    any_spec = pl.BlockSpec(memory_space=pl.ANY)
    out, *converted = pl.pallas_call(
        functools.partial(_mm_with_jobs_kernel, kern=kern, n_in=len(in_specs),
                          jobs=tuple(job._replace(src=None) for job in jobs), n_steps=n_steps),
        grid=grid,
        in_specs=in_specs + [any_spec] * len(jobs),
        out_specs=[out_spec] + [any_spec] * len(jobs),
        out_shape=[out_shape] + _job_out_shapes(jobs),
        scratch_shapes=_job_scratch(jobs, n_steps),
        compiler_params=_cparams("arbitrary", "arbitrary"),
        name=name + "_cast",
    )(a, w.stack, *extra_in, *[job.src for job in jobs])
    return out, converted


def _cache_tile_copies(stage, zeros, sem, cache_hbm, slot, layer, fill_layers, batch0, head0):
    n_batch, _, tn = stage.shape[1:]
    copies = []
    for hh in range(tn // HEAD_DIM):
        dst = lambda lyr: cache_hbm.at[pl.ds(batch0, n_batch), lyr, :, head0 + hh, :]
        copies.append(pltpu.make_async_copy(stage.at[slot, :, :, pl.ds(hh * HEAD_DIM, HEAD_DIM)],
                                            dst(layer), sem.at[slot, 0, hh]))
        for f, lyr in enumerate(fill_layers):
            copies.append(pltpu.make_async_copy(zeros, dst(lyr), sem.at[slot, 1 + f, hh]))
    return copies


def _mm_cache_kernel(*refs, layer, fill_layers, n_steps):
    a_ref, w_ref = refs[:2]
    lo_ref, cache_hbm, stage, zeros, sem = refs[-5:]
    n_cols = pl.num_programs(1)
    step = pl.program_id(0) * n_cols + pl.program_id(1)
    slot = step % 2
    n_batch, _, tn = stage.shape[1:]

    def copies(s, which):
        return _cache_tile_copies(stage, zeros, sem, cache_hbm, which, layer, fill_layers,
                                  (s // n_cols) * n_batch, (s % n_cols) * (tn // HEAD_DIM))

    if fill_layers:
        @pl.when(step == 0)
        def _():
            zeros[...] = jnp.zeros(zeros.shape, zeros.dtype)

    @pl.when(step >= 2)
    def _():
        for cp in copies(step - 2, slot):
            cp.wait()

    acc = _dot(a_ref, w_ref)
    lo_ref[...] = acc.astype(lo_ref.dtype)
    stage[slot] = acc.reshape(stage.shape[1:])
    for cp in copies(step, slot):
        cp.start()

    @pl.when(step == n_steps - 1)
    def _():
        if n_steps > 1:
            for cp in copies(step - 1, 1 - slot):
                cp.wait()
        for cp in copies(step, slot):
            cp.wait()


def _matmul_to_cache(a, w, w_col0, cache, cache_shape, layer, name):
    m, k = a.shape
    _, n_layers, seq_len, n_heads, _ = cache_shape
    d = n_heads * HEAD_DIM
    tm, tn = _tiles(m, d)
    assert tm % seq_len == 0 and tn % HEAD_DIM == 0
    grid = (m // tm, d // tn)
    nb0 = w_col0 // tn
    any_spec = pl.BlockSpec(memory_space=pl.ANY)
    fresh = cache is None
    fill_layers = tuple(l for l in range(n_layers) if l != layer) if fresh else ()
    n_batch = tm // seq_len
    return pl.pallas_call(
        functools.partial(_mm_cache_kernel, layer=layer, fill_layers=fill_layers,
                          n_steps=grid[0] * grid[1]),
        grid=grid,
        in_specs=[pl.BlockSpec((tm, k), lambda i, j: (i, 0)),
                  pl.BlockSpec((None, k, tn), lambda i, j: (w.layer, 0, nb0 + j))]
                 + ([] if fresh else [any_spec]),
        out_specs=[pl.BlockSpec((tm, tn), lambda i, j: (i, j)), any_spec],
        out_shape=[jax.ShapeDtypeStruct((m, d), BF16), jax.ShapeDtypeStruct(cache_shape, F32)],
        scratch_shapes=[pltpu.VMEM((2, n_batch, seq_len, tn), F32),
                        pltpu.VMEM((n_batch, seq_len, HEAD_DIM), F32),
                        pltpu.SemaphoreType.DMA((2, 1 + len(fill_layers), tn // HEAD_DIM))],
        input_output_aliases={} if fresh else {2: 1},
        compiler_params=_cparams("arbitrary", "arbitrary"),
        name=name,
    )(a, w.stack, *([] if fresh else [cache]))


def _gate_group_matmul(h, w_in, z_col0, d, w_grp, pool_scale_row):
    m, dm = d.shape
    k = h.shape[1]
    _, g, dg, _ = w_grp.stack.shape
    tm = min(MM_TILE, m)
    zb0 = z_col0 // dg
    return pl.pallas_call(
        _mm_gate_group_kernel,
        grid=(m // tm, g),
        in_specs=[
            pl.BlockSpec((tm, k), lambda i, j: (i, 0)),
            pl.BlockSpec((None, k, dg), lambda i, j: (w_in.layer, 0, zb0 + j)),
            pl.BlockSpec((tm, dg), lambda i, j: (i, j)),
            pl.BlockSpec((None, None, dg, dg), lambda i, j: (w_grp.layer, j, 0, 0)),
            pl.BlockSpec((1, dg), lambda i, j: (0, j)),
        ],
        out_specs=pl.BlockSpec((tm, dg), lambda i, j: (i, j)),
        out_shape=jax.ShapeDtypeStruct((m, dm), BF16),
        compiler_params=_cparams("parallel", "parallel"),
        name="pool_gate_group",
    )(h, w_in.stack, d, w_grp.stack, pool_scale_row)


def _out_proj_residual(a, w_out, x2d, mod4, layer, seq_len, row0, per_batch):
    m, d = x2d.shape
    tm, tn = _tiles(m, d)
    x_spec = lambda tm, tn: pl.BlockSpec((tm, tn), lambda i, j: (i, j))
    gate_spec = lambda tm, tn: pl.BlockSpec(
        (None, None, 1, tn), _mod_row_map(layer, 2, seq_len, tm, row0, per_batch, (1, d // tn)))
    return _matmul(_mm_resid_kernel, a, w_out, 0, d, F32, "out_proj_residual",
                   extra_in=(x2d, mod4), extra_specs=(x_spec, gate_spec))


def _softmax_exps(scores):
    m = scores[0].max(axis=-1, keepdims=True)
    for s in scores[1:]:
        m = jnp.maximum(m, s.max(axis=-1, keepdims=True))
    return [jnp.exp(s - m).astype(BF16) for s in scores]


def _qk(q, k):
    return lax.dot_general(q, k, (((1,), (1,)), ((), ())), preferred_element_type=F32)


def _with_ones(v):
    return jnp.concatenate([v, jnp.ones(v.shape, v.dtype)], axis=1)


def _normalise_gate(o_ext, z):
    o = o_ext[:, :HEAD_DIM] / o_ext[:, HEAD_DIM:]
    return o * _silu(z.astype(F32))


def _attn_ctx_kernel(q_ref, k_ref, v_ref, z_ref, o_ref, *, heads):
    for hh in range(heads):
        sl = slice(hh * HEAD_DIM, (hh + 1) * HEAD_DIM)
        k = k_ref[:, sl].astype(BF16)
        v = _with_ones(v_ref[:, sl].astype(BF16))
        (e,) = _softmax_exps([_qk(q_ref[:, sl], k)])
        o_ext = jnp.dot(e, v, preferred_element_type=F32)
        o_ref[:, sl] = _normalise_gate(o_ext, z_ref[:, sl]).astype(o_ref.dtype)


def _heads_per_step(n_heads, most=8):
    return min(most, n_heads)


def _attn_context(q, k, v, z, seq_len):
    m, d = q.shape
    hg = _heads_per_step(d // HEAD_DIM, most=CTX_HEADS_PER_STEP)
    wblk = hg * HEAD_DIM
    spec = pl.BlockSpec((seq_len, wblk), lambda b, h: (b, h))
    return pl.pallas_call(
        functools.partial(_attn_ctx_kernel, heads=hg),
        grid=(m // seq_len, d // wblk),
        in_specs=[spec, spec, spec, spec],
        out_specs=spec,
        out_shape=jax.ShapeDtypeStruct((m, d), BF16),
        compiler_params=_cparams("parallel", "parallel"),
        name="attn_context",
    )(q, k, v, z)


def _rel_bias_kernel(rpb_ref, o_ref):
    rows, lanes = o_ref.shape
    lane = lax.broadcasted_iota(jnp.int32, (1, lanes), 1)
    qc = lane // (2 * GRID_W)
    kc = lane % GRID_W
    rel = jnp.clip(kc - qc, -(WIN_W - 1), WIN_W - 1) + (WIN_W - 1)
    start = jnp.clip(qc - WIN_W // 2, 0, GRID_W - WIN_W)
    in_window = (kc >= start) & (kc < start + WIN_W)
    table = rpb_ref[...]
    acc = jnp.zeros((rows, lanes), F32)
    for j in range(N_REL_COLS):
        acc = jnp.where(rel == j, table[:, j:j + 1], acc)
    row = lax.broadcasted_iota(jnp.int32, (rows, 1), 0)
    masked_row = (row % N_REL_ROWS) == (N_REL_ROWS - 1)
    o_ref[...] = jnp.where(in_window & jnp.logical_not(masked_row), acc, NEG_INF)


def _rel_bias_table(rpb_layer):
    h = rpb_layer.shape[0]
    padded = jnp.pad(rpb_layer, ((0, 0), (0, 1), (0, 0))).reshape(h * N_REL_ROWS, N_REL_COLS)
    rows = h * N_REL_ROWS
    tr = min(64, rows)
    lanes = GRID_W * 2 * GRID_W
    flat = pl.pallas_call(
        _rel_bias_kernel,
        grid=(rows // tr,),
        in_specs=[pl.BlockSpec((tr, N_REL_COLS), lambda i: (i, 0))],
        out_specs=pl.BlockSpec((tr, lanes), lambda i: (i, 0)),
        out_shape=jax.ShapeDtypeStruct((rows, lanes), F32),
        compiler_params=_cparams("parallel"),
        name="rel_bias_table",
    )(padded)
    return flat.reshape(h, N_REL_ROWS, GRID_W, 2 * GRID_W)


def _cache_copies(cache_hbm, buf, sem, layer, b, head0, slot, heads):
    return [pltpu.make_async_copy(cache_hbm.at[b, layer, :, head0 + hh, :], buf.at[slot, hh],
                                  sem.at[slot, hh]) for hh in range(heads)]


class _CastJob(NamedTuple):
    src: jax.Array
    row0: int
    rows: int
    col0: int
    cols: int


def _plan_cast(src, row0, rows, col0, cols, n_steps):
    even = rows > 0 and rows % (n_steps * PACKED_ROWS) == 0 and cols % LANES == 0 and col0 % LANES == 0
    return _CastJob(src, row0, rows, col0, cols) if even else None


def _cast_now(src, row0, rows, col0, cols):
    return src[row0:row0 + rows, col0:col0 + cols].astype(BF16)


def _cast_block_kernel(x_ref, o_ref):
    o_ref[...] = x_ref[...].astype(o_ref.dtype)


def _cast_columns(stack, layer, col0, cols):
    _, k, _ = stack.shape
    tr = min(512, k)
    assert col0 % cols == 0
    return pl.pallas_call(
        _cast_block_kernel,
        grid=(k // tr,),
        in_specs=[pl.BlockSpec((None, tr, cols), lambda i: (layer, i, col0 // cols))],
        out_specs=pl.BlockSpec((tr, cols), lambda i: (i, 0)),
        out_shape=jax.ShapeDtypeStruct((k, cols), BF16),
        compiler_params=_cparams("parallel"),
        name="cast_columns",
    )(stack)


def _job_scratch(jobs, n_steps):
    scratch = []
    for job in jobs:
        chunk = (2, job.rows // n_steps, job.cols)
        scratch += [pltpu.VMEM(chunk, F32), pltpu.VMEM(chunk, BF16)]
    if jobs:
        scratch.append(pltpu.SemaphoreType.DMA((len(jobs), 2, 2)))
    return scratch


def _job_out_shapes(jobs):
    return [jax.ShapeDtypeStruct((job.rows, job.cols), BF16) for job in jobs]


def _cast_jobs_step(jobs, srcs, dsts, bufs, sem, step, n_steps):
    slot = step % 2
    per_step = [job.rows // n_steps for job in jobs]

    def load(j, s, which):
        rows = pl.ds(jobs[j].row0 + s * per_step[j], per_step[j])
        cols = pl.ds(jobs[j].col0, jobs[j].cols)
        return pltpu.make_async_copy(srcs[j].at[rows, cols], bufs[j][0].at[which], sem.at[j, 0, which])

    def store(j, s, which):
        rows = pl.ds(s * per_step[j], per_step[j])
        return pltpu.make_async_copy(bufs[j][1].at[which], dsts[j].at[rows, :], sem.at[j, 1, which])

    n_jobs = range(len(jobs))

    @pl.when(step == 0)
    def _():
        for j in n_jobs:
            load(j, step, slot).start()

    @pl.when(step + 1 < n_steps)
    def _():
        for j in n_jobs:
            load(j, step + 1, 1 - slot).start()

    @pl.when(step >= 2)
    def _():
        for j in n_jobs:
            store(j, step - 2, slot).wait()

    for j in n_jobs:
        load(j, step, slot).wait()
        bufs[j][1][slot] = bufs[j][0][slot].astype(BF16)
        store(j, step, slot).start()

    @pl.when(step == n_steps - 1)
    def _():
        for j in n_jobs:
            if n_steps > 1:
                store(j, step - 1, 1 - slot).wait()
            store(j, step, slot).wait()


def _attn_lat_kernel(*refs, heads, grid_rows, layer, jobs, n_steps):
    n_jobs = len(jobs)
    (q_ref, k0_ref, k1_ref, k2_ref, v0_ref, v1_ref, v2_ref, kc_hbm, vc_hbm, z_ref, tab_ref) = refs[:11]
    job_srcs = refs[11:11 + n_jobs]
    o_ref = refs[11 + n_jobs]
    job_dsts = refs[12 + n_jobs:12 + 2 * n_jobs]
    bias_ref, kc_buf, vc_buf, kc_sem, vc_sem = refs[12 + 2 * n_jobs:17 + 2 * n_jobs]
    job_scratch = refs[17 + 2 * n_jobs:]
    hgi, rb, b = pl.program_id(0), pl.program_id(1), pl.program_id(2)
    n_rb, n_b = pl.num_programs(1), pl.num_programs(2)
    step = (hgi * n_rb + rb) * n_b + b
    slot = step % 2

    if n_jobs:
        bufs = [job_scratch[2 * j:2 * j + 2] for j in range(n_jobs)]
        _cast_jobs_step(jobs, job_srcs, job_dsts, bufs, job_scratch[-1], step, n_steps)

    def cache_copies(head_group, batch, which):
        return (_cache_copies(kc_hbm, kc_buf, kc_sem, layer, batch, head_group * heads, which, heads)
                + _cache_copies(vc_hbm, vc_buf, vc_sem, layer, batch, head_group * heads, which, heads))

    @pl.when(step == 0)
    def _():
        for cp in cache_copies(hgi, b, slot):
            cp.start()

    @pl.when(step + 1 < n_steps)
    def _():
        last_b = b == n_b - 1
        next_b = jnp.where(last_b, 0, b + 1)
        next_hgi = jnp.where(last_b & (rb == n_rb - 1), hgi + 1, hgi)
        for cp in cache_copies(next_hgi, next_b, 1 - slot):
            cp.start()

    for cp in cache_copies(hgi, b, slot):
        cp.wait()

    slab0 = Q_ROWS * jnp.clip(rb - 1, 0, grid_rows // Q_ROWS - SLAB_BLOCKS)

    @pl.when(pl.program_id(2) == 0)
    def _():
        left = lax.broadcasted_iota(jnp.int32, (GRID_W, 2 * GRID_W), 1) < GRID_W
        for rq in range(Q_ROWS):
            r = rb * Q_ROWS + rq
            win0 = jnp.clip(r - WIN_H // 2, 0, grid_rows - WIN_H)

            def entry(kk):
                kr = slab0 + kk
                ok = (kr >= win0) & (kr < win0 + WIN_H)
                return jnp.where(ok, kr - r + (WIN_H - 1), N_REL_ROWS - 1)

            for kk in range(0, SLAB_ROWS, 2):
                i_left, i_right = entry(kk), entry(kk + 1)
                for hh in range(heads):
                    tile = jnp.where(left, tab_ref[hh, i_left], tab_ref[hh, i_right])
                    bias_ref[hh, rq * GRID_W:(rq + 1) * GRID_W, kk * GRID_W:(kk + 2) * GRID_W] = tile

    k_refs = (k0_ref, k1_ref, k2_ref)
    v_refs = (v0_ref, v1_ref, v2_ref)

    def attend(pieces):
        for hh in range(heads):
            sl = slice(hh * HEAD_DIM, (hh + 1) * HEAD_DIM)
            q = q_ref[:, sl]
            scores = [_qk(q, k_refs[p][:, sl]) + bias_ref[hh, :, p * Q_BLOCK:(p + 1) * Q_BLOCK]
                      for p in pieces]
            scores.append(_qk(q, kc_buf[slot, hh].astype(BF16)))
            exps = _softmax_exps(scores)
            o_ext = jnp.dot(exps[-1], _with_ones(vc_buf[slot, hh].astype(BF16)),
                            preferred_element_type=F32)
            for e, p in zip(exps, pieces):
                o_ext = o_ext + jnp.dot(e, _with_ones(v_refs[p][:, sl]), preferred_element_type=F32)
            o_ref[:, sl] = _normalise_gate(o_ext, z_ref[:, sl]).astype(o_ref.dtype)

    all_pieces = tuple(range(SLAB_BLOCKS))
    pl.when(rb == 0)(lambda: attend(all_pieces[:-1]))
    pl.when(rb == n_rb - 1)(lambda: attend(all_pieces[1:]))
    pl.when((rb > 0) & (rb < n_rb - 1))(lambda: attend(all_pieces))


def _latent_steps(m, d, seq_len):
    return (d // (_heads_per_step(d // HEAD_DIM) * HEAD_DIM)) * (seq_len // Q_BLOCK) * (m // seq_len)


def _attn_latent(q, k, v, z, k_ctx, v_ctx, layer, bias_tab, seq_len, jobs=()):
    m, d = q.shape
    batch = m // seq_len
    grid_rows = seq_len // GRID_W
    n_rb = grid_rows // Q_ROWS
    assert grid_rows % Q_ROWS == 0 and n_rb >= SLAB_BLOCKS
    past = k_ctx.shape[2]
    hg = _heads_per_step(d // HEAD_DIM)
    wblk = hg * HEAD_DIM

    q_spec = pl.BlockSpec((Q_BLOCK, wblk), lambda h, rb, b: (b * n_rb + rb, h))

    def slab_spec(p):
        return pl.BlockSpec(
            (Q_BLOCK, wblk),
            lambda h, rb, b: (b * n_rb + jnp.clip(rb - 1, 0, n_rb - SLAB_BLOCKS) + p, h))

    any_spec = pl.BlockSpec(memory_space=pl.ANY)
    tab_spec = pl.BlockSpec((hg, N_REL_ROWS, GRID_W, 2 * GRID_W), lambda h, rb, b: (h, 0, 0, 0))
    n_steps = _latent_steps(m, d, seq_len)
    out, *converted = pl.pallas_call(
        functools.partial(_attn_lat_kernel, heads=hg, grid_rows=grid_rows, layer=layer,
                          jobs=tuple(job._replace(src=None) for job in jobs), n_steps=n_steps),
        grid=(d // wblk, n_rb, batch),
        in_specs=[q_spec, slab_spec(0), slab_spec(1), slab_spec(2),
                  slab_spec(0), slab_spec(1), slab_spec(2),
                  any_spec, any_spec, q_spec, tab_spec] + [any_spec] * len(jobs),
        out_specs=[q_spec] + [any_spec] * len(jobs),
        out_shape=[jax.ShapeDtypeStruct((m, d), BF16)]
                  + _job_out_shapes(jobs),
        scratch_shapes=[pltpu.VMEM((hg, Q_BLOCK, SLAB_BLOCKS * Q_BLOCK), F32),
                        pltpu.VMEM((2, hg, past, HEAD_DIM), F32),
                        pltpu.VMEM((2, hg, past, HEAD_DIM), F32),
                        pltpu.SemaphoreType.DMA((2, hg)),
                        pltpu.SemaphoreType.DMA((2, hg))] + _job_scratch(jobs, n_steps),
        compiler_params=_cparams("arbitrary", "arbitrary", "arbitrary"),
        name="attn_latent",
    )(q, k, k, k, v, v, v, k_ctx, v_ctx, z, bias_tab, *[job.src for job in jobs])
    return out, converted


class _CastLedger:
    def __init__(self):
        self.ready = {}
        self.pending = {}

    def want(self, name, src, row0, rows, col0, cols, host, host_steps):
        job = _plan_cast(src, row0, rows, col0, cols, host_steps)
        if job is None:
            self.ready[name] = _cast_now(src, row0, rows, col0, cols)
        else:
            self.pending.setdefault(host, []).append((name, job))

    def jobs(self, host):
        return tuple(job for _, job in self.pending.get(host, ()))

    def done(self, host, converted):
        for (name, _), array in zip(self.pending.pop(host), converted):
            self.ready[name] = array


def _attn_projections(h, weight_of, d, tag, ledger, new_caches=None, cache_shape=None, layer=0):
    def project(part, kern):
        w, col0 = weight_of(part)
        host = f"{part}_{tag}"
        jobs = ledger.jobs(host)
        out = _matmul(kern, h, w, col0, d, BF16, "attn_in_" + part, jobs=jobs)
        if jobs:
            out, converted = out
            ledger.done(host, converted)
        return out

    plain = functools.partial(_mm_cast_kernel, scale=None)
    q = project("q", functools.partial(_mm_cast_kernel, scale=HEAD_DIM ** -0.5))
    if cache_shape is None:
        k, v = project("k", plain), project("v", plain)
    else:
        (wk, ck), (wv, cv) = weight_of("k"), weight_of("v")
        k, new_k = _matmul_to_cache(h, wk, ck, new_caches[0], cache_shape, layer, "attn_in_k_cache")
        v, new_v = _matmul_to_cache(h, wv, cv, new_caches[1], cache_shape, layer, "attn_in_v_cache")
        new_caches = (new_k, new_v)
    z = project("z", plain)
    return q, k, v, z, new_caches


def _pool_branch(h, w_in, w_grp, pool_scale_row, d, seq_len):
    dg = d // len(POOL_WINDOWS)
    pooled = _matmul(functools.partial(_mm_pool_kernel, seq_len=seq_len, cols_per_group=dg),
                     h, w_in, 0, d, BF16, "pool_in_u", tn_max=dg)
    return _gate_group_matmul(h, w_in, d, pooled, w_grp, pool_scale_row)


def kernel(x_prompt, x_sample, c, cache_k, cache_v, c_ctx, norm_g, w_ada, b_ada, w_in_attn, rpb,
           w_out_attn, w_in_pool, w_grp_pool, pool_scale, w_out_pool, final_norm_g):
    bc, tc, d = x_prompt.shape
    bl, tl, _ = x_sample.shape
    depth = norm_g.shape[0]
    n_heads = cache_k.shape[3]
    past = cache_k.shape[2]
    n_attn = w_in_attn.shape[0]
    assert n_heads * HEAD_DIM == d and tl % (GRID_W * Q_ROWS) == 0

    n_cond = -(-(1 + bl) // 8) * 8
    cond = jnp.concatenate([c_ctx[None, :], c, jnp.zeros((n_cond - 1 - bl, d), F32)], axis=0)
    mod = _ada_modulation(cond, w_ada, b_ada)
    mod4 = mod.reshape(depth, n_cond, 1, 3 * d)

    xc = x_prompt.reshape(bc * tc, d)
    xl = x_sample.reshape(bl * tl, d)
    g_rows = norm_g.reshape(depth, 1, d)

    n_pool = w_in_pool.shape[0]
    flat = lambda w: w.reshape(-1, w.shape[-1])
    steps_c, steps_l = _matmul_steps(bc * tc, d), _matmul_steps(bl * tl, d)
    ledger = _CastLedger()
    ledger.ready["attn0_q"] = _cast_columns(w_in_attn, 0, 0, d)
    ledger.ready["attn0_z"] = _cast_columns(w_in_attn, 0, 3 * d, d)
    ledger.want("attn0_k", flat(w_in_attn), 0, d, d, d, "q_c", steps_c)
    ledger.want("attn0_v", flat(w_in_attn), 0, d, 2 * d, d, "q_c", steps_c)
    ledger.want("out_attn", flat(w_out_attn), 0, n_attn * d, 0, d, "z_c", steps_c)
    pool_a = (n_pool // 2) * d if n_pool > 1 else n_pool * d
    ledger.want("in_pool_a", flat(w_in_pool), 0, pool_a, 0, 2 * d, "q_l", steps_l)
    ledger.want("in_pool_b", flat(w_in_pool), pool_a, n_pool * d - pool_a, 0, 2 * d, "k_l", steps_l)
    ledger.want("attn_rest_a", flat(w_in_attn), d, (n_attn - 1) * d, 0, 2 * d, "v_l", steps_l)
    ledger.want("attn_rest_b", flat(w_in_attn), d, (n_attn - 1) * d, 2 * d, 2 * d, "z_l", steps_l)
    lat_steps = _latent_steps(bl * tl, d, tl)
    ledger.want("out_pool", flat(w_out_pool), 0, n_pool * d, 0, d, "lat", lat_steps)
    ledger.want("grp_pool", flat(w_grp_pool), 0, flat(w_grp_pool).shape[0], 0, w_grp_pool.shape[-1],
                "lat", lat_steps)

    def attn_weight_of(j):
        def weight_of(part):
            if j == 0:
                return _Weight(ledger.ready["attn0_" + part][None], 0), 0
            half = ledger.ready["attn_rest_a" if part in "qk" else "attn_rest_b"]
            return _Weight(half.reshape(n_attn - 1, d, 2 * d), j - 1), (0 if part in "qv" else d)
        return weight_of

    def pool_weight(j):
        if j * d < pool_a:
            return _Weight(ledger.ready["in_pool_a"].reshape(-1, d, 2 * d), j)
        return _Weight(ledger.ready["in_pool_b"].reshape(-1, d, 2 * d), j - pool_a // d)

    cache_shape = (bc, n_attn, tc, n_heads, HEAD_DIM)
    new_caches = (None, None)
    for i in range(depth):
        j = i // 2
        hc = _modulate(xc, g_rows, mod4, i, tc, 0, False)
        hl = _modulate(xl, g_rows, mod4, i, tl, 1, True)
        if i % 2 == 0:
            weight_of = attn_weight_of(j)
            qc, kc, vc, zc, new_caches = _attn_projections(hc, weight_of, d, "c", ledger, new_caches,
                                                           cache_shape, j)
            oc = _attn_context(qc, kc, vc, zc, tc)
            ql, kl, vl, zl, _ = _attn_projections(hl, weight_of, d, "l", ledger)
            bias_tab = _rel_bias_table(rpb[j])
            jobs = ledger.jobs("lat")
            ol, converted = _attn_latent(ql, kl, vl, zl, cache_k, cache_v, j, bias_tab, tl, jobs=jobs)
            if jobs:
                ledger.done("lat", converted)
            w_out = _Weight(ledger.ready["out_attn"].reshape(n_attn, d, d), j)
        else:
            w_in = pool_weight(j)
            w_grp = _Weight(ledger.ready["grp_pool"].reshape(w_grp_pool.shape), j)
            ps_row = pool_scale[j].reshape(1, d)
            oc = _pool_branch(hc, w_in, w_grp, ps_row, d, tc)
            ol = _pool_branch(hl, w_in, w_grp, ps_row, d, tl)
            w_out = _Weight(ledger.ready["out_pool"].reshape(n_pool, d, d), j)
        xc = _out_proj_residual(oc, w_out, xc, mod4, i, tc, 0, False)
        xl = _out_proj_residual(ol, w_out, xl, mod4, i, tl, 1, True)

    y_prompt = _final_norm(xc, final_norm_g).reshape(bc, tc, d)
    y_sample = _final_norm(xl, final_norm_g).reshape(bl, tl, d)
    return (y_prompt, y_sample, new_caches[0], new_caches[1])
```

```python
import functools
from typing import NamedTuple

import jax
import jax.numpy as jnp
from jax import lax
from jax.experimental import pallas as pl
from jax.experimental.pallas import tpu as pltpu

F32 = jnp.float32
BF16 = jnp.bfloat16

LANES = 128
HEAD_DIM = 128
GRID_W = 64
WIN_H = 8
WIN_W = 16
POOL_WINDOWS = (2, 4, 8, 16)
RMS_EPS = 1e-6
NEG_INF = -1e30

Q_ROWS = 4
Q_BLOCK = Q_ROWS * GRID_W
SLAB_BLOCKS = 3
SLAB_ROWS = SLAB_BLOCKS * Q_ROWS
N_REL_ROWS = 2 * WIN_H
N_REL_COLS = 2 * WIN_W - 1
POOL_PAD = 8

VMEM_LIMIT = 58 * 1024 * 1024
MM_TILE = 1024
CTX_HEADS_PER_STEP = 16
PACKED_ROWS = 16
NORM_ROWS = PACKED_ROWS
NORM_BLOCK_ROWS = 512
RESID_NORM_ROWS = 512


def _cparams(*sem):
    return pltpu.CompilerParams(dimension_semantics=sem, vmem_limit_bytes=VMEM_LIMIT)


def _silu(x):
    return x / (1.0 + jnp.exp(-x))


def _ada_kernel(cond_ref, w_ref, b_ref, o_ref):
    s = _silu(cond_ref[...]).astype(BF16)
    o_ref[...] = jnp.dot(s, w_ref[...].astype(BF16), preferred_element_type=F32) + b_ref[...]


def _ada_modulation(cond, w_ada, b_ada):
    depth, d, d3 = w_ada.shape
    r = cond.shape[0]
    tn = min(512, d3)
    return pl.pallas_call(
        _ada_kernel,
        grid=(depth, d3 // tn),
        in_specs=[
            pl.BlockSpec((r, d), lambda i, n: (0, 0)),
            pl.BlockSpec((None, d, tn), lambda i, n: (i, 0, n)),
            pl.BlockSpec((None, 1, tn), lambda i, n: (i, 0, n)),
        ],
        out_specs=pl.BlockSpec((None, r, tn), lambda i, n: (i, 0, n)),
        out_shape=jax.ShapeDtypeStruct((depth, r, d3), F32),
        compiler_params=_cparams("parallel", "parallel"),
        name="ada_modulation",
    )(cond, w_ada, b_ada.reshape(depth, 1, d3))


def _modulate_kernel(x_ref, g_ref, sh_ref, sc_ref, o_ref, rinv_ref, gain_ref, shift_ref):
    x = x_ref[...]
    rinv = lax.rsqrt(jnp.mean(x * x, axis=-1, keepdims=True) + RMS_EPS)
    rinv_ref[...] = jnp.broadcast_to(rinv, rinv_ref.shape)
    gain_ref[...] = jnp.broadcast_to(g_ref[...] * (1.0 + sc_ref[...]), gain_ref.shape)
    shift_ref[...] = jnp.broadcast_to(sh_ref[...], shift_ref.shape)

    def chunk(i, carry):
        rows = pl.ds(pl.multiple_of(i * NORM_ROWS, NORM_ROWS), NORM_ROWS)
        r = rinv_ref[rows, :]
        for c in range(x_ref.shape[1] // LANES):
            cols = slice(c * LANES, (c + 1) * LANES)
            y = x_ref[rows, cols] * r * gain_ref[:, cols] + shift_ref[:, cols]
            o_ref[rows, cols] = y.astype(BF16)
        return carry

    lax.fori_loop(0, x_ref.shape[0] // NORM_ROWS, chunk, 0)


def _mod_row_map(layer, part, seq_len, rows_per_step, row0, per_batch, ncol_axis):
    def index(*ids):
        m = ids[0]
        row = row0 + (m * rows_per_step) // seq_len if per_batch else row0
        col = part if ncol_axis is None else part * ncol_axis[1] + ids[ncol_axis[0]]
        return (layer, row, 0, col)
    return index


def _modulate(x2d, g_row, mod4, layer, seq_len, row0, per_batch):
    m, d = x2d.shape
    tt = min(NORM_BLOCK_ROWS, seq_len if per_batch else m)
    return pl.pallas_call(
        _modulate_kernel,
        grid=(m // tt,),
        in_specs=[
            pl.BlockSpec((tt, d), lambda i: (i, 0)),
            pl.BlockSpec((None, 1, d), lambda i: (layer, 0, 0)),
            pl.BlockSpec((None, None, 1, d), _mod_row_map(layer, 0, seq_len, tt, row0, per_batch, None)),
            pl.BlockSpec((None, None, 1, d), _mod_row_map(layer, 1, seq_len, tt, row0, per_batch, None)),
        ],
        out_specs=pl.BlockSpec((tt, d), lambda i: (i, 0)),
        out_shape=jax.ShapeDtypeStruct((m, d), BF16),
        scratch_shapes=[pltpu.VMEM((tt, LANES), F32), pltpu.VMEM((NORM_ROWS, d), F32),
                        pltpu.VMEM((NORM_ROWS, d), F32)],
        compiler_params=_cparams("parallel"),
        name="modulate",
    )(x2d, g_row, mod4, mod4)


def _dot(a_ref, w_ref):
    return jnp.dot(a_ref[...], w_ref[...], preferred_element_type=F32)


def _mm_cast_kernel(a_ref, w_ref, o_ref, *, scale):
    acc = _dot(a_ref, w_ref)
    if scale is not None:
        acc = acc * scale
    o_ref[...] = acc.astype(o_ref.dtype)


def _window_sum(u, window):
    t, n = u.shape
    zeros = jnp.zeros((POOL_PAD, n), F32)
    up = jnp.concatenate([zeros, u, zeros], axis=0)
    size = t + 2 * POOL_PAD

    def ahead(x, k):
        return pltpu.roll(x, (-k) % size, axis=0)

    fwd, span = up, 1
    while 2 * span < window:
        fwd = fwd + ahead(fwd, span)
        span *= 2
    s = fwd + ahead(fwd, -span)
    return s[POOL_PAD:POOL_PAD + t, :]


def _mm_pool_kernel(a_ref, w_ref, o_ref, *, seq_len, cols_per_group):
    tm, tn = o_ref.shape
    group = (pl.program_id(1) * tn) // cols_per_group
    t = lax.broadcasted_iota(jnp.int32, (seq_len, 1), 0)
    for gi, window in enumerate(POOL_WINDOWS):
        @pl.when(group == gi)
        def _(window=window):
            acc = _dot(a_ref, w_ref)
            half = window // 2
            count = jnp.minimum(t + half, seq_len) - jnp.maximum(t - half, 0)
            inv_count = 1.0 / count.astype(F32)
            for s in range(tm // seq_len):
                u = acc[s * seq_len:(s + 1) * seq_len, :]
                d = _window_sum(u, window) * inv_count - u
                o_ref[s * seq_len:(s + 1) * seq_len, :] = d.astype(o_ref.dtype)


def _mm_gate_group_kernel(h_ref, wz_ref, d_ref, wg_ref, ps_ref, o_ref):
    z = _dot(h_ref, wz_ref)
    y = _dot(d_ref, wg_ref)
    o_ref[...] = (y * ps_ref[...] * _silu(z)).astype(o_ref.dtype)


def _mm_resid_norm_kernel(*refs, n_cols, modulated):
    if modulated:
        a_ref, w_ref, x_ref, gate_ref, g_ref, sh_ref, sc_ref = refs[:7]
        xo_ref, n_ref, stash, rinv_ref, gain_ref, shift_ref = refs[7:]
    else:
        a_ref, w_ref, x_ref, gate_ref, g_ref = refs[:5]
        n_ref, rinv_ref, gain_ref = refs[5:]
        stash = n_ref
    j = pl.program_id(1)
    tm, tn = x_ref.shape
    d = n_cols * tn
    xn = x_ref[...] + gate_ref[...] * _dot(a_ref, w_ref)
    if modulated:
        xo_ref[...] = xn
    sumsq = jnp.broadcast_to(jnp.sum(xn * xn, axis=-1, keepdims=True), rinv_ref.shape)
    for c in range(n_cols):
        @pl.when(j == c)
        def _(c=c):
            stash[:, c * tn:(c + 1) * tn] = xn

    @pl.when(j == 0)
    def _():
        rinv_ref[...] = sumsq

    @pl.when(j > 0)
    def _():
        rinv_ref[...] += sumsq

    @pl.when(j == n_cols - 1)
    def _():
        rinv_ref[...] = lax.rsqrt(rinv_ref[...] * (1.0 / d) + RMS_EPS)
        gain = g_ref[...] * (1.0 + sc_ref[...]) if modulated else g_ref[...]
        gain_ref[...] = jnp.broadcast_to(gain, gain_ref.shape)
        if modulated:
            shift_ref[...] = jnp.broadcast_to(sh_ref[...], shift_ref.shape)

        def chunk(i, carry):
            rows = pl.ds(pl.multiple_of(i * NORM_ROWS, NORM_ROWS), NORM_ROWS)
            r = rinv_ref[rows, :]
            for c in range(d // LANES):
                cols = slice(c * LANES, (c + 1) * LANES)
                y = stash[rows, cols] * r * gain_ref[:, cols]
                if modulated:
                    y = y + shift_ref[:, cols]
                n_ref[rows, cols] = y.astype(n_ref.dtype)
            return carry

        lax.fori_loop(0, tm // NORM_ROWS, chunk, 0)


def _tiles(m, n):
    return min(MM_TILE, m), min(MM_TILE, n)


class _Weight(NamedTuple):
    stack: jax.Array
    layer: int


def _mm_with_jobs_kernel(*refs, kern, n_in, jobs, n_steps):
    n_jobs = len(jobs)
    ins, srcs = refs[:n_in], refs[n_in:n_in + n_jobs]
    o_ref, dsts = refs[n_in + n_jobs], refs[n_in + n_jobs + 1:n_in + 2 * n_jobs + 1]
    scratch = refs[n_in + 2 * n_jobs + 1:]
    step = pl.program_id(0) * pl.num_programs(1) + pl.program_id(1)
    bufs = [scratch[2 * j:2 * j + 2] for j in range(n_jobs)]
    _cast_jobs_step(jobs, srcs, dsts, bufs, scratch[-1], step, n_steps)
    kern(*ins, o_ref)


def _matmul_steps(m, n_out):
    tm, tn = _tiles(m, n_out)
    return (m // tm) * (n_out // tn)


def _matmul(kern, a, w, w_col0, n_out, out_dtype, name, extra_in=(), extra_specs=(), tn_max=None,
            jobs=()):
    m, k = a.shape
    tm, tn = _tiles(m, n_out)
    if tn_max is not None:
        tn = min(tn, tn_max)
    nb0 = w_col0 // tn
    grid = (m // tm, n_out // tn)
    in_specs = [
        pl.BlockSpec((tm, k), lambda i, j: (i, 0)),
        pl.BlockSpec((None, k, tn), lambda i, j: (w.layer, 0, nb0 + j)),
        *[spec(tm, tn) for spec in extra_specs],
    ]
    out_spec = pl.BlockSpec((tm, tn), lambda i, j: (i, j))
    out_shape = jax.ShapeDtypeStruct((m, n_out), out_dtype)
    if not jobs:
        return pl.pallas_call(
            kern, grid=grid, in_specs=in_specs, out_specs=out_spec, out_shape=out_shape,
            compiler_params=_cparams("parallel", "parallel"), name=name,
        )(a, w.stack, *extra_in)
    n_steps = grid[0] * grid[1]
    any_spec = pl.BlockSpec(memory_space=pl.ANY)
    out, *converted = pl.pallas_call(
        functools.partial(_mm_with_jobs_kernel, kern=kern, n_in=len(in_specs),
                          jobs=tuple(job._replace(src=None) for job in jobs), n_steps=n_steps),
        grid=grid,
        in_specs=in_specs + [any_spec] * len(jobs),
        out_specs=[out_spec] + [any_spec] * len(jobs),
        out_shape=[out_shape] + _job_out_shapes(jobs),
        scratch_shapes=_job_scratch(jobs, n_steps),
        compiler_params=_cparams("arbitrary", "arbitrary"),
        name=name + "_cast",
    )(a, w.stack, *extra_in, *[job.src for job in jobs])
    return out, converted


def _cache_tile_copies(stage, zeros, sem, cache_hbm, slot, layer, fill_layers, batch0, head0):
    n_batch, _, tn = stage.shape[1:]
    copies = []
    for hh in range(tn // HEAD_DIM):
        dst = lambda lyr: cache_hbm.at[pl.ds(batch0, n_batch), lyr, :, head0 + hh, :]
        copies.append(pltpu.make_async_copy(stage.at[slot, :, :, pl.ds(hh * HEAD_DIM, HEAD_DIM)],
                                            dst(layer), sem.at[slot, 0, hh]))
        for f, lyr in enumerate(fill_layers):
            copies.append(pltpu.make_async_copy(zeros, dst(lyr), sem.at[slot, 1 + f, hh]))
    return copies


def _mm_cache_kernel(*refs, layer, fill_layers, n_steps):
    a_ref, w_ref = refs[:2]
    lo_ref, cache_hbm, stage, zeros, sem = refs[-5:]
    n_cols = pl.num_programs(1)
    step = pl.program_id(0) * n_cols + pl.program_id(1)
    slot = step % 2
    n_batch, _, tn = stage.shape[1:]

    def copies(s, which):
        return _cache_tile_copies(stage, zeros, sem, cache_hbm, which, layer, fill_layers,
                                  (s // n_cols) * n_batch, (s % n_cols) * (tn // HEAD_DIM))

    if fill_layers:
        @pl.when(step == 0)
        def _():
            zeros[...] = jnp.zeros(zeros.shape, zeros.dtype)

    @pl.when(step >= 2)
    def _():
        for cp in copies(step - 2, slot):
            cp.wait()

    acc = _dot(a_ref, w_ref)
    lo_ref[...] = acc.astype(lo_ref.dtype)
    stage[slot] = acc.reshape(stage.shape[1:])
    for cp in copies(step, slot):
        cp.start()

    @pl.when(step == n_steps - 1)
    def _():
        if n_steps > 1:
            for cp in copies(step - 1, 1 - slot):
                cp.wait()
        for cp in copies(step, slot):
            cp.wait()


def _matmul_to_cache(a, w, w_col0, cache, cache_shape, layer, name):
    m, k = a.shape
    _, n_layers, seq_len, n_heads, _ = cache_shape
    d = n_heads * HEAD_DIM
    tm, tn = _tiles(m, d)
    assert tm % seq_len == 0 and tn % HEAD_DIM == 0
    grid = (m // tm, d // tn)
    nb0 = w_col0 // tn
    any_spec = pl.BlockSpec(memory_space=pl.ANY)
    fresh = cache is None
    fill_layers = tuple(l for l in range(n_layers) if l != layer) if fresh else ()
    n_batch = tm // seq_len
    return pl.pallas_call(
        functools.partial(_mm_cache_kernel, layer=layer, fill_layers=fill_layers,
                          n_steps=grid[0] * grid[1]),
        grid=grid,
        in_specs=[pl.BlockSpec((tm, k), lambda i, j: (i, 0)),
                  pl.BlockSpec((None, k, tn), lambda i, j: (w.layer, 0, nb0 + j))]
                 + ([] if fresh else [any_spec]),
        out_specs=[pl.BlockSpec((tm, tn), lambda i, j: (i, j)), any_spec],
        out_shape=[jax.ShapeDtypeStruct((m, d), BF16), jax.ShapeDtypeStruct(cache_shape, F32)],
        scratch_shapes=[pltpu.VMEM((2, n_batch, seq_len, tn), F32),
                        pltpu.VMEM((n_batch, seq_len, HEAD_DIM), F32),
                        pltpu.SemaphoreType.DMA((2, 1 + len(fill_layers), tn // HEAD_DIM))],
        input_output_aliases={} if fresh else {2: 1},
        compiler_params=_cparams("arbitrary", "arbitrary"),
        name=name,
    )(a, w.stack, *([] if fresh else [cache]))


def _gate_group_matmul(h, w_in, z_col0, d, w_grp, pool_scale_row):
    m, dm = d.shape
    k = h.shape[1]
    _, g, dg, _ = w_grp.stack.shape
    tm = min(MM_TILE, m)
    zb0 = z_col0 // dg
    return pl.pallas_call(
        _mm_gate_group_kernel,
        grid=(m // tm, g),
        in_specs=[
            pl.BlockSpec((tm, k), lambda i, j: (i, 0)),
            pl.BlockSpec((None, k, dg), lambda i, j: (w_in.layer, 0, zb0 + j)),
            pl.BlockSpec((tm, dg), lambda i, j: (i, j)),
            pl.BlockSpec((None, None, dg, dg), lambda i, j: (w_grp.layer, j, 0, 0)),
            pl.BlockSpec((1, dg), lambda i, j: (0, j)),
        ],
        out_specs=pl.BlockSpec((tm, dg), lambda i, j: (i, j)),
        out_shape=jax.ShapeDtypeStruct((m, dm), BF16),
        compiler_params=_cparams("parallel", "parallel"),
        name="pool_gate_group",
    )(h, w_in.stack, d, w_grp.stack, pool_scale_row)


def _out_proj_norm(a, w_out, x2d, mod4, layer, seq_len, row0, per_batch, norm_gain, next_layer):
    m, d = x2d.shape
    k = a.shape[1]
    tm = min(RESID_NORM_ROWS, seq_len if per_batch else m)
    tn = min(MM_TILE, d)
    n_cols = d // tn
    modulated = next_layer is not None

    def mod_row(lyr, part, cols):
        return _mod_row_map(lyr, part, seq_len, tm, row0, per_batch, cols)

    in_specs = [
        pl.BlockSpec((tm, k), lambda i, j: (i, 0)),
        pl.BlockSpec((None, k, tn), lambda i, j: (w_out.layer, 0, j)),
        pl.BlockSpec((tm, tn), lambda i, j: (i, j)),
        pl.BlockSpec((None, None, 1, tn), mod_row(layer, 2, (1, n_cols))),
    ]
    row_block = pl.BlockSpec((tm, d), lambda i, j: (i, 0))
    small = [pltpu.VMEM((tm, LANES), F32), pltpu.VMEM((NORM_ROWS, d), F32)]
    if modulated:
        in_specs += [pl.BlockSpec((None, 1, d), lambda i, j: (next_layer, 0, 0)),
                     pl.BlockSpec((None, None, 1, d), mod_row(next_layer, 0, None)),
                     pl.BlockSpec((None, None, 1, d), mod_row(next_layer, 1, None))]
        operands = (a, w_out.stack, x2d, mod4, norm_gain, mod4, mod4)
        out_specs = [pl.BlockSpec((tm, tn), lambda i, j: (i, j)), row_block]
        out_shape = [jax.ShapeDtypeStruct((m, d), F32), jax.ShapeDtypeStruct((m, d), BF16)]
        scratch = [pltpu.VMEM((tm, d), F32)] + small + [pltpu.VMEM((NORM_ROWS, d), F32)]
    else:
        in_specs += [pl.BlockSpec((1, d), lambda i, j: (0, 0))]
        operands = (a, w_out.stack, x2d, mod4, norm_gain)
        out_specs = row_block
        out_shape = jax.ShapeDtypeStruct((m, d), F32)
        scratch = small
    return pl.pallas_call(
        functools.partial(_mm_resid_norm_kernel, n_cols=n_cols, modulated=modulated),
        grid=(m // tm, n_cols),
        in_specs=in_specs, out_specs=out_specs, out_shape=out_shape, scratch_shapes=scratch,
        compiler_params=_cparams("parallel", "arbitrary"),
        name="out_proj_norm" if modulated else "out_proj_final_norm",
    )(*operands)


def _softmax_exps(scores):
    m = scores[0].max(axis=-1, keepdims=True)
    for s in scores[1:]:
        m = jnp.maximum(m, s.max(axis=-1, keepdims=True))
    return [jnp.exp(s - m).astype(BF16) for s in scores]


def _qk(q, k):
    return lax.dot_general(q, k, (((1,), (1,)), ((), ())), preferred_element_type=F32)


def _with_ones(v):
    return jnp.concatenate([v, jnp.ones(v.shape, v.dtype)], axis=1)


def _normalise_gate(o_ext, z):
    o = o_ext[:, :HEAD_DIM] / o_ext[:, HEAD_DIM:]
    return o * _silu(z.astype(F32))


def _attn_ctx_kernel(q_ref, k_ref, v_ref, z_ref, o_ref, *, heads):
    for hh in range(heads):
        sl = slice(hh * HEAD_DIM, (hh + 1) * HEAD_DIM)
        k = k_ref[:, sl].astype(BF16)
        v = _with_ones(v_ref[:, sl].astype(BF16))
        (e,) = _softmax_exps([_qk(q_ref[:, sl], k)])
        o_ext = jnp.dot(e, v, preferred_element_type=F32)
        o_ref[:, sl] = _normalise_gate(o_ext, z_ref[:, sl]).astype(o_ref.dtype)


def _heads_per_step(n_heads, most=8):
    return min(most, n_heads)


def _attn_context(q, k, v, z, seq_len):
    m, d = q.shape
    hg = _heads_per_step(d // HEAD_DIM, most=CTX_HEADS_PER_STEP)
    wblk = hg * HEAD_DIM
    spec = pl.BlockSpec((seq_len, wblk), lambda b, h: (b, h))
    return pl.pallas_call(
        functools.partial(_attn_ctx_kernel, heads=hg),
        grid=(m // seq_len, d // wblk),
        in_specs=[spec, spec, spec, spec],
        out_specs=spec,
        out_shape=jax.ShapeDtypeStruct((m, d), BF16),
        compiler_params=_cparams("parallel", "parallel"),
        name="attn_context",
    )(q, k, v, z)


def _rel_bias_kernel(rpb_ref, o_ref):
    rows, lanes = o_ref.shape
    lane = lax.broadcasted_iota(jnp.int32, (1, lanes), 1)
    qc = lane // (2 * GRID_W)
    kc = lane % GRID_W
    rel = jnp.clip(kc - qc, -(WIN_W - 1), WIN_W - 1) + (WIN_W - 1)
    start = jnp.clip(qc - WIN_W // 2, 0, GRID_W - WIN_W)
    in_window = (kc >= start) & (kc < start + WIN_W)
    table = rpb_ref[...]
    acc = jnp.zeros((rows, lanes), F32)
    for j in range(N_REL_COLS):
        acc = jnp.where(rel == j, table[:, j:j + 1], acc)
    row = lax.broadcasted_iota(jnp.int32, (rows, 1), 0)
    masked_row = (row % N_REL_ROWS) == (N_REL_ROWS - 1)
    o_ref[...] = jnp.where(in_window & jnp.logical_not(masked_row), acc, NEG_INF)


def _rel_bias_table(rpb_layer):
    h = rpb_layer.shape[0]
    padded = jnp.pad(rpb_layer, ((0, 0), (0, 1), (0, 0))).reshape(h * N_REL_ROWS, N_REL_COLS)
    rows = h * N_REL_ROWS
    tr = min(64, rows)
    lanes = GRID_W * 2 * GRID_W
    flat = pl.pallas_call(
        _rel_bias_kernel,
        grid=(rows // tr,),
        in_specs=[pl.BlockSpec((tr, N_REL_COLS), lambda i: (i, 0))],
        out_specs=pl.BlockSpec((tr, lanes), lambda i: (i, 0)),
        out_shape=jax.ShapeDtypeStruct((rows, lanes), F32),
        compiler_params=_cparams("parallel"),
        name="rel_bias_table",
    )(padded)
    return flat.reshape(h, N_REL_ROWS, GRID_W, 2 * GRID_W)


def _cache_copies(cache_hbm, buf, sem, layer, b, head0, slot, heads):
    return [pltpu.make_async_copy(cache_hbm.at[b, layer, :, head0 + hh, :], buf.at[slot, hh],
                                  sem.at[slot, hh]) for hh in range(heads)]


class _CastJob(NamedTuple):
    src: jax.Array
    row0: int
    rows: int
    col0: int
    cols: int


def _plan_cast(src, row0, rows, col0, cols, n_steps):
    even = rows > 0 and rows % (n_steps * PACKED_ROWS) == 0 and cols % LANES == 0 and col0 % LANES == 0
    return _CastJob(src, row0, rows, col0, cols) if even else None


def _cast_now(src, row0, rows, col0, cols):
    return src[row0:row0 + rows, col0:col0 + cols].astype(BF16)


def _cast_block_kernel(x_ref, o_ref):
    o_ref[...] = x_ref[...].astype(o_ref.dtype)


def _cast_columns(stack, layer, col0, cols):
    _, k, _ = stack.shape
    tr = min(512, k)
    assert col0 % cols == 0
    return pl.pallas_call(
        _cast_block_kernel,
        grid=(k // tr,),
        in_specs=[pl.BlockSpec((None, tr, cols), lambda i: (layer, i, col0 // cols))],
        out_specs=pl.BlockSpec((tr, cols), lambda i: (i, 0)),
        out_shape=jax.ShapeDtypeStruct((k, cols), BF16),
        compiler_params=_cparams("parallel"),
        name="cast_columns",
    )(stack)


def _job_scratch(jobs, n_steps):
    scratch = []
    for job in jobs:
        chunk = (2, job.rows // n_steps, job.cols)
        scratch += [pltpu.VMEM(chunk, F32), pltpu.VMEM(chunk, BF16)]
    if jobs:
        scratch.append(pltpu.SemaphoreType.DMA((len(jobs), 2, 2)))
    return scratch


def _job_out_shapes(jobs):
    return [jax.ShapeDtypeStruct((job.rows, job.cols), BF16) for job in jobs]


def _cast_jobs_step(jobs, srcs, dsts, bufs, sem, step, n_steps):
    slot = step % 2
    per_step = [job.rows // n_steps for job in jobs]

    def load(j, s, which):
        rows = pl.ds(jobs[j].row0 + s * per_step[j], per_step[j])
        cols = pl.ds(jobs[j].col0, jobs[j].cols)
        return pltpu.make_async_copy(srcs[j].at[rows, cols], bufs[j][0].at[which], sem.at[j, 0, which])

    def store(j, s, which):
        rows = pl.ds(s * per_step[j], per_step[j])
        return pltpu.make_async_copy(bufs[j][1].at[which], dsts[j].at[rows, :], sem.at[j, 1, which])

    n_jobs = range(len(jobs))

    @pl.when(step == 0)
    def _():
        for j in n_jobs:
            load(j, step, slot).start()

    @pl.when(step + 1 < n_steps)
    def _():
        for j in n_jobs:
            load(j, step + 1, 1 - slot).start()

    @pl.when(step >= 2)
    def _():
        for j in n_jobs:
            store(j, step - 2, slot).wait()

    for j in n_jobs:
        load(j, step, slot).wait()
        bufs[j][1][slot] = bufs[j][0][slot].astype(BF16)
        store(j, step, slot).start()

    @pl.when(step == n_steps - 1)
    def _():
        for j in n_jobs:
            if n_steps > 1:
                store(j, step - 1, 1 - slot).wait()
            store(j, step, slot).wait()


def _attn_lat_kernel(*refs, heads, grid_rows, layer, jobs, n_steps):
    n_jobs = len(jobs)
    (q_ref, k0_ref, k1_ref, k2_ref, v0_ref, v1_ref, v2_ref, kc_hbm, vc_hbm, z_ref, tab_ref) = refs[:11]
    job_srcs = refs[11:11 + n_jobs]
    o_ref = refs[11 + n_jobs]
    job_dsts = refs[12 + n_jobs:12 + 2 * n_jobs]
    bias_ref, kc_buf, vc_buf, kc_sem, vc_sem = refs[12 + 2 * n_jobs:17 + 2 * n_jobs]
    job_scratch = refs[17 + 2 * n_jobs:]
    hgi, rb, b = pl.program_id(0), pl.program_id(1), pl.program_id(2)
    n_rb, n_b = pl.num_programs(1), pl.num_programs(2)
    step = (hgi * n_rb + rb) * n_b + b
    slot = step % 2

    if n_jobs:
        bufs = [job_scratch[2 * j:2 * j + 2] for j in range(n_jobs)]
        _cast_jobs_step(jobs, job_srcs, job_dsts, bufs, job_scratch[-1], step, n_steps)

    def cache_copies(head_group, batch, which):
        return (_cache_copies(kc_hbm, kc_buf, kc_sem, layer, batch, head_group * heads, which, heads)
                + _cache_copies(vc_hbm, vc_buf, vc_sem, layer, batch, head_group * heads, which, heads))

    @pl.when(step == 0)
    def _():
        for cp in cache_copies(hgi, b, slot):
            cp.start()

    @pl.when(step + 1 < n_steps)
    def _():
        last_b = b == n_b - 1
        next_b = jnp.where(last_b, 0, b + 1)
        next_hgi = jnp.where(last_b & (rb == n_rb - 1), hgi + 1, hgi)
        for cp in cache_copies(next_hgi, next_b, 1 - slot):
            cp.start()

    for cp in cache_copies(hgi, b, slot):
        cp.wait()

    slab0 = Q_ROWS * jnp.clip(rb - 1, 0, grid_rows // Q_ROWS - SLAB_BLOCKS)

    @pl.when(pl.program_id(2) == 0)
    def _():
        left = lax.broadcasted_iota(jnp.int32, (GRID_W, 2 * GRID_W), 1) < GRID_W
        for rq in range(Q_ROWS):
            r = rb * Q_ROWS + rq
            win0 = jnp.clip(r - WIN_H // 2, 0, grid_rows - WIN_H)

            def entry(kk):
                kr = slab0 + kk
                ok = (kr >= win0) & (kr < win0 + WIN_H)
                return jnp.where(ok, kr - r + (WIN_H - 1), N_REL_ROWS - 1)

            for kk in range(0, SLAB_ROWS, 2):
                i_left, i_right = entry(kk), entry(kk + 1)
                for hh in range(heads):
                    tile = jnp.where(left, tab_ref[hh, i_left], tab_ref[hh, i_right])
                    bias_ref[hh, rq * GRID_W:(rq + 1) * GRID_W, kk * GRID_W:(kk + 2) * GRID_W] = tile

    k_refs = (k0_ref, k1_ref, k2_ref)
    v_refs = (v0_ref, v1_ref, v2_ref)

    def attend(pieces):
        for hh in range(heads):
            sl = slice(hh * HEAD_DIM, (hh + 1) * HEAD_DIM)
            q = q_ref[:, sl]
            scores = [_qk(q, k_refs[p][:, sl]) + bias_ref[hh, :, p * Q_BLOCK:(p + 1) * Q_BLOCK]
                      for p in pieces]
            scores.append(_qk(q, kc_buf[slot, hh].astype(BF16)))
            exps = _softmax_exps(scores)
            o_ext = jnp.dot(exps[-1], _with_ones(vc_buf[slot, hh].astype(BF16)),
                            preferred_element_type=F32)
            for e, p in zip(exps, pieces):
                o_ext = o_ext + jnp.dot(e, _with_ones(v_refs[p][:, sl]), preferred_element_type=F32)
            o_ref[:, sl] = _normalise_gate(o_ext, z_ref[:, sl]).astype(o_ref.dtype)

    all_pieces = tuple(range(SLAB_BLOCKS))
    pl.when(rb == 0)(lambda: attend(all_pieces[:-1]))
    pl.when(rb == n_rb - 1)(lambda: attend(all_pieces[1:]))
    pl.when((rb > 0) & (rb < n_rb - 1))(lambda: attend(all_pieces))


def _latent_steps(m, d, seq_len):
    return (d // (_heads_per_step(d // HEAD_DIM) * HEAD_DIM)) * (seq_len // Q_BLOCK) * (m // seq_len)


def _attn_latent(q, k, v, z, k_ctx, v_ctx, layer, bias_tab, seq_len, jobs=()):
    m, d = q.shape
    batch = m // seq_len
    grid_rows = seq_len // GRID_W
    n_rb = grid_rows // Q_ROWS
    assert grid_rows % Q_ROWS == 0 and n_rb >= SLAB_BLOCKS
    past = k_ctx.shape[2]
    hg = _heads_per_step(d // HEAD_DIM)
    wblk = hg * HEAD_DIM

    q_spec = pl.BlockSpec((Q_BLOCK, wblk), lambda h, rb, b: (b * n_rb + rb, h))

    def slab_spec(p):
        return pl.BlockSpec(
            (Q_BLOCK, wblk),
            lambda h, rb, b: (b * n_rb + jnp.clip(rb - 1, 0, n_rb - SLAB_BLOCKS) + p, h))

    any_spec = pl.BlockSpec(memory_space=pl.ANY)
    tab_spec = pl.BlockSpec((hg, N_REL_ROWS, GRID_W, 2 * GRID_W), lambda h, rb, b: (h, 0, 0, 0))
    n_steps = _latent_steps(m, d, seq_len)
    out, *converted = pl.pallas_call(
        functools.partial(_attn_lat_kernel, heads=hg, grid_rows=grid_rows, layer=layer,
                          jobs=tuple(job._replace(src=None) for job in jobs), n_steps=n_steps),
        grid=(d // wblk, n_rb, batch),
        in_specs=[q_spec, slab_spec(0), slab_spec(1), slab_spec(2),
                  slab_spec(0), slab_spec(1), slab_spec(2),
                  any_spec, any_spec, q_spec, tab_spec] + [any_spec] * len(jobs),
        out_specs=[q_spec] + [any_spec] * len(jobs),
        out_shape=[jax.ShapeDtypeStruct((m, d), BF16)]
                  + _job_out_shapes(jobs),
        scratch_shapes=[pltpu.VMEM((hg, Q_BLOCK, SLAB_BLOCKS * Q_BLOCK), F32),
                        pltpu.VMEM((2, hg, past, HEAD_DIM), F32),
                        pltpu.VMEM((2, hg, past, HEAD_DIM), F32),
                        pltpu.SemaphoreType.DMA((2, hg)),
                        pltpu.SemaphoreType.DMA((2, hg))] + _job_scratch(jobs, n_steps),
        compiler_params=_cparams("arbitrary", "arbitrary", "arbitrary"),
        name="attn_latent",
    )(q, k, k, k, v, v, v, k_ctx, v_ctx, z, bias_tab, *[job.src for job in jobs])
    return out, converted


class _CastLedger:
    def __init__(self):
        self.ready = {}
        self.pending = {}

    def want(self, name, src, row0, rows, col0, cols, host, host_steps):
        job = _plan_cast(src, row0, rows, col0, cols, host_steps)
        if job is None:
            self.ready[name] = _cast_now(src, row0, rows, col0, cols)
        else:
            self.pending.setdefault(host, []).append((name, job))

    def jobs(self, host):
        return tuple(job for _, job in self.pending.get(host, ()))

    def done(self, host, converted):
        for (name, _), array in zip(self.pending.pop(host), converted):
            self.ready[name] = array


def _attn_projections(h, weight_of, d, tag, ledger, new_caches=None, cache_shape=None, layer=0):
    def project(part, kern):
        w, col0 = weight_of(part)
        host = f"{part}_{tag}"
        jobs = ledger.jobs(host)
        out = _matmul(kern, h, w, col0, d, BF16, "attn_in_" + part, jobs=jobs)
        if jobs:
            out, converted = out
            ledger.done(host, converted)
        return out

    plain = functools.partial(_mm_cast_kernel, scale=None)
    q = project("q", functools.partial(_mm_cast_kernel, scale=HEAD_DIM ** -0.5))
    if cache_shape is None:
        k, v = project("k", plain), project("v", plain)
    else:
        (wk, ck), (wv, cv) = weight_of("k"), weight_of("v")
        k, new_k = _matmul_to_cache(h, wk, ck, new_caches[0], cache_shape, layer, "attn_in_k_cache")
        v, new_v = _matmul_to_cache(h, wv, cv, new_caches[1], cache_shape, layer, "attn_in_v_cache")
        new_caches = (new_k, new_v)
    z = project("z", plain)
    return q, k, v, z, new_caches


def _pool_branch(h, w_in, w_grp, pool_scale_row, d, seq_len):
    dg = d // len(POOL_WINDOWS)
    pooled = _matmul(functools.partial(_mm_pool_kernel, seq_len=seq_len, cols_per_group=dg),
                     h, w_in, 0, d, BF16, "pool_in_u", tn_max=dg)
    return _gate_group_matmul(h, w_in, d, pooled, w_grp, pool_scale_row)


def kernel(x_prompt, x_sample, c, cache_k, cache_v, c_ctx, norm_g, w_ada, b_ada, w_in_attn, rpb,
           w_out_attn, w_in_pool, w_grp_pool, pool_scale, w_out_pool, final_norm_g):
    bc, tc, d = x_prompt.shape
    bl, tl, _ = x_sample.shape
    depth = norm_g.shape[0]
    n_heads = cache_k.shape[3]
    past = cache_k.shape[2]
    n_attn = w_in_attn.shape[0]
    assert n_heads * HEAD_DIM == d and tl % (GRID_W * Q_ROWS) == 0

    n_cond = -(-(1 + bl) // 8) * 8
    cond = jnp.concatenate([c_ctx[None, :], c, jnp.zeros((n_cond - 1 - bl, d), F32)], axis=0)
    mod = _ada_modulation(cond, w_ada, b_ada)
    mod4 = mod.reshape(depth, n_cond, 1, 3 * d)

    xc = x_prompt.reshape(bc * tc, d)
    xl = x_sample.reshape(bl * tl, d)
    g_rows = norm_g.reshape(depth, 1, d)

    n_pool = w_in_pool.shape[0]
    flat = lambda w: w.reshape(-1, w.shape[-1])
    steps_c, steps_l = _matmul_steps(bc * tc, d), _matmul_steps(bl * tl, d)
    ledger = _CastLedger()
    ledger.ready["attn0_q"] = _cast_columns(w_in_attn, 0, 0, d)
    ledger.ready["attn0_z"] = _cast_columns(w_in_attn, 0, 3 * d, d)
    ledger.want("attn0_k", flat(w_in_attn), 0, d, d, d, "q_c", steps_c)
    ledger.want("attn0_v", flat(w_in_attn), 0, d, 2 * d, d, "q_c", steps_c)
    ledger.want("out_attn", flat(w_out_attn), 0, n_attn * d, 0, d, "z_c", steps_c)
    pool_a = (n_pool // 2) * d if n_pool > 1 else n_pool * d
    ledger.want("in_pool_a", flat(w_in_pool), 0, pool_a, 0, 2 * d, "q_l", steps_l)
    ledger.want("in_pool_b", flat(w_in_pool), pool_a, n_pool * d - pool_a, 0, 2 * d, "k_l", steps_l)
    ledger.want("attn_rest_a", flat(w_in_attn), d, (n_attn - 1) * d, 0, 2 * d, "v_l", steps_l)
    ledger.want("attn_rest_b", flat(w_in_attn), d, (n_attn - 1) * d, 2 * d, 2 * d, "z_l", steps_l)
    lat_steps = _latent_steps(bl * tl, d, tl)
    ledger.want("out_pool", flat(w_out_pool), 0, n_pool * d, 0, d, "lat", lat_steps)
    ledger.want("grp_pool", flat(w_grp_pool), 0, flat(w_grp_pool).shape[0], 0, w_grp_pool.shape[-1],
                "lat", lat_steps)

    def attn_weight_of(j):
        def weight_of(part):
            if j == 0:
                return _Weight(ledger.ready["attn0_" + part][None], 0), 0
            half = ledger.ready["attn_rest_a" if part in "qk" else "attn_rest_b"]
            return _Weight(half.reshape(n_attn - 1, d, 2 * d), j - 1), (0 if part in "qv" else d)
        return weight_of

    def pool_weight(j):
        if j * d < pool_a:
            return _Weight(ledger.ready["in_pool_a"].reshape(-1, d, 2 * d), j)
        return _Weight(ledger.ready["in_pool_b"].reshape(-1, d, 2 * d), j - pool_a // d)

    cache_shape = (bc, n_attn, tc, n_heads, HEAD_DIM)
    new_caches = (None, None)
    hc = _modulate(xc, g_rows, mod4, 0, tc, 0, False)
    hl = _modulate(xl, g_rows, mod4, 0, tl, 1, True)
    for i in range(depth):
        j = i // 2
        if i % 2 == 0:
            weight_of = attn_weight_of(j)
            qc, kc, vc, zc, new_caches = _attn_projections(hc, weight_of, d, "c", ledger, new_caches,
                                                           cache_shape, j)
            oc = _attn_context(qc, kc, vc, zc, tc)
            ql, kl, vl, zl, _ = _attn_projections(hl, weight_of, d, "l", ledger)
            bias_tab = _rel_bias_table(rpb[j])
            jobs = ledger.jobs("lat")
            ol, converted = _attn_latent(ql, kl, vl, zl, cache_k, cache_v, j, bias_tab, tl, jobs=jobs)
            if jobs:
                ledger.done("lat", converted)
            w_out = _Weight(ledger.ready["out_attn"].reshape(n_attn, d, d), j)
        else:
            w_in = pool_weight(j)
            w_grp = _Weight(ledger.ready["grp_pool"].reshape(w_grp_pool.shape), j)
            ps_row = pool_scale[j].reshape(1, d)
            oc = _pool_branch(hc, w_in, w_grp, ps_row, d, tc)
            ol = _pool_branch(hl, w_in, w_grp, ps_row, d, tl)
            w_out = _Weight(ledger.ready["out_pool"].reshape(n_pool, d, d), j)
        if i + 1 < depth:
            xc, hc = _out_proj_norm(oc, w_out, xc, mod4, i, tc, 0, False, g_rows, i + 1)
            xl, hl = _out_proj_norm(ol, w_out, xl, mod4, i, tl, 1, True, g_rows, i + 1)
        else:
            final_gain = final_norm_g.reshape(1, d)
            xc = _out_proj_norm(oc, w_out, xc, mod4, i, tc, 0, False, final_gain, None)
            xl = _out_proj_norm(ol, w_out, xl, mod4, i, tl, 1, True, final_gain, None)

    return (xc.reshape(bc, tc, d), xl.reshape(bl, tl, d), new_caches[0], new_caches[1])
```

```python
import functools
from typing import NamedTuple

import jax
import jax.numpy as jnp
from jax import lax
from jax.experimental import pallas as pl
from jax.experimental.pallas import tpu as pltpu

F32 = jnp.float32
BF16 = jnp.bfloat16

LANES = 128
HEAD_DIM = 128
GRID_W = 64
WIN_H = 8
WIN_W = 16
POOL_WINDOWS = (2, 4, 8, 16)
RMS_EPS = 1e-6
NEG_INF = -1e30

Q_ROWS = 4
Q_BLOCK = Q_ROWS * GRID_W
SLAB_BLOCKS = 3
SLAB_ROWS = SLAB_BLOCKS * Q_ROWS
N_REL_ROWS = 2 * WIN_H
N_REL_COLS = 2 * WIN_W - 1
POOL_PAD = 8

VMEM_LIMIT = 58 * 1024 * 1024
MM_TILE = 1024
CTX_HEADS_PER_STEP = 16
PACKED_ROWS = 16
NORM_ROWS = PACKED_ROWS
NORM_BLOCK_ROWS = 512
RESID_NORM_ROWS = 512


def _cparams(*sem):
    return pltpu.CompilerParams(dimension_semantics=sem, vmem_limit_bytes=VMEM_LIMIT)


def _silu(x):
    return x / (1.0 + jnp.exp(-x))


def _ada_kernel(cond_ref, w_ref, b_ref, o_ref):
    s = _silu(cond_ref[...]).astype(BF16)
    o_ref[...] = jnp.dot(s, w_ref[...].astype(BF16), preferred_element_type=F32) + b_ref[...]


def _ada_modulation(cond, w_ada, b_ada):
    depth, d, d3 = w_ada.shape
    r = cond.shape[0]
    tn = min(512, d3)
    return pl.pallas_call(
        _ada_kernel,
        grid=(depth, d3 // tn),
        in_specs=[
            pl.BlockSpec((r, d), lambda i, n: (0, 0)),
            pl.BlockSpec((None, d, tn), lambda i, n: (i, 0, n)),
            pl.BlockSpec((None, 1, tn), lambda i, n: (i, 0, n)),
        ],
        out_specs=pl.BlockSpec((None, r, tn), lambda i, n: (i, 0, n)),
        out_shape=jax.ShapeDtypeStruct((depth, r, d3), F32),
        compiler_params=_cparams("parallel", "parallel"),
        name="ada_modulation",
    )(cond, w_ada, b_ada.reshape(depth, 1, d3))


def _modulate_kernel(x_ref, g_ref, sh_ref, sc_ref, o_ref, rinv_ref, gain_ref, shift_ref):
    x = x_ref[...]
    rinv = lax.rsqrt(jnp.mean(x * x, axis=-1, keepdims=True) + RMS_EPS)
    rinv_ref[...] = jnp.broadcast_to(rinv, rinv_ref.shape)
    gain_ref[...] = jnp.broadcast_to(g_ref[...] * (1.0 + sc_ref[...]), gain_ref.shape)
    shift_ref[...] = jnp.broadcast_to(sh_ref[...], shift_ref.shape)

    def chunk(i, carry):
        rows = pl.ds(pl.multiple_of(i * NORM_ROWS, NORM_ROWS), NORM_ROWS)
        r = rinv_ref[rows, :]
        for c in range(x_ref.shape[1] // LANES):
            cols = slice(c * LANES, (c + 1) * LANES)
            y = x_ref[rows, cols] * r * gain_ref[:, cols] + shift_ref[:, cols]
            o_ref[rows, cols] = y.astype(BF16)
        return carry

    lax.fori_loop(0, x_ref.shape[0] // NORM_ROWS, chunk, 0)


def _mod_row_map(layer, part, seq_len, rows_per_step, row0, per_batch, ncol_axis):
    def index(*ids):
        m = ids[0]
        row = row0 + (m * rows_per_step) // seq_len if per_batch else row0
        col = part if ncol_axis is None else part * ncol_axis[1] + ids[ncol_axis[0]]
        return (layer, row, 0, col)
    return index


def _modulate(x2d, g_row, mod4, layer, seq_len, row0, per_batch):
    m, d = x2d.shape
    tt = min(NORM_BLOCK_ROWS, seq_len if per_batch else m)
    return pl.pallas_call(
        _modulate_kernel,
        grid=(m // tt,),
        in_specs=[
            pl.BlockSpec((tt, d), lambda i: (i, 0)),
            pl.BlockSpec((None, 1, d), lambda i: (layer, 0, 0)),
            pl.BlockSpec((None, None, 1, d), _mod_row_map(layer, 0, seq_len, tt, row0, per_batch, None)),
            pl.BlockSpec((None, None, 1, d), _mod_row_map(layer, 1, seq_len, tt, row0, per_batch, None)),
        ],
        out_specs=pl.BlockSpec((tt, d), lambda i: (i, 0)),
        out_shape=jax.ShapeDtypeStruct((m, d), BF16),
        scratch_shapes=[pltpu.VMEM((tt, LANES), F32), pltpu.VMEM((NORM_ROWS, d), F32),
                        pltpu.VMEM((NORM_ROWS, d), F32)],
        compiler_params=_cparams("parallel"),
        name="modulate",
    )(x2d, g_row, mod4, mod4)


def _dot(a_ref, w_ref):
    return jnp.dot(a_ref[...], w_ref[...], preferred_element_type=F32)


def _mm_cast_kernel(a_ref, w_ref, o_ref, *, scale):
    acc = _dot(a_ref, w_ref)
    if scale is not None:
        acc = acc * scale
    o_ref[...] = acc.astype(o_ref.dtype)


def _window_sum(u, window):
    t, n = u.shape
    zeros = jnp.zeros((POOL_PAD, n), F32)
    up = jnp.concatenate([zeros, u, zeros], axis=0)
    size = t + 2 * POOL_PAD

    def ahead(x, k):
        return pltpu.roll(x, (-k) % size, axis=0)

    fwd, span = up, 1
    while 2 * span < window:
        fwd = fwd + ahead(fwd, span)
        span *= 2
    s = fwd + ahead(fwd, -span)
    return s[POOL_PAD:POOL_PAD + t, :]


def _mm_pool_kernel(a_ref, w_ref, o_ref, *, seq_len, cols_per_group):
    tm, tn = o_ref.shape
    group = (pl.program_id(1) * tn) // cols_per_group
    t = lax.broadcasted_iota(jnp.int32, (seq_len, 1), 0)
    for gi, window in enumerate(POOL_WINDOWS):
        @pl.when(group == gi)
        def _(window=window):
            acc = _dot(a_ref, w_ref)
            half = window // 2
            count = jnp.minimum(t + half, seq_len) - jnp.maximum(t - half, 0)
            inv_count = 1.0 / count.astype(F32)
            for s in range(tm // seq_len):
                u = acc[s * seq_len:(s + 1) * seq_len, :]
                d = _window_sum(u, window) * inv_count - u
                o_ref[s * seq_len:(s + 1) * seq_len, :] = d.astype(o_ref.dtype)


def _mm_gate_group_kernel(h_ref, wz_ref, d_ref, wg_ref, ps_ref, o_ref):
    z = _dot(h_ref, wz_ref)
    y = _dot(d_ref, wg_ref)
    o_ref[...] = (y * ps_ref[...] * _silu(z)).astype(o_ref.dtype)


def _mm_resid_norm_kernel(*refs, n_cols, modulated):
    if modulated:
        a_ref, w_ref, x_ref, gate_ref, g_ref, sh_ref, sc_ref = refs[:7]
        xo_ref, n_ref, stash, rinv_ref, gain_ref, shift_ref = refs[7:]
    else:
        a_ref, w_ref, x_ref, gate_ref, g_ref = refs[:5]
        n_ref, rinv_ref, gain_ref = refs[5:]
        stash = n_ref
    j = pl.program_id(1)
    tm, tn = x_ref.shape
    d = n_cols * tn
    xn = x_ref[...] + gate_ref[...] * _dot(a_ref, w_ref)
    if modulated:
        xo_ref[...] = xn
    sumsq = jnp.broadcast_to(jnp.sum(xn * xn, axis=-1, keepdims=True), rinv_ref.shape)
    stash[:, pl.ds(pl.multiple_of(j * tn, tn), tn)] = xn

    @pl.when(j == 0)
    def _():
        rinv_ref[...] = sumsq

    @pl.when(j > 0)
    def _():
        rinv_ref[...] += sumsq

    @pl.when(j == n_cols - 1)
    def _():
        rinv_ref[...] = lax.rsqrt(rinv_ref[...] * (1.0 / d) + RMS_EPS)
        gain = g_ref[...] * (1.0 + sc_ref[...]) if modulated else g_ref[...]
        gain_ref[...] = jnp.broadcast_to(gain, gain_ref.shape)
        if modulated:
            shift_ref[...] = jnp.broadcast_to(sh_ref[...], shift_ref.shape)

        def chunk(i, carry):
            rows = pl.ds(pl.multiple_of(i * NORM_ROWS, NORM_ROWS), NORM_ROWS)
            r = rinv_ref[rows, :]
            for c in range(d // LANES):
                cols = slice(c * LANES, (c + 1) * LANES)
                y = stash[rows, cols] * r * gain_ref[:, cols]
                if modulated:
                    y = y + shift_ref[:, cols]
                n_ref[rows, cols] = y.astype(n_ref.dtype)
            return carry

        lax.fori_loop(0, tm // NORM_ROWS, chunk, 0)


def _tiles(m, n):
    return min(MM_TILE, m), min(MM_TILE, n)


class _Weight(NamedTuple):
    stack: jax.Array
    layer: int


def _mm_with_jobs_kernel(*refs, kern, n_in, jobs, n_steps):
    n_jobs = len(jobs)
    ins, srcs = refs[:n_in], refs[n_in:n_in + n_jobs]
    o_ref, dsts = refs[n_in + n_jobs], refs[n_in + n_jobs + 1:n_in + 2 * n_jobs + 1]
    scratch = refs[n_in + 2 * n_jobs + 1:]
    step = pl.program_id(0) * pl.num_programs(1) + pl.program_id(1)
    bufs = [scratch[2 * j:2 * j + 2] for j in range(n_jobs)]
    _cast_jobs_step(jobs, srcs, dsts, bufs, scratch[-1], step, n_steps)
    kern(*ins, o_ref)


def _matmul_steps(m, n_out):
    tm, tn = _tiles(m, n_out)
    return (m // tm) * (n_out // tn)


def _matmul(kern, a, w, w_col0, n_out, out_dtype, name, extra_in=(), extra_specs=(), tn_max=None,
            jobs=()):
    m, k = a.shape
    tm, tn = _tiles(m, n_out)
    if tn_max is not None:
        tn = min(tn, tn_max)
    nb0 = w_col0 // tn
    grid = (m // tm, n_out // tn)
    in_specs = [
        pl.BlockSpec((tm, k), lambda i, j: (i, 0)),
        pl.BlockSpec((None, k, tn), lambda i, j: (w.layer, 0, nb0 + j)),
        *[spec(tm, tn) for spec in extra_specs],
    ]
    out_spec = pl.BlockSpec((tm, tn), lambda i, j: (i, j))
    out_shape = jax.ShapeDtypeStruct((m, n_out), out_dtype)
    if not jobs:
        return pl.pallas_call(
            kern, grid=grid, in_specs=in_specs, out_specs=out_spec, out_shape=out_shape,
            compiler_params=_cparams("parallel", "parallel"), name=name,
        )(a, w.stack, *extra_in)
    n_steps = grid[0] * grid[1]
    any_spec = pl.BlockSpec(memory_space=pl.ANY)
    out, *converted = pl.pallas_call(
        functools.partial(_mm_with_jobs_kernel, kern=kern, n_in=len(in_specs),
                          jobs=tuple(job._replace(src=None) for job in jobs), n_steps=n_steps),
        grid=grid,
        in_specs=in_specs + [any_spec] * len(jobs),
        out_specs=[out_spec] + [any_spec] * len(jobs),
        out_shape=[out_shape] + _job_out_shapes(jobs),
        scratch_shapes=_job_scratch(jobs, n_steps),
        compiler_params=_cparams("arbitrary", "arbitrary"),
        name=name + "_cast",
    )(a, w.stack, *extra_in, *[job.src for job in jobs])
    return out, converted


def _cache_tile_copies(stage, zeros, sem, cache_hbm, slot, layer, fill_layers, batch0, head0):
    n_batch, _, tn = stage.shape[1:]
    copies = []
    for hh in range(tn // HEAD_DIM):
        dst = lambda lyr: cache_hbm.at[pl.ds(batch0, n_batch), lyr, :, head0 + hh, :]
        copies.append(pltpu.make_async_copy(stage.at[slot, :, :, pl.ds(hh * HEAD_DIM, HEAD_DIM)],
                                            dst(layer), sem.at[slot, 0, hh]))
        for f, lyr in enumerate(fill_layers):
            copies.append(pltpu.make_async_copy(zeros, dst(lyr), sem.at[slot, 1 + f, hh]))
    return copies


def _mm_cache_kernel(*refs, layer, fill_layers, n_steps):
    a_ref, w_ref = refs[:2]
    lo_ref, cache_hbm, stage, zeros, sem = refs[-5:]
    n_cols = pl.num_programs(1)
    step = pl.program_id(0) * n_cols + pl.program_id(1)
    slot = step % 2
    n_batch, _, tn = stage.shape[1:]

    def copies(s, which):
        return _cache_tile_copies(stage, zeros, sem, cache_hbm, which, layer, fill_layers,
                                  (s // n_cols) * n_batch, (s % n_cols) * (tn // HEAD_DIM))

    if fill_layers:
        @pl.when(step == 0)
        def _():
            zeros[...] = jnp.zeros(zeros.shape, zeros.dtype)

    @pl.when(step >= 2)
    def _():
        for cp in copies(step - 2, slot):
            cp.wait()

    acc = _dot(a_ref, w_ref)
    lo_ref[...] = acc.astype(lo_ref.dtype)
    stage[slot] = acc.reshape(stage.shape[1:])
    for cp in copies(step, slot):
        cp.start()

    @pl.when(step == n_steps - 1)
    def _():
        if n_steps > 1:
            for cp in copies(step - 1, 1 - slot):
                cp.wait()
        for cp in copies(step, slot):
            cp.wait()


def _matmul_to_cache(a, w, w_col0, cache, cache_shape, layer, name):
    m, k = a.shape
    _, n_layers, seq_len, n_heads, _ = cache_shape
    d = n_heads * HEAD_DIM
    tm, tn = _tiles(m, d)
    assert tm % seq_len == 0 and tn % HEAD_DIM == 0
    grid = (m // tm, d // tn)
    nb0 = w_col0 // tn
    any_spec = pl.BlockSpec(memory_space=pl.ANY)
    fresh = cache is None
    fill_layers = tuple(l for l in range(n_layers) if l != layer) if fresh else ()
    n_batch = tm // seq_len
    return pl.pallas_call(
        functools.partial(_mm_cache_kernel, layer=layer, fill_layers=fill_layers,
                          n_steps=grid[0] * grid[1]),
        grid=grid,
        in_specs=[pl.BlockSpec((tm, k), lambda i, j: (i, 0)),
                  pl.BlockSpec((None, k, tn), lambda i, j: (w.layer, 0, nb0 + j))]
                 + ([] if fresh else [any_spec]),
        out_specs=[pl.BlockSpec((tm, tn), lambda i, j: (i, j)), any_spec],
        out_shape=[jax.ShapeDtypeStruct((m, d), BF16), jax.ShapeDtypeStruct(cache_shape, F32)],
        scratch_shapes=[pltpu.VMEM((2, n_batch, seq_len, tn), F32),
                        pltpu.VMEM((n_batch, seq_len, HEAD_DIM), F32),
                        pltpu.SemaphoreType.DMA((2, 1 + len(fill_layers), tn // HEAD_DIM))],
        input_output_aliases={} if fresh else {2: 1},
        compiler_params=_cparams("arbitrary", "arbitrary"),
        name=name,
    )(a, w.stack, *([] if fresh else [cache]))


def _gate_group_matmul(h, w_in, z_col0, d, w_grp, pool_scale_row):
    m, dm = d.shape
    k = h.shape[1]
    _, g, dg, _ = w_grp.stack.shape
    tm = min(MM_TILE, m)
    zb0 = z_col0 // dg
    return pl.pallas_call(
        _mm_gate_group_kernel,
        grid=(m // tm, g),
        in_specs=[
            pl.BlockSpec((tm, k), lambda i, j: (i, 0)),
            pl.BlockSpec((None, k, dg), lambda i, j: (w_in.layer, 0, zb0 + j)),
            pl.BlockSpec((tm, dg), lambda i, j: (i, j)),
            pl.BlockSpec((None, None, dg, dg), lambda i, j: (w_grp.layer, j, 0, 0)),
            pl.BlockSpec((1, dg), lambda i, j: (0, j)),
        ],
        out_specs=pl.BlockSpec((tm, dg), lambda i, j: (i, j)),
        out_shape=jax.ShapeDtypeStruct((m, dm), BF16),
        compiler_params=_cparams("parallel", "parallel"),
        name="pool_gate_group",
    )(h, w_in.stack, d, w_grp.stack, pool_scale_row)


def _out_proj_norm(a, w_out, x2d, mod4, layer, seq_len, row0, per_batch, norm_gain, next_layer):
    m, d = x2d.shape
    k = a.shape[1]
    tm = min(RESID_NORM_ROWS, seq_len if per_batch else m)
    tn = min(MM_TILE, d)
    n_cols = d // tn
    modulated = next_layer is not None

    def mod_row(lyr, part, cols):
        return _mod_row_map(lyr, part, seq_len, tm, row0, per_batch, cols)

    in_specs = [
        pl.BlockSpec((tm, k), lambda i, j: (i, 0)),
        pl.BlockSpec((None, k, tn), lambda i, j: (w_out.layer, 0, j)),
        pl.BlockSpec((tm, tn), lambda i, j: (i, j)),
        pl.BlockSpec((None, None, 1, tn), mod_row(layer, 2, (1, n_cols))),
    ]
    row_block = pl.BlockSpec((tm, d), lambda i, j: (i, 0))
    small = [pltpu.VMEM((tm, LANES), F32), pltpu.VMEM((NORM_ROWS, d), F32)]
    if modulated:
        in_specs += [pl.BlockSpec((None, 1, d), lambda i, j: (next_layer, 0, 0)),
                     pl.BlockSpec((None, None, 1, d), mod_row(next_layer, 0, None)),
                     pl.BlockSpec((None, None, 1, d), mod_row(next_layer, 1, None))]
        operands = (a, w_out.stack, x2d, mod4, norm_gain, mod4, mod4)
        out_specs = [pl.BlockSpec((tm, tn), lambda i, j: (i, j)), row_block]
        out_shape = [jax.ShapeDtypeStruct((m, d), F32), jax.ShapeDtypeStruct((m, d), BF16)]
        scratch = [pltpu.VMEM((tm, d), F32)] + small + [pltpu.VMEM((NORM_ROWS, d), F32)]
    else:
        in_specs += [pl.BlockSpec((1, d), lambda i, j: (0, 0))]
        operands = (a, w_out.stack, x2d, mod4, norm_gain)
        out_specs = row_block
        out_shape = jax.ShapeDtypeStruct((m, d), F32)
        scratch = small
    return pl.pallas_call(
        functools.partial(_mm_resid_norm_kernel, n_cols=n_cols, modulated=modulated),
        grid=(m // tm, n_cols),
        in_specs=in_specs, out_specs=out_specs, out_shape=out_shape, scratch_shapes=scratch,
        compiler_params=_cparams("parallel", "arbitrary"),
        name="out_proj_norm" if modulated else "out_proj_final_norm",
    )(*operands)


def _softmax_exps(scores):
    m = scores[0].max(axis=-1, keepdims=True)
    for s in scores[1:]:
        m = jnp.maximum(m, s.max(axis=-1, keepdims=True))
    return [jnp.exp(s - m).astype(BF16) for s in scores]


def _qk(q, k):
    return lax.dot_general(q, k, (((1,), (1,)), ((), ())), preferred_element_type=F32)


def _with_ones(v):
    return jnp.concatenate([v, jnp.ones(v.shape, v.dtype)], axis=1)


def _normalise_gate(o_ext, z):
    o = o_ext[:, :HEAD_DIM] / o_ext[:, HEAD_DIM:]
    return o * _silu(z.astype(F32))


def _attn_ctx_kernel(q_ref, k_ref, v_ref, z_ref, o_ref, *, heads):
    for hh in range(heads):
        sl = slice(hh * HEAD_DIM, (hh + 1) * HEAD_DIM)
        k = k_ref[:, sl].astype(BF16)
        v = _with_ones(v_ref[:, sl].astype(BF16))
        (e,) = _softmax_exps([_qk(q_ref[:, sl], k)])
        o_ext = jnp.dot(e, v, preferred_element_type=F32)
        o_ref[:, sl] = _normalise_gate(o_ext, z_ref[:, sl]).astype(o_ref.dtype)


def _heads_per_step(n_heads, most=8):
    return min(most, n_heads)


def _attn_context(q, k, v, z, seq_len):
    m, d = q.shape
    hg = _heads_per_step(d // HEAD_DIM, most=CTX_HEADS_PER_STEP)
    wblk = hg * HEAD_DIM
    spec = pl.BlockSpec((seq_len, wblk), lambda b, h: (b, h))
    return pl.pallas_call(
        functools.partial(_attn_ctx_kernel, heads=hg),
        grid=(m // seq_len, d // wblk),
        in_specs=[spec, spec, spec, spec],
        out_specs=spec,
        out_shape=jax.ShapeDtypeStruct((m, d), BF16),
        compiler_params=_cparams("parallel", "parallel"),
        name="attn_context",
    )(q, k, v, z)


def _rel_bias_kernel(rpb_ref, o_ref):
    rows, lanes = o_ref.shape
    lane = lax.broadcasted_iota(jnp.int32, (1, lanes), 1)
    qc = lane // (2 * GRID_W)
    kc = lane % GRID_W
    rel = jnp.clip(kc - qc, -(WIN_W - 1), WIN_W - 1) + (WIN_W - 1)
    start = jnp.clip(qc - WIN_W // 2, 0, GRID_W - WIN_W)
    in_window = (kc >= start) & (kc < start + WIN_W)
    table = rpb_ref[...]
    acc = jnp.zeros((rows, lanes), F32)
    for j in range(N_REL_COLS):
        acc = jnp.where(rel == j, table[:, j:j + 1], acc)
    row = lax.broadcasted_iota(jnp.int32, (rows, 1), 0)
    masked_row = (row % N_REL_ROWS) == (N_REL_ROWS - 1)
    o_ref[...] = jnp.where(in_window & jnp.logical_not(masked_row), acc, NEG_INF)


def _rel_bias_table(rpb_layer):
    h = rpb_layer.shape[0]
    padded = jnp.pad(rpb_layer, ((0, 0), (0, 1), (0, 0))).reshape(h * N_REL_ROWS, N_REL_COLS)
    rows = h * N_REL_ROWS
    tr = min(64, rows)
    lanes = GRID_W * 2 * GRID_W
    flat = pl.pallas_call(
        _rel_bias_kernel,
        grid=(rows // tr,),
        in_specs=[pl.BlockSpec((tr, N_REL_COLS), lambda i: (i, 0))],
        out_specs=pl.BlockSpec((tr, lanes), lambda i: (i, 0)),
        out_shape=jax.ShapeDtypeStruct((rows, lanes), F32),
        compiler_params=_cparams("parallel"),
        name="rel_bias_table",
    )(padded)
    return flat.reshape(h, N_REL_ROWS, GRID_W, 2 * GRID_W)


def _cache_copies(cache_hbm, buf, sem, layer, b, head0, slot, heads):
    return [pltpu.make_async_copy(cache_hbm.at[b, layer, :, head0 + hh, :], buf.at[slot, hh],
                                  sem.at[slot, hh]) for hh in range(heads)]


class _CastJob(NamedTuple):
    src: jax.Array
    row0: int
    rows: int
    col0: int
    cols: int


def _plan_cast(src, row0, rows, col0, cols, n_steps):
    even = rows > 0 and rows % (n_steps * PACKED_ROWS) == 0 and cols % LANES == 0 and col0 % LANES == 0
    return _CastJob(src, row0, rows, col0, cols) if even else None


def _cast_now(src, row0, rows, col0, cols):
    return src[row0:row0 + rows, col0:col0 + cols].astype(BF16)


def _cast_block_kernel(x_ref, o_ref):
    o_ref[...] = x_ref[...].astype(o_ref.dtype)


def _cast_columns(stack, layer, col0, cols):
    _, k, _ = stack.shape
    tr = min(512, k)
    assert col0 % cols == 0
    return pl.pallas_call(
        _cast_block_kernel,
        grid=(k // tr,),
        in_specs=[pl.BlockSpec((None, tr, cols), lambda i: (layer, i, col0 // cols))],
        out_specs=pl.BlockSpec((tr, cols), lambda i: (i, 0)),
        out_shape=jax.ShapeDtypeStruct((k, cols), BF16),
        compiler_params=_cparams("parallel"),
        name="cast_columns",
    )(stack)


def _job_scratch(jobs, n_steps):
    scratch = []
    for job in jobs:
        chunk = (2, job.rows // n_steps, job.cols)
        scratch += [pltpu.VMEM(chunk, F32), pltpu.VMEM(chunk, BF16)]
    if jobs:
        scratch.append(pltpu.SemaphoreType.DMA((len(jobs), 2, 2)))
    return scratch


def _job_out_shapes(jobs):
    return [jax.ShapeDtypeStruct((job.rows, job.cols), BF16) for job in jobs]


def _cast_jobs_step(jobs, srcs, dsts, bufs, sem, step, n_steps):
    slot = step % 2
    per_step = [job.rows // n_steps for job in jobs]

    def load(j, s, which):
        rows = pl.ds(jobs[j].row0 + s * per_step[j], per_step[j])
        cols = pl.ds(jobs[j].col0, jobs[j].cols)
        return pltpu.make_async_copy(srcs[j].at[rows, cols], bufs[j][0].at[which], sem.at[j, 0, which])

    def store(j, s, which):
        rows = pl.ds(s * per_step[j], per_step[j])
        return pltpu.make_async_copy(bufs[j][1].at[which], dsts[j].at[rows, :], sem.at[j, 1, which])

    n_jobs = range(len(jobs))

    @pl.when(step == 0)
    def _():
        for j in n_jobs:
            load(j, step, slot).start()

    @pl.when(step + 1 < n_steps)
    def _():
        for j in n_jobs:
            load(j, step + 1, 1 - slot).start()

    @pl.when(step >= 2)
    def _():
        for j in n_jobs:
            store(j, step - 2, slot).wait()

    for j in n_jobs:
        load(j, step, slot).wait()
        bufs[j][1][slot] = bufs[j][0][slot].astype(BF16)
        store(j, step, slot).start()

    @pl.when(step == n_steps - 1)
    def _():
        for j in n_jobs:
            if n_steps > 1:
                store(j, step - 1, 1 - slot).wait()
            store(j, step, slot).wait()


def _attn_lat_kernel(*refs, heads, grid_rows, layer, jobs, n_steps):
    n_jobs = len(jobs)
    (q_ref, k0_ref, k1_ref, k2_ref, v0_ref, v1_ref, v2_ref, kc_hbm, vc_hbm, z_ref, tab_ref) = refs[:11]
    job_srcs = refs[11:11 + n_jobs]
    o_ref = refs[11 + n_jobs]
    job_dsts = refs[12 + n_jobs:12 + 2 * n_jobs]
    bias_ref, kc_buf, vc_buf, kc_sem, vc_sem = refs[12 + 2 * n_jobs:17 + 2 * n_jobs]
    job_scratch = refs[17 + 2 * n_jobs:]
    hgi, rb, b = pl.program_id(0), pl.program_id(1), pl.program_id(2)
    n_rb, n_b = pl.num_programs(1), pl.num_programs(2)
    step = (hgi * n_rb + rb) * n_b + b
    slot = step % 2

    if n_jobs:
        bufs = [job_scratch[2 * j:2 * j + 2] for j in range(n_jobs)]
        _cast_jobs_step(jobs, job_srcs, job_dsts, bufs, job_scratch[-1], step, n_steps)

    def cache_copies(head_group, batch, which):
        return (_cache_copies(kc_hbm, kc_buf, kc_sem, layer, batch, head_group * heads, which, heads)
                + _cache_copies(vc_hbm, vc_buf, vc_sem, layer, batch, head_group * heads, which, heads))

    @pl.when(step == 0)
    def _():
        for cp in cache_copies(hgi, b, slot):
            cp.start()

    @pl.when(step + 1 < n_steps)
    def _():
        last_b = b == n_b - 1
        next_b = jnp.where(last_b, 0, b + 1)
        next_hgi = jnp.where(last_b & (rb == n_rb - 1), hgi + 1, hgi)
        for cp in cache_copies(next_hgi, next_b, 1 - slot):
            cp.start()

    for cp in cache_copies(hgi, b, slot):
        cp.wait()

    slab0 = Q_ROWS * jnp.clip(rb - 1, 0, grid_rows // Q_ROWS - SLAB_BLOCKS)

    @pl.when(pl.program_id(2) == 0)
    def _():
        left = lax.broadcasted_iota(jnp.int32, (GRID_W, 2 * GRID_W), 1) < GRID_W
        for rq in range(Q_ROWS):
            r = rb * Q_ROWS + rq
            win0 = jnp.clip(r - WIN_H // 2, 0, grid_rows - WIN_H)

            def entry(kk):
                kr = slab0 + kk
                ok = (kr >= win0) & (kr < win0 + WIN_H)
                return jnp.where(ok, kr - r + (WIN_H - 1), N_REL_ROWS - 1)

            for kk in range(0, SLAB_ROWS, 2):
                i_left, i_right = entry(kk), entry(kk + 1)
                for hh in range(heads):
                    tile = jnp.where(left, tab_ref[hh, i_left], tab_ref[hh, i_right])
                    bias_ref[hh, rq * GRID_W:(rq + 1) * GRID_W, kk * GRID_W:(kk + 2) * GRID_W] = tile

    k_refs = (k0_ref, k1_ref, k2_ref)
    v_refs = (v0_ref, v1_ref, v2_ref)

    def attend(pieces):
        for hh in range(heads):
            sl = slice(hh * HEAD_DIM, (hh + 1) * HEAD_DIM)
            q = q_ref[:, sl]
            scores = [_qk(q, k_refs[p][:, sl]) + bias_ref[hh, :, p * Q_BLOCK:(p + 1) * Q_BLOCK]
                      for p in pieces]
            scores.append(_qk(q, kc_buf[slot, hh].astype(BF16)))
            exps = _softmax_exps(scores)
            o_ext = jnp.dot(exps[-1], _with_ones(vc_buf[slot, hh].astype(BF16)),
                            preferred_element_type=F32)
            for e, p in zip(exps, pieces):
                o_ext = o_ext + jnp.dot(e, _with_ones(v_refs[p][:, sl]), preferred_element_type=F32)
            o_ref[:, sl] = _normalise_gate(o_ext, z_ref[:, sl]).astype(o_ref.dtype)

    all_pieces = tuple(range(SLAB_BLOCKS))
    pl.when(rb == 0)(lambda: attend(all_pieces[:-1]))
    pl.when(rb == n_rb - 1)(lambda: attend(all_pieces[1:]))
    pl.when((rb > 0) & (rb < n_rb - 1))(lambda: attend(all_pieces))


def _latent_steps(m, d, seq_len):
    return (d // (_heads_per_step(d // HEAD_DIM) * HEAD_DIM)) * (seq_len // Q_BLOCK) * (m // seq_len)


def _attn_latent(q, k, v, z, k_ctx, v_ctx, layer, bias_tab, seq_len, jobs=()):
    m, d = q.shape
    batch = m // seq_len
    grid_rows = seq_len // GRID_W
    n_rb = grid_rows // Q_ROWS
    assert grid_rows % Q_ROWS == 0 and n_rb >= SLAB_BLOCKS
    past = k_ctx.shape[2]
    hg = _heads_per_step(d // HEAD_DIM)
    wblk = hg * HEAD_DIM

    q_spec = pl.BlockSpec((Q_BLOCK, wblk), lambda h, rb, b: (b * n_rb + rb, h))

    def slab_spec(p):
        return pl.BlockSpec(
            (Q_BLOCK, wblk),
            lambda h, rb, b: (b * n_rb + jnp.clip(rb - 1, 0, n_rb - SLAB_BLOCKS) + p, h))

    any_spec = pl.BlockSpec(memory_space=pl.ANY)
    tab_spec = pl.BlockSpec((hg, N_REL_ROWS, GRID_W, 2 * GRID_W), lambda h, rb, b: (h, 0, 0, 0))
    n_steps = _latent_steps(m, d, seq_len)
    out, *converted = pl.pallas_call(
        functools.partial(_attn_lat_kernel, heads=hg, grid_rows=grid_rows, layer=layer,
                          jobs=tuple(job._replace(src=None) for job in jobs), n_steps=n_steps),
        grid=(d // wblk, n_rb, batch),
        in_specs=[q_spec, slab_spec(0), slab_spec(1), slab_spec(2),
                  slab_spec(0), slab_spec(1), slab_spec(2),
                  any_spec, any_spec, q_spec, tab_spec] + [any_spec] * len(jobs),
        out_specs=[q_spec] + [any_spec] * len(jobs),
        out_shape=[jax.ShapeDtypeStruct((m, d), BF16)]
                  + _job_out_shapes(jobs),
        scratch_shapes=[pltpu.VMEM((hg, Q_BLOCK, SLAB_BLOCKS * Q_BLOCK), F32),
                        pltpu.VMEM((2, hg, past, HEAD_DIM), F32),
                        pltpu.VMEM((2, hg, past, HEAD_DIM), F32),
                        pltpu.SemaphoreType.DMA((2, hg)),
                        pltpu.SemaphoreType.DMA((2, hg))] + _job_scratch(jobs, n_steps),
        compiler_params=_cparams("arbitrary", "arbitrary", "arbitrary"),
        name="attn_latent",
    )(q, k, k, k, v, v, v, k_ctx, v_ctx, z, bias_tab, *[job.src for job in jobs])
    return out, converted


class _CastLedger:
    def __init__(self):
        self.ready = {}
        self.pending = {}

    def want(self, name, src, row0, rows, col0, cols, host, host_steps):
        job = _plan_cast(src, row0, rows, col0, cols, host_steps)
        if job is None:
            self.ready[name] = _cast_now(src, row0, rows, col0, cols)
        else:
            self.pending.setdefault(host, []).append((name, job))

    def jobs(self, host):
        return tuple(job for _, job in self.pending.get(host, ()))

    def done(self, host, converted):
        for (name, _), array in zip(self.pending.pop(host), converted):
            self.ready[name] = array


def _attn_projections(h, weight_of, d, tag, ledger, new_caches=None, cache_shape=None, layer=0):
    def project(part, kern):
        w, col0 = weight_of(part)
        host = f"{part}_{tag}"
        jobs = ledger.jobs(host)
        out = _matmul(kern, h, w, col0, d, BF16, "attn_in_" + part, jobs=jobs)
        if jobs:
            out, converted = out
            ledger.done(host, converted)
        return out

    plain = functools.partial(_mm_cast_kernel, scale=None)
    q = project("q", functools.partial(_mm_cast_kernel, scale=HEAD_DIM ** -0.5))
    if cache_shape is None:
        k, v = project("k", plain), project("v", plain)
    else:
        (wk, ck), (wv, cv) = weight_of("k"), weight_of("v")
        k, new_k = _matmul_to_cache(h, wk, ck, new_caches[0], cache_shape, layer, "attn_in_k_cache")
        v, new_v = _matmul_to_cache(h, wv, cv, new_caches[1], cache_shape, layer, "attn_in_v_cache")
        new_caches = (new_k, new_v)
    z = project("z", plain)
    return q, k, v, z, new_caches


def _pool_branch(h, w_in, w_grp, pool_scale_row, d, seq_len):
    dg = d // len(POOL_WINDOWS)
    pooled = _matmul(functools.partial(_mm_pool_kernel, seq_len=seq_len, cols_per_group=dg),
                     h, w_in, 0, d, BF16, "pool_in_u", tn_max=dg)
    return _gate_group_matmul(h, w_in, d, pooled, w_grp, pool_scale_row)


def kernel(x_prompt, x_sample, c, cache_k, cache_v, c_ctx, norm_g, w_ada, b_ada, w_in_attn, rpb,
           w_out_attn, w_in_pool, w_grp_pool, pool_scale, w_out_pool, final_norm_g):
    bc, tc, d = x_prompt.shape
    bl, tl, _ = x_sample.shape
    depth = norm_g.shape[0]
    n_heads = cache_k.shape[3]
    past = cache_k.shape[2]
    n_attn = w_in_attn.shape[0]
    assert n_heads * HEAD_DIM == d and tl % (GRID_W * Q_ROWS) == 0

    n_cond = -(-(1 + bl) // 8) * 8
    cond = jnp.concatenate([c_ctx[None, :], c, jnp.zeros((n_cond - 1 - bl, d), F32)], axis=0)
    mod = _ada_modulation(cond, w_ada, b_ada)
    mod4 = mod.reshape(depth, n_cond, 1, 3 * d)

    xc = x_prompt.reshape(bc * tc, d)
    xl = x_sample.reshape(bl * tl, d)
    g_rows = norm_g.reshape(depth, 1, d)

    n_pool = w_in_pool.shape[0]
    flat = lambda w: w.reshape(-1, w.shape[-1])
    steps_c, steps_l = _matmul_steps(bc * tc, d), _matmul_steps(bl * tl, d)
    ledger = _CastLedger()
    ledger.ready["attn0_q"] = _cast_columns(w_in_attn, 0, 0, d)
    ledger.ready["attn0_z"] = _cast_columns(w_in_attn, 0, 3 * d, d)
    ledger.want("attn0_k", flat(w_in_attn), 0, d, d, d, "q_c", steps_c)
    ledger.want("attn0_v", flat(w_in_attn), 0, d, 2 * d, d, "q_c", steps_c)
    ledger.want("out_attn", flat(w_out_attn), 0, n_attn * d, 0, d, "z_c", steps_c)
    pool_a = (n_pool // 2) * d if n_pool > 1 else n_pool * d
    ledger.want("in_pool_a", flat(w_in_pool), 0, pool_a, 0, 2 * d, "q_l", steps_l)
    ledger.want("in_pool_b", flat(w_in_pool), pool_a, n_pool * d - pool_a, 0, 2 * d, "k_l", steps_l)
    ledger.want("attn_rest_a", flat(w_in_attn), d, (n_attn - 1) * d, 0, 2 * d, "v_l", steps_l)
    ledger.want("attn_rest_b", flat(w_in_attn), d, (n_attn - 1) * d, 2 * d, 2 * d, "z_l", steps_l)
    lat_steps = _latent_steps(bl * tl, d, tl)
    ledger.want("out_pool", flat(w_out_pool), 0, n_pool * d, 0, d, "lat", lat_steps)
    ledger.want("grp_pool", flat(w_grp_pool), 0, flat(w_grp_pool).shape[0], 0, w_grp_pool.shape[-1],
                "lat", lat_steps)

    def attn_weight_of(j):
        def weight_of(part):
            if j == 0:
                return _Weight(ledger.ready["attn0_" + part][None], 0), 0
            half = ledger.ready["attn_rest_a" if part in "qk" else "attn_rest_b"]
            return _Weight(half.reshape(n_attn - 1, d, 2 * d), j - 1), (0 if part in "qv" else d)
        return weight_of

    def pool_weight(j):
        if j * d < pool_a:
            return _Weight(ledger.ready["in_pool_a"].reshape(-1, d, 2 * d), j)
        return _Weight(ledger.ready["in_pool_b"].reshape(-1, d, 2 * d), j - pool_a // d)

    cache_shape = (bc, n_attn, tc, n_heads, HEAD_DIM)
    new_caches = (None, None)
    hc = _modulate(xc, g_rows, mod4, 0, tc, 0, False)
    hl = _modulate(xl, g_rows, mod4, 0, tl, 1, True)
    for i in range(depth):
        j = i // 2
        if i % 2 == 0:
            weight_of = attn_weight_of(j)
            qc, kc, vc, zc, new_caches = _attn_projections(hc, weight_of, d, "c", ledger, new_caches,
                                                           cache_shape, j)
            oc = _attn_context(qc, kc, vc, zc, tc)
            ql, kl, vl, zl, _ = _attn_projections(hl, weight_of, d, "l", ledger)
            bias_tab = _rel_bias_table(rpb[j])
            jobs = ledger.jobs("lat")
            ol, converted = _attn_latent(ql, kl, vl, zl, cache_k, cache_v, j, bias_tab, tl, jobs=jobs)
            if jobs:
                ledger.done("lat", converted)
            w_out = _Weight(ledger.ready["out_attn"].reshape(n_attn, d, d), j)
        else:
            w_in = pool_weight(j)
            w_grp = _Weight(ledger.ready["grp_pool"].reshape(w_grp_pool.shape), j)
            ps_row = pool_scale[j].reshape(1, d)
            oc = _pool_branch(hc, w_in, w_grp, ps_row, d, tc)
            ol = _pool_branch(hl, w_in, w_grp, ps_row, d, tl)
            w_out = _Weight(ledger.ready["out_pool"].reshape(n_pool, d, d), j)
        if i + 1 < depth:
            xc, hc = _out_proj_norm(oc, w_out, xc, mod4, i, tc, 0, False, g_rows, i + 1)
            xl, hl = _out_proj_norm(ol, w_out, xl, mod4, i, tl, 1, True, g_rows, i + 1)
        else:
            final_gain = final_norm_g.reshape(1, d)
            xc = _out_proj_norm(oc, w_out, xc, mod4, i, tc, 0, False, final_gain, None)
            xl = _out_proj_norm(ol, w_out, xl, mod4, i, tl, 1, True, final_gain, None)

    return (xc.reshape(bc, tc, d), xl.reshape(bl, tl, d), new_caches[0], new_caches[1])
```

```python
import functools
from typing import NamedTuple

import jax
import jax.numpy as jnp
from jax import lax
from jax.experimental import pallas as pl
from jax.experimental.pallas import tpu as pltpu

F32 = jnp.float32
BF16 = jnp.bfloat16

LANES = 128
HEAD_DIM = 128
GRID_W = 64
WIN_H = 8
WIN_W = 16
POOL_WINDOWS = (2, 4, 8, 16)
RMS_EPS = 1e-6
NEG_INF = -1e30

Q_ROWS = 4
Q_BLOCK = Q_ROWS * GRID_W
SLAB_BLOCKS = 3
SLAB_ROWS = SLAB_BLOCKS * Q_ROWS
N_REL_ROWS = 2 * WIN_H
N_REL_COLS = 2 * WIN_W - 1
POOL_PAD = 8

VMEM_LIMIT = 58 * 1024 * 1024
MM_TILE = 1024
CTX_HEADS_PER_STEP = 32
PACKED_ROWS = 16
NORM_ROWS = PACKED_ROWS
NORM_BLOCK_ROWS = 512
RESID_NORM_ROWS = 512


def _cparams(*sem):
    return pltpu.CompilerParams(dimension_semantics=sem, vmem_limit_bytes=VMEM_LIMIT)


def _silu(x):
    return x / (1.0 + jnp.exp(-x))


def _ada_kernel(cond_ref, w_ref, b_ref, o_ref):
    s = _silu(cond_ref[...]).astype(BF16)
    o_ref[...] = jnp.dot(s, w_ref[...].astype(BF16), preferred_element_type=F32) + b_ref[...]


def _ada_modulation(cond, w_ada, b_ada):
    depth, d, d3 = w_ada.shape
    r = cond.shape[0]
    tn = min(512, d3)
    return pl.pallas_call(
        _ada_kernel,
        grid=(depth, d3 // tn),
        in_specs=[
            pl.BlockSpec((r, d), lambda i, n: (0, 0)),
            pl.BlockSpec((None, d, tn), lambda i, n: (i, 0, n)),
            pl.BlockSpec((None, 1, tn), lambda i, n: (i, 0, n)),
        ],
        out_specs=pl.BlockSpec((None, r, tn), lambda i, n: (i, 0, n)),
        out_shape=jax.ShapeDtypeStruct((depth, r, d3), F32),
        compiler_params=_cparams("parallel", "parallel"),
        name="ada_modulation",
    )(cond, w_ada, b_ada.reshape(depth, 1, d3))


def _modulate_kernel(x_ref, g_ref, sh_ref, sc_ref, o_ref, rinv_ref, gain_ref, shift_ref):
    x = x_ref[...]
    rinv = lax.rsqrt(jnp.mean(x * x, axis=-1, keepdims=True) + RMS_EPS)
    rinv_ref[...] = jnp.broadcast_to(rinv, rinv_ref.shape)
    gain_ref[...] = jnp.broadcast_to(g_ref[...] * (1.0 + sc_ref[...]), gain_ref.shape)
    shift_ref[...] = jnp.broadcast_to(sh_ref[...], shift_ref.shape)

    def chunk(i, carry):
        rows = pl.ds(pl.multiple_of(i * NORM_ROWS, NORM_ROWS), NORM_ROWS)
        r = rinv_ref[rows, :]
        for c in range(x_ref.shape[1] // LANES):
            cols = slice(c * LANES, (c + 1) * LANES)
            y = x_ref[rows, cols] * r * gain_ref[:, cols] + shift_ref[:, cols]
            o_ref[rows, cols] = y.astype(BF16)
        return carry

    lax.fori_loop(0, x_ref.shape[0] // NORM_ROWS, chunk, 0)


def _mod_row_map(layer, part, seq_len, rows_per_step, row0, per_batch, ncol_axis):
    def index(*ids):
        m = ids[0]
        row = row0 + (m * rows_per_step) // seq_len if per_batch else row0
        col = part if ncol_axis is None else part * ncol_axis[1] + ids[ncol_axis[0]]
        return (layer, row, 0, col)
    return index


def _modulate(x2d, g_row, mod4, layer, seq_len, row0, per_batch):
    m, d = x2d.shape
    tt = min(NORM_BLOCK_ROWS, seq_len if per_batch else m)
    return pl.pallas_call(
        _modulate_kernel,
        grid=(m // tt,),
        in_specs=[
            pl.BlockSpec((tt, d), lambda i: (i, 0)),
            pl.BlockSpec((None, 1, d), lambda i: (layer, 0, 0)),
            pl.BlockSpec((None, None, 1, d), _mod_row_map(layer, 0, seq_len, tt, row0, per_batch, None)),
            pl.BlockSpec((None, None, 1, d), _mod_row_map(layer, 1, seq_len, tt, row0, per_batch, None)),
        ],
        out_specs=pl.BlockSpec((tt, d), lambda i: (i, 0)),
        out_shape=jax.ShapeDtypeStruct((m, d), BF16),
        scratch_shapes=[pltpu.VMEM((tt, LANES), F32), pltpu.VMEM((NORM_ROWS, d), F32),
                        pltpu.VMEM((NORM_ROWS, d), F32)],
        compiler_params=_cparams("parallel"),
        name="modulate",
    )(x2d, g_row, mod4, mod4)


def _dot(a_ref, w_ref):
    return jnp.dot(a_ref[...], w_ref[...], preferred_element_type=F32)


def _mm_cast_kernel(a_ref, w_ref, o_ref, *, scale):
    acc = _dot(a_ref, w_ref)
    if scale is not None:
        acc = acc * scale
    o_ref[...] = acc.astype(o_ref.dtype)


def _window_sum(u, window):
    t, n = u.shape
    zeros = jnp.zeros((POOL_PAD, n), F32)
    up = jnp.concatenate([zeros, u, zeros], axis=0)
    size = t + 2 * POOL_PAD

    def ahead(x, k):
        return pltpu.roll(x, (-k) % size, axis=0)

    fwd, span = up, 1
    while 2 * span < window:
        fwd = fwd + ahead(fwd, span)
        span *= 2
    s = fwd + ahead(fwd, -span)
    return s[POOL_PAD:POOL_PAD + t, :]


def _mm_pool_kernel(a_ref, w_ref, o_ref, *, seq_len, cols_per_group):
    tm, tn = o_ref.shape
    group = (pl.program_id(1) * tn) // cols_per_group
    t = lax.broadcasted_iota(jnp.int32, (seq_len, 1), 0)
    for gi, window in enumerate(POOL_WINDOWS):
        @pl.when(group == gi)
        def _(window=window):
            acc = _dot(a_ref, w_ref)
            half = window // 2
            count = jnp.minimum(t + half, seq_len) - jnp.maximum(t - half, 0)
            inv_count = 1.0 / count.astype(F32)
            for s in range(tm // seq_len):
                u = acc[s * seq_len:(s + 1) * seq_len, :]
                d = _window_sum(u, window) * inv_count - u
                o_ref[s * seq_len:(s + 1) * seq_len, :] = d.astype(o_ref.dtype)


def _mm_gate_group_kernel(h_ref, wz_ref, d_ref, wg_ref, ps_ref, o_ref):
    z = _dot(h_ref, wz_ref)
    y = _dot(d_ref, wg_ref)
    o_ref[...] = (y * ps_ref[...] * _silu(z)).astype(o_ref.dtype)


def _mm_resid_norm_kernel(*refs, n_cols, modulated):
    if modulated:
        a_ref, w_ref, x_ref, gate_ref, g_ref, sh_ref, sc_ref = refs[:7]
        xo_ref, n_ref, stash, rinv_ref, gain_ref, shift_ref = refs[7:]
    else:
        a_ref, w_ref, x_ref, gate_ref, g_ref = refs[:5]
        n_ref, rinv_ref, gain_ref = refs[5:]
        stash = n_ref
    j = pl.program_id(1)
    tm, tn = x_ref.shape
    d = n_cols * tn
    xn = x_ref[...] + gate_ref[...] * _dot(a_ref, w_ref)
    if modulated:
        xo_ref[...] = xn
    sumsq = jnp.broadcast_to(jnp.sum(xn * xn, axis=-1, keepdims=True), rinv_ref.shape)
    stash[:, pl.ds(pl.multiple_of(j * tn, tn), tn)] = xn

    @pl.when(j == 0)
    def _():
        rinv_ref[...] = sumsq

    @pl.when(j > 0)
    def _():
        rinv_ref[...] += sumsq

    @pl.when(j == n_cols - 1)
    def _():
        rinv_ref[...] = lax.rsqrt(rinv_ref[...] * (1.0 / d) + RMS_EPS)
        gain = g_ref[...] * (1.0 + sc_ref[...]) if modulated else g_ref[...]
        gain_ref[...] = jnp.broadcast_to(gain, gain_ref.shape)
        if modulated:
            shift_ref[...] = jnp.broadcast_to(sh_ref[...], shift_ref.shape)

        def chunk(i, carry):
            rows = pl.ds(pl.multiple_of(i * NORM_ROWS, NORM_ROWS), NORM_ROWS)
            r = rinv_ref[rows, :]
            for c in range(d // LANES):
                cols = slice(c * LANES, (c + 1) * LANES)
                y = stash[rows, cols] * r * gain_ref[:, cols]
                if modulated:
                    y = y + shift_ref[:, cols]
                n_ref[rows, cols] = y.astype(n_ref.dtype)
            return carry

        lax.fori_loop(0, tm // NORM_ROWS, chunk, 0)


def _tiles(m, n):
    return min(MM_TILE, m), min(MM_TILE, n)


class _Weight(NamedTuple):
    stack: jax.Array
    layer: int


def _mm_with_jobs_kernel(*refs, kern, n_in, jobs, n_steps):
    n_jobs = len(jobs)
    ins, srcs = refs[:n_in], refs[n_in:n_in + n_jobs]
    o_ref, dsts = refs[n_in + n_jobs], refs[n_in + n_jobs + 1:n_in + 2 * n_jobs + 1]
    scratch = refs[n_in + 2 * n_jobs + 1:]
    step = pl.program_id(0) * pl.num_programs(1) + pl.program_id(1)
    bufs = [scratch[2 * j:2 * j + 2] for j in range(n_jobs)]
    _cast_jobs_step(jobs, srcs, dsts, bufs, scratch[-1], step, n_steps)
    kern(*ins, o_ref)


def _matmul_steps(m, n_out):
    tm, tn = _tiles(m, n_out)
    return (m // tm) * (n_out // tn)


def _matmul(kern, a, w, w_col0, n_out, out_dtype, name, tn_max=None, jobs=()):
    m, k = a.shape
    tm, tn = _tiles(m, n_out)
    if tn_max is not None:
        tn = min(tn, tn_max)
    nb0 = w_col0 // tn
    grid = (m // tm, n_out // tn)
    in_specs = [
        pl.BlockSpec((tm, k), lambda i, j: (i, 0)),
        pl.BlockSpec((None, k, tn), lambda i, j: (w.layer, 0, nb0 + j)),
    ]
    out_spec = pl.BlockSpec((tm, tn), lambda i, j: (i, j))
    out_shape = jax.ShapeDtypeStruct((m, n_out), out_dtype)
    if not jobs:
        return pl.pallas_call(
            kern, grid=grid, in_specs=in_specs, out_specs=out_spec, out_shape=out_shape,
            compiler_params=_cparams("parallel", "parallel"), name=name,
        )(a, w.stack)
    n_steps = grid[0] * grid[1]
    any_spec = pl.BlockSpec(memory_space=pl.ANY)
    out, *converted = pl.pallas_call(
        functools.partial(_mm_with_jobs_kernel, kern=kern, n_in=len(in_specs),
                          jobs=tuple(job._replace(src=None) for job in jobs), n_steps=n_steps),
        grid=grid,
        in_specs=in_specs + [any_spec] * len(jobs),
        out_specs=[out_spec] + [any_spec] * len(jobs),
        out_shape=[out_shape] + _job_out_shapes(jobs),
        scratch_shapes=_job_scratch(jobs, n_steps),
        compiler_params=_cparams("arbitrary", "arbitrary"),
        name=name + "_cast",
    )(a, w.stack, *[job.src for job in jobs])
    return out, converted


def _cache_tile_copies(stage, zeros, sem, cache_hbm, slot, layer, fill_layers, batch0, head0):
    n_batch, _, tn = stage.shape[1:]
    copies = []
    for hh in range(tn // HEAD_DIM):
        dst = lambda lyr: cache_hbm.at[pl.ds(batch0, n_batch), lyr, :, head0 + hh, :]
        copies.append(pltpu.make_async_copy(stage.at[slot, :, :, pl.ds(hh * HEAD_DIM, HEAD_DIM)],
                                            dst(layer), sem.at[slot, 0, hh]))
        for f, lyr in enumerate(fill_layers):
            copies.append(pltpu.make_async_copy(zeros, dst(lyr), sem.at[slot, 1 + f, hh]))
    return copies


def _mm_cache_kernel(*refs, layer, fill_layers, n_steps):
    a_ref, w_ref = refs[:2]
    lo_ref, cache_hbm, stage, zeros, sem = refs[-5:]
    n_cols = pl.num_programs(1)
    step = pl.program_id(0) * n_cols + pl.program_id(1)
    slot = step % 2
    n_batch, _, tn = stage.shape[1:]

    def copies(s, which):
        return _cache_tile_copies(stage, zeros, sem, cache_hbm, which, layer, fill_layers,
                                  (s // n_cols) * n_batch, (s % n_cols) * (tn // HEAD_DIM))

    if fill_layers:
        @pl.when(step == 0)
        def _():
            zeros[...] = jnp.zeros(zeros.shape, zeros.dtype)

    @pl.when(step >= 2)
    def _():
        for cp in copies(step - 2, slot):
            cp.wait()

    acc = _dot(a_ref, w_ref)
    lo_ref[...] = acc.astype(lo_ref.dtype)
    stage[slot] = acc.reshape(stage.shape[1:])
    for cp in copies(step, slot):
        cp.start()

    @pl.when(step == n_steps - 1)
    def _():
        if n_steps > 1:
            for cp in copies(step - 1, 1 - slot):
                cp.wait()
        for cp in copies(step, slot):
            cp.wait()


def _matmul_to_cache(a, w, w_col0, cache, cache_shape, layer, name):
    m, k = a.shape
    _, n_layers, seq_len, n_heads, _ = cache_shape
    d = n_heads * HEAD_DIM
    tm, tn = _tiles(m, d)
    assert tm % seq_len == 0 and tn % HEAD_DIM == 0
    grid = (m // tm, d // tn)
    nb0 = w_col0 // tn
    any_spec = pl.BlockSpec(memory_space=pl.ANY)
    fresh = cache is None
    fill_layers = tuple(l for l in range(n_layers) if l != layer) if fresh else ()
    n_batch = tm // seq_len
    return pl.pallas_call(
        functools.partial(_mm_cache_kernel, layer=layer, fill_layers=fill_layers,
                          n_steps=grid[0] * grid[1]),
        grid=grid,
        in_specs=[pl.BlockSpec((tm, k), lambda i, j: (i, 0)),
                  pl.BlockSpec((None, k, tn), lambda i, j: (w.layer, 0, nb0 + j))]
                 + ([] if fresh else [any_spec]),
        out_specs=[pl.BlockSpec((tm, tn), lambda i, j: (i, j)), any_spec],
        out_shape=[jax.ShapeDtypeStruct((m, d), BF16), jax.ShapeDtypeStruct(cache_shape, F32)],
        scratch_shapes=[pltpu.VMEM((2, n_batch, seq_len, tn), F32),
                        pltpu.VMEM((n_batch, seq_len, HEAD_DIM), F32),
                        pltpu.SemaphoreType.DMA((2, 1 + len(fill_layers), tn // HEAD_DIM))],
        input_output_aliases={} if fresh else {2: 1},
        compiler_params=_cparams("arbitrary", "arbitrary"),
        name=name,
    )(a, w.stack, *([] if fresh else [cache]))


def _gate_group_matmul(h, w_in, z_col0, d, w_grp, pool_scale_row):
    m, dm = d.shape
    k = h.shape[1]
    _, g, dg, _ = w_grp.stack.shape
    tm = min(MM_TILE, m)
    zb0 = z_col0 // dg
    return pl.pallas_call(
        _mm_gate_group_kernel,
        grid=(m // tm, g),
        in_specs=[
            pl.BlockSpec((tm, k), lambda i, j: (i, 0)),
            pl.BlockSpec((None, k, dg), lambda i, j: (w_in.layer, 0, zb0 + j)),
            pl.BlockSpec((tm, dg), lambda i, j: (i, j)),
            pl.BlockSpec((None, None, dg, dg), lambda i, j: (w_grp.layer, j, 0, 0)),
            pl.BlockSpec((1, dg), lambda i, j: (0, j)),
        ],
        out_specs=pl.BlockSpec((tm, dg), lambda i, j: (i, j)),
        out_shape=jax.ShapeDtypeStruct((m, dm), BF16),
        compiler_params=_cparams("parallel", "parallel"),
        name="pool_gate_group",
    )(h, w_in.stack, d, w_grp.stack, pool_scale_row)


def _out_proj_norm(a, w_out, x2d, mod4, layer, seq_len, row0, per_batch, norm_gain, next_layer):
    m, d = x2d.shape
    k = a.shape[1]
    tm = min(RESID_NORM_ROWS, seq_len if per_batch else m)
    tn = min(MM_TILE, d)
    n_cols = d // tn
    modulated = next_layer is not None

    def mod_row(lyr, part, cols):
        return _mod_row_map(lyr, part, seq_len, tm, row0, per_batch, cols)

    in_specs = [
        pl.BlockSpec((tm, k), lambda i, j: (i, 0)),
        pl.BlockSpec((None, k, tn), lambda i, j: (w_out.layer, 0, j)),
        pl.BlockSpec((tm, tn), lambda i, j: (i, j)),
        pl.BlockSpec((None, None, 1, tn), mod_row(layer, 2, (1, n_cols))),
    ]
    row_block = pl.BlockSpec((tm, d), lambda i, j: (i, 0))
    small = [pltpu.VMEM((tm, LANES), F32), pltpu.VMEM((NORM_ROWS, d), F32)]
    if modulated:
        in_specs += [pl.BlockSpec((None, 1, d), lambda i, j: (next_layer, 0, 0)),
                     pl.BlockSpec((None, None, 1, d), mod_row(next_layer, 0, None)),
                     pl.BlockSpec((None, None, 1, d), mod_row(next_layer, 1, None))]
        operands = (a, w_out.stack, x2d, mod4, norm_gain, mod4, mod4)
        out_specs = [pl.BlockSpec((tm, tn), lambda i, j: (i, j)), row_block]
        out_shape = [jax.ShapeDtypeStruct((m, d), F32), jax.ShapeDtypeStruct((m, d), BF16)]
        scratch = [pltpu.VMEM((tm, d), F32)] + small + [pltpu.VMEM((NORM_ROWS, d), F32)]
    else:
        in_specs += [pl.BlockSpec((1, d), lambda i, j: (0, 0))]
        operands = (a, w_out.stack, x2d, mod4, norm_gain)
        out_specs = row_block
        out_shape = jax.ShapeDtypeStruct((m, d), F32)
        scratch = small
    return pl.pallas_call(
        functools.partial(_mm_resid_norm_kernel, n_cols=n_cols, modulated=modulated),
        grid=(m // tm, n_cols),
        in_specs=in_specs, out_specs=out_specs, out_shape=out_shape, scratch_shapes=scratch,
        compiler_params=_cparams("parallel", "arbitrary"),
        name="out_proj_norm" if modulated else "out_proj_final_norm",
    )(*operands)


def _softmax_exps(scores):
    m = scores[0].max(axis=-1, keepdims=True)
    for s in scores[1:]:
        m = jnp.maximum(m, s.max(axis=-1, keepdims=True))
    return [jnp.exp(s - m).astype(BF16) for s in scores]


def _qk(q, k):
    return lax.dot_general(q, k, (((1,), (1,)), ((), ())), preferred_element_type=F32)


def _with_ones(v):
    return jnp.concatenate([v, jnp.ones(v.shape, v.dtype)], axis=1)


def _normalise_gate(o_ext, z):
    o = o_ext[:, :HEAD_DIM] / o_ext[:, HEAD_DIM:]
    return o * _silu(z.astype(F32))


def _attn_ctx_kernel(q_ref, k_ref, v_ref, z_ref, o_ref, *, heads):
    for hh in range(heads):
        sl = slice(hh * HEAD_DIM, (hh + 1) * HEAD_DIM)
        k = k_ref[:, sl].astype(BF16)
        v = _with_ones(v_ref[:, sl].astype(BF16))
        (e,) = _softmax_exps([_qk(q_ref[:, sl], k)])
        o_ext = jnp.dot(e, v, preferred_element_type=F32)
        o_ref[:, sl] = _normalise_gate(o_ext, z_ref[:, sl]).astype(o_ref.dtype)


def _heads_per_step(n_heads, most=8):
    return min(most, n_heads)


def _attn_context(q, k, v, z, seq_len):
    m, d = q.shape
    hg = _heads_per_step(d // HEAD_DIM, most=CTX_HEADS_PER_STEP)
    wblk = hg * HEAD_DIM
    spec = pl.BlockSpec((seq_len, wblk), lambda b, h: (b, h))
    return pl.pallas_call(
        functools.partial(_attn_ctx_kernel, heads=hg),
        grid=(m // seq_len, d // wblk),
        in_specs=[spec, spec, spec, spec],
        out_specs=spec,
        out_shape=jax.ShapeDtypeStruct((m, d), BF16),
        compiler_params=_cparams("parallel", "parallel"),
        name="attn_context",
    )(q, k, v, z)


def _rel_bias_kernel(rpb_ref, o_ref):
    rows, lanes = o_ref.shape
    lane = lax.broadcasted_iota(jnp.int32, (1, lanes), 1)
    qc = lane // (2 * GRID_W)
    kc = lane % GRID_W
    rel = jnp.clip(kc - qc, -(WIN_W - 1), WIN_W - 1) + (WIN_W - 1)
    start = jnp.clip(qc - WIN_W // 2, 0, GRID_W - WIN_W)
    in_window = (kc >= start) & (kc < start + WIN_W)
    table = rpb_ref[...]
    acc = jnp.zeros((rows, lanes), F32)
    for j in range(N_REL_COLS):
        acc = jnp.where(rel == j, table[:, j:j + 1], acc)
    row = lax.broadcasted_iota(jnp.int32, (rows, 1), 0)
    masked_row = (row % N_REL_ROWS) == (N_REL_ROWS - 1)
    o_ref[...] = jnp.where(in_window & jnp.logical_not(masked_row), acc, NEG_INF)


def _rel_bias_table(rpb_layer):
    h = rpb_layer.shape[0]
    padded = jnp.pad(rpb_layer, ((0, 0), (0, 1), (0, 0))).reshape(h * N_REL_ROWS, N_REL_COLS)
    rows = h * N_REL_ROWS
    tr = min(64, rows)
    lanes = GRID_W * 2 * GRID_W
    flat = pl.pallas_call(
        _rel_bias_kernel,
        grid=(rows // tr,),
        in_specs=[pl.BlockSpec((tr, N_REL_COLS), lambda i: (i, 0))],
        out_specs=pl.BlockSpec((tr, lanes), lambda i: (i, 0)),
        out_shape=jax.ShapeDtypeStruct((rows, lanes), F32),
        compiler_params=_cparams("parallel"),
        name="rel_bias_table",
    )(padded)
    return flat.reshape(h, N_REL_ROWS, GRID_W, 2 * GRID_W)


def _cache_copies(cache_hbm, buf, sem, layer, b, head0, slot, heads):
    return [pltpu.make_async_copy(cache_hbm.at[b, layer, :, head0 + hh, :], buf.at[slot, hh],
                                  sem.at[slot, hh]) for hh in range(heads)]


class _CastJob(NamedTuple):
    src: jax.Array
    row0: int
    rows: int
    col0: int
    cols: int


def _plan_cast(src, row0, rows, col0, cols, n_steps):
    even = rows > 0 and rows % (n_steps * PACKED_ROWS) == 0 and cols % LANES == 0 and col0 % LANES == 0
    return _CastJob(src, row0, rows, col0, cols) if even else None


def _cast_now(src, row0, rows, col0, cols):
    return src[row0:row0 + rows, col0:col0 + cols].astype(BF16)


def _cast_block_kernel(x_ref, o_ref):
    o_ref[...] = x_ref[...].astype(o_ref.dtype)


def _cast_columns(stack, layer, col0, cols):
    _, k, _ = stack.shape
    tr = min(512, k)
    assert col0 % cols == 0
    return pl.pallas_call(
        _cast_block_kernel,
        grid=(k // tr,),
        in_specs=[pl.BlockSpec((None, tr, cols), lambda i: (layer, i, col0 // cols))],
        out_specs=pl.BlockSpec((tr, cols), lambda i: (i, 0)),
        out_shape=jax.ShapeDtypeStruct((k, cols), BF16),
        compiler_params=_cparams("parallel"),
        name="cast_columns",
    )(stack)


def _job_scratch(jobs, n_steps):
    scratch = []
    for job in jobs:
        chunk = (2, job.rows // n_steps, job.cols)
        scratch += [pltpu.VMEM(chunk, F32), pltpu.VMEM(chunk, BF16)]
    if jobs:
        scratch.append(pltpu.SemaphoreType.DMA((len(jobs), 2, 2)))
    return scratch


def _job_out_shapes(jobs):
    return [jax.ShapeDtypeStruct((job.rows, job.cols), BF16) for job in jobs]


def _cast_jobs_step(jobs, srcs, dsts, bufs, sem, step, n_steps):
    slot = step % 2
    per_step = [job.rows // n_steps for job in jobs]

    def load(j, s, which):
        rows = pl.ds(jobs[j].row0 + s * per_step[j], per_step[j])
        cols = pl.ds(jobs[j].col0, jobs[j].cols)
        return pltpu.make_async_copy(srcs[j].at[rows, cols], bufs[j][0].at[which], sem.at[j, 0, which])

    def store(j, s, which):
        rows = pl.ds(s * per_step[j], per_step[j])
        return pltpu.make_async_copy(bufs[j][1].at[which], dsts[j].at[rows, :], sem.at[j, 1, which])

    n_jobs = range(len(jobs))

    @pl.when(step == 0)
    def _():
        for j in n_jobs:
            load(j, step, slot).start()

    @pl.when(step + 1 < n_steps)
    def _():
        for j in n_jobs:
            load(j, step + 1, 1 - slot).start()

    @pl.when(step >= 2)
    def _():
        for j in n_jobs:
            store(j, step - 2, slot).wait()

    for j in n_jobs:
        load(j, step, slot).wait()
        bufs[j][1][slot] = bufs[j][0][slot].astype(BF16)
        store(j, step, slot).start()

    @pl.when(step == n_steps - 1)
    def _():
        for j in n_jobs:
            if n_steps > 1:
                store(j, step - 1, 1 - slot).wait()
            store(j, step, slot).wait()


def _attn_lat_kernel(*refs, heads, grid_rows, layer, jobs, n_steps):
    n_jobs = len(jobs)
    (q_ref, k0_ref, k1_ref, k2_ref, v0_ref, v1_ref, v2_ref, kc_hbm, vc_hbm, z_ref, tab_ref) = refs[:11]
    job_srcs = refs[11:11 + n_jobs]
    o_ref = refs[11 + n_jobs]
    job_dsts = refs[12 + n_jobs:12 + 2 * n_jobs]
    bias_ref, kc_buf, vc_buf, kc_sem, vc_sem = refs[12 + 2 * n_jobs:17 + 2 * n_jobs]
    job_scratch = refs[17 + 2 * n_jobs:]
    hgi, rb, b = pl.program_id(0), pl.program_id(1), pl.program_id(2)
    n_rb, n_b = pl.num_programs(1), pl.num_programs(2)
    step = (hgi * n_rb + rb) * n_b + b
    slot = step % 2

    if n_jobs:
        bufs = [job_scratch[2 * j:2 * j + 2] for j in range(n_jobs)]
        _cast_jobs_step(jobs, job_srcs, job_dsts, bufs, job_scratch[-1], step, n_steps)

    def cache_copies(head_group, batch, which):
        return (_cache_copies(kc_hbm, kc_buf, kc_sem, layer, batch, head_group * heads, which, heads)
                + _cache_copies(vc_hbm, vc_buf, vc_sem, layer, batch, head_group * heads, which, heads))

    @pl.when(step == 0)
    def _():
        for cp in cache_copies(hgi, b, slot):
            cp.start()

    @pl.when(step + 1 < n_steps)
    def _():
        last_b = b == n_b - 1
        next_b = jnp.where(last_b, 0, b + 1)
        next_hgi = jnp.where(last_b & (rb == n_rb - 1), hgi + 1, hgi)
        for cp in cache_copies(next_hgi, next_b, 1 - slot):
            cp.start()

    for cp in cache_copies(hgi, b, slot):
        cp.wait()

    slab0 = Q_ROWS * jnp.clip(rb - 1, 0, grid_rows // Q_ROWS - SLAB_BLOCKS)

    @pl.when(pl.program_id(2) == 0)
    def _():
        left = lax.broadcasted_iota(jnp.int32, (GRID_W, 2 * GRID_W), 1) < GRID_W
        for rq in range(Q_ROWS):
            r = rb * Q_ROWS + rq
            win0 = jnp.clip(r - WIN_H // 2, 0, grid_rows - WIN_H)

            def entry(kk):
                kr = slab0 + kk
                ok = (kr >= win0) & (kr < win0 + WIN_H)
                return jnp.where(ok, kr - r + (WIN_H - 1), N_REL_ROWS - 1)

            for kk in range(0, SLAB_ROWS, 2):
                i_left, i_right = entry(kk), entry(kk + 1)
                for hh in range(heads):
                    tile = jnp.where(left, tab_ref[hh, i_left], tab_ref[hh, i_right])
                    bias_ref[hh, rq * GRID_W:(rq + 1) * GRID_W, kk * GRID_W:(kk + 2) * GRID_W] = tile

    k_refs = (k0_ref, k1_ref, k2_ref)
    v_refs = (v0_ref, v1_ref, v2_ref)

    def attend(pieces):
        for hh in range(heads):
            sl = slice(hh * HEAD_DIM, (hh + 1) * HEAD_DIM)
            q = q_ref[:, sl]
            scores = [_qk(q, k_refs[p][:, sl]) + bias_ref[hh, :, p * Q_BLOCK:(p + 1) * Q_BLOCK]
                      for p in pieces]
            scores.append(_qk(q, kc_buf[slot, hh].astype(BF16)))
            exps = _softmax_exps(scores)
            o_ext = jnp.dot(exps[-1], _with_ones(vc_buf[slot, hh].astype(BF16)),
                            preferred_element_type=F32)
            for e, p in zip(exps, pieces):
                o_ext = o_ext + jnp.dot(e, _with_ones(v_refs[p][:, sl]), preferred_element_type=F32)
            o_ref[:, sl] = _normalise_gate(o_ext, z_ref[:, sl]).astype(o_ref.dtype)

    all_pieces = tuple(range(SLAB_BLOCKS))
    pl.when(rb == 0)(lambda: attend(all_pieces[:-1]))
    pl.when(rb == n_rb - 1)(lambda: attend(all_pieces[1:]))
    pl.when((rb > 0) & (rb < n_rb - 1))(lambda: attend(all_pieces))


def _latent_steps(m, d, seq_len):
    return (d // (_heads_per_step(d // HEAD_DIM) * HEAD_DIM)) * (seq_len // Q_BLOCK) * (m // seq_len)


def _attn_latent(q, k, v, z, k_ctx, v_ctx, layer, bias_tab, seq_len, jobs=()):
    m, d = q.shape
    batch = m // seq_len
    grid_rows = seq_len // GRID_W
    n_rb = grid_rows // Q_ROWS
    assert grid_rows % Q_ROWS == 0 and n_rb >= SLAB_BLOCKS
    past = k_ctx.shape[2]
    hg = _heads_per_step(d // HEAD_DIM)
    wblk = hg * HEAD_DIM

    q_spec = pl.BlockSpec((Q_BLOCK, wblk), lambda h, rb, b: (b * n_rb + rb, h))

    def slab_spec(p):
        return pl.BlockSpec(
            (Q_BLOCK, wblk),
            lambda h, rb, b: (b * n_rb + jnp.clip(rb - 1, 0, n_rb - SLAB_BLOCKS) + p, h))

    any_spec = pl.BlockSpec(memory_space=pl.ANY)
    tab_spec = pl.BlockSpec((hg, N_REL_ROWS, GRID_W, 2 * GRID_W), lambda h, rb, b: (h, 0, 0, 0))
    n_steps = _latent_steps(m, d, seq_len)
    out, *converted = pl.pallas_call(
        functools.partial(_attn_lat_kernel, heads=hg, grid_rows=grid_rows, layer=layer,
                          jobs=tuple(job._replace(src=None) for job in jobs), n_steps=n_steps),
        grid=(d // wblk, n_rb, batch),
        in_specs=[q_spec, slab_spec(0), slab_spec(1), slab_spec(2),
                  slab_spec(0), slab_spec(1), slab_spec(2),
                  any_spec, any_spec, q_spec, tab_spec] + [any_spec] * len(jobs),
        out_specs=[q_spec] + [any_spec] * len(jobs),
        out_shape=[jax.ShapeDtypeStruct((m, d), BF16)]
                  + _job_out_shapes(jobs),
        scratch_shapes=[pltpu.VMEM((hg, Q_BLOCK, SLAB_BLOCKS * Q_BLOCK), F32),
                        pltpu.VMEM((2, hg, past, HEAD_DIM), F32),
                        pltpu.VMEM((2, hg, past, HEAD_DIM), F32),
                        pltpu.SemaphoreType.DMA((2, hg)),
                        pltpu.SemaphoreType.DMA((2, hg))] + _job_scratch(jobs, n_steps),
        compiler_params=_cparams("arbitrary", "arbitrary", "arbitrary"),
        name="attn_latent",
    )(q, k, k, k, v, v, v, k_ctx, v_ctx, z, bias_tab, *[job.src for job in jobs])
    return out, converted


class _CastLedger:
    def __init__(self):
        self.ready = {}
        self.pending = {}

    def want(self, name, src, row0, rows, col0, cols, host, host_steps):
        job = _plan_cast(src, row0, rows, col0, cols, host_steps)
        if job is None:
            self.ready[name] = _cast_now(src, row0, rows, col0, cols)
        else:
            self.pending.setdefault(host, []).append((name, job))

    def jobs(self, host):
        return tuple(job for _, job in self.pending.get(host, ()))

    def done(self, host, converted):
        for (name, _), array in zip(self.pending.pop(host), converted):
            self.ready[name] = array


def _attn_projections(h, weight_of, d, tag, ledger, new_caches=None, cache_shape=None, layer=0):
    def project(part, kern):
        w, col0 = weight_of(part)
        host = f"{part}_{tag}"
        jobs = ledger.jobs(host)
        out = _matmul(kern, h, w, col0, d, BF16, "attn_in_" + part, jobs=jobs)
        if jobs:
            out, converted = out
            ledger.done(host, converted)
        return out

    plain = functools.partial(_mm_cast_kernel, scale=None)
    q = project("q", functools.partial(_mm_cast_kernel, scale=HEAD_DIM ** -0.5))
    if cache_shape is None:
        k, v = project("k", plain), project("v", plain)
    else:
        (wk, ck), (wv, cv) = weight_of("k"), weight_of("v")
        k, new_k = _matmul_to_cache(h, wk, ck, new_caches[0], cache_shape, layer, "attn_in_k_cache")
        v, new_v = _matmul_to_cache(h, wv, cv, new_caches[1], cache_shape, layer, "attn_in_v_cache")
        new_caches = (new_k, new_v)
    z = project("z", plain)
    return q, k, v, z, new_caches


def _pool_branch(h, w_in, w_grp, pool_scale_row, d, seq_len):
    dg = d // len(POOL_WINDOWS)
    pooled = _matmul(functools.partial(_mm_pool_kernel, seq_len=seq_len, cols_per_group=dg),
                     h, w_in, 0, d, BF16, "pool_in_u", tn_max=dg)
    return _gate_group_matmul(h, w_in, d, pooled, w_grp, pool_scale_row)


def kernel(x_prompt, x_sample, c, cache_k, cache_v, c_ctx, norm_g, w_ada, b_ada, w_in_attn, rpb,
           w_out_attn, w_in_pool, w_grp_pool, pool_scale, w_out_pool, final_norm_g):
    bc, tc, d = x_prompt.shape
    bl, tl, _ = x_sample.shape
    depth = norm_g.shape[0]
    n_heads = cache_k.shape[3]
    n_attn = w_in_attn.shape[0]
    assert n_heads * HEAD_DIM == d and tl % (GRID_W * Q_ROWS) == 0

    n_cond = -(-(1 + bl) // 8) * 8
    cond = jnp.concatenate([c_ctx[None, :], c, jnp.zeros((n_cond - 1 - bl, d), F32)], axis=0)
    mod = _ada_modulation(cond, w_ada, b_ada)
    mod4 = mod.reshape(depth, n_cond, 1, 3 * d)

    xc = x_prompt.reshape(bc * tc, d)
    xl = x_sample.reshape(bl * tl, d)
    g_rows = norm_g.reshape(depth, 1, d)

    n_pool = w_in_pool.shape[0]
    flat = lambda w: w.reshape(-1, w.shape[-1])
    steps_c, steps_l = _matmul_steps(bc * tc, d), _matmul_steps(bl * tl, d)
    ledger = _CastLedger()
    ledger.ready["attn0_q"] = _cast_columns(w_in_attn, 0, 0, d)
    ledger.ready["attn0_z"] = _cast_columns(w_in_attn, 0, 3 * d, d)
    ledger.want("attn0_k", flat(w_in_attn), 0, d, d, d, "q_c", steps_c)
    ledger.want("attn0_v", flat(w_in_attn), 0, d, 2 * d, d, "q_c", steps_c)
    ledger.want("out_attn", flat(w_out_attn), 0, n_attn * d, 0, d, "z_c", steps_c)
    pool_a = (n_pool // 2) * d if n_pool > 1 else n_pool * d
    ledger.want("in_pool_a", flat(w_in_pool), 0, pool_a, 0, 2 * d, "q_l", steps_l)
    ledger.want("in_pool_b", flat(w_in_pool), pool_a, n_pool * d - pool_a, 0, 2 * d, "k_l", steps_l)
    ledger.want("attn_rest_a", flat(w_in_attn), d, (n_attn - 1) * d, 0, 2 * d, "v_l", steps_l)
    ledger.want("attn_rest_b", flat(w_in_attn), d, (n_attn - 1) * d, 2 * d, 2 * d, "z_l", steps_l)
    lat_steps = _latent_steps(bl * tl, d, tl)
    ledger.want("out_pool", flat(w_out_pool), 0, n_pool * d, 0, d, "lat", lat_steps)
    ledger.want("grp_pool", flat(w_grp_pool), 0, flat(w_grp_pool).shape[0], 0, w_grp_pool.shape[-1],
                "lat", lat_steps)

    def attn_weight_of(j):
        def weight_of(part):
            if j == 0:
                return _Weight(ledger.ready["attn0_" + part][None], 0), 0
            half = ledger.ready["attn_rest_a" if part in "qk" else "attn_rest_b"]
            return _Weight(half.reshape(n_attn - 1, d, 2 * d), j - 1), (0 if part in "qv" else d)
        return weight_of

    def pool_weight(j):
        if j * d < pool_a:
            return _Weight(ledger.ready["in_pool_a"].reshape(-1, d, 2 * d), j)
        return _Weight(ledger.ready["in_pool_b"].reshape(-1, d, 2 * d), j - pool_a // d)

    cache_shape = (bc, n_attn, tc, n_heads, HEAD_DIM)
    new_caches = (None, None)
    hc = _modulate(xc, g_rows, mod4, 0, tc, 0, False)
    hl = _modulate(xl, g_rows, mod4, 0, tl, 1, True)
    for i in range(depth):
        j = i // 2
        if i % 2 == 0:
            weight_of = attn_weight_of(j)
            qc, kc, vc, zc, new_caches = _attn_projections(hc, weight_of, d, "c", ledger, new_caches,
                                                           cache_shape, j)
            oc = _attn_context(qc, kc, vc, zc, tc)
            ql, kl, vl, zl, _ = _attn_projections(hl, weight_of, d, "l", ledger)
            bias_tab = _rel_bias_table(rpb[j])
            jobs = ledger.jobs("lat")
            ol, converted = _attn_latent(ql, kl, vl, zl, cache_k, cache_v, j, bias_tab, tl, jobs=jobs)
            if jobs:
                ledger.done("lat", converted)
            w_out = _Weight(ledger.ready["out_attn"].reshape(n_attn, d, d), j)
        else:
            w_in = pool_weight(j)
            w_grp = _Weight(ledger.ready["grp_pool"].reshape(w_grp_pool.shape), j)
            ps_row = pool_scale[j].reshape(1, d)
            oc = _pool_branch(hc, w_in, w_grp, ps_row, d, tc)
            ol = _pool_branch(hl, w_in, w_grp, ps_row, d, tl)
            w_out = _Weight(ledger.ready["out_pool"].reshape(n_pool, d, d), j)
        if i + 1 < depth:
            xc, hc = _out_proj_norm(oc, w_out, xc, mod4, i, tc, 0, False, g_rows, i + 1)
            xl, hl = _out_proj_norm(ol, w_out, xl, mod4, i, tl, 1, True, g_rows, i + 1)
        else:
            final_gain = final_norm_g.reshape(1, d)
            xc = _out_proj_norm(oc, w_out, xc, mod4, i, tc, 0, False, final_gain, None)
            xl = _out_proj_norm(ol, w_out, xl, mod4, i, tl, 1, True, final_gain, None)

    return (xc.reshape(bc, tc, d), xl.reshape(bl, tl, d), new_caches[0], new_caches[1])
```

```python
import functools
from typing import NamedTuple

import jax
import jax.numpy as jnp
from jax import lax
from jax.experimental import pallas as pl
from jax.experimental.pallas import tpu as pltpu

F32 = jnp.float32
BF16 = jnp.bfloat16

LANES = 128
HEAD_DIM = 128
GRID_W = 64
WIN_H = 8
WIN_W = 16
POOL_WINDOWS = (2, 4, 8, 16)
RMS_EPS = 1e-6
NEG_INF = -1e30

Q_ROWS = 4
Q_BLOCK = Q_ROWS * GRID_W
SLAB_BLOCKS = 3
SLAB_ROWS = SLAB_BLOCKS * Q_ROWS
N_REL_ROWS = 2 * WIN_H
N_REL_COLS = 2 * WIN_W - 1
POOL_PAD = 8

VMEM_LIMIT = 58 * 1024 * 1024
MM_TILE = 1024
CTX_HEADS_PER_STEP = 32
PACKED_ROWS = 16
NORM_ROWS = PACKED_ROWS
NORM_BLOCK_ROWS = 512
RESID_NORM_ROWS = 512


def _cparams(*sem):
    return pltpu.CompilerParams(dimension_semantics=sem, vmem_limit_bytes=VMEM_LIMIT)


def _silu(x):
    return x / (1.0 + jnp.exp(-x))


def _ada_kernel(cond_ref, w_ref, b_ref, o_ref):
    s = _silu(cond_ref[...]).astype(BF16)
    o_ref[...] = jnp.dot(s, w_ref[...].astype(BF16), preferred_element_type=F32) + b_ref[...]


def _ada_modulation(cond, w_ada, b_ada):
    depth, d, d3 = w_ada.shape
    r = cond.shape[0]
    tn = min(512, d3)
    return pl.pallas_call(
        _ada_kernel,
        grid=(depth, d3 // tn),
        in_specs=[
            pl.BlockSpec((r, d), lambda i, n: (0, 0)),
            pl.BlockSpec((None, d, tn), lambda i, n: (i, 0, n)),
            pl.BlockSpec((None, 1, tn), lambda i, n: (i, 0, n)),
        ],
        out_specs=pl.BlockSpec((None, r, tn), lambda i, n: (i, 0, n)),
        out_shape=jax.ShapeDtypeStruct((depth, r, d3), F32),
        compiler_params=_cparams("parallel", "parallel"),
        name="ada_modulation",
    )(cond, w_ada, b_ada.reshape(depth, 1, d3))


def _modulate_kernel(x_ref, g_ref, sh_ref, sc_ref, o_ref, rinv_ref, gain_ref, shift_ref):
    x = x_ref[...]
    rinv = lax.rsqrt(jnp.mean(x * x, axis=-1, keepdims=True) + RMS_EPS)
    rinv_ref[...] = jnp.broadcast_to(rinv, rinv_ref.shape)
    gain_ref[...] = jnp.broadcast_to(g_ref[...] * (1.0 + sc_ref[...]), gain_ref.shape)
    shift_ref[...] = jnp.broadcast_to(sh_ref[...], shift_ref.shape)

    def chunk(i, carry):
        rows = pl.ds(pl.multiple_of(i * NORM_ROWS, NORM_ROWS), NORM_ROWS)
        r = rinv_ref[rows, :]
        for c in range(x_ref.shape[1] // LANES):
            cols = slice(c * LANES, (c + 1) * LANES)
            y = x_ref[rows, cols] * r * gain_ref[:, cols] + shift_ref[:, cols]
            o_ref[rows, cols] = y.astype(BF16)
        return carry

    lax.fori_loop(0, x_ref.shape[0] // NORM_ROWS, chunk, 0)


def _mod_row_map(layer, part, seq_len, rows_per_step, row0, per_batch, ncol_axis):
    def index(*ids):
        m = ids[0]
        row = row0 + (m * rows_per_step) // seq_len if per_batch else row0
        col = part if ncol_axis is None else part * ncol_axis[1] + ids[ncol_axis[0]]
        return (layer, row, 0, col)
    return index


def _modulate(x2d, g_row, mod4, layer, seq_len, row0, per_batch):
    m, d = x2d.shape
    tt = min(NORM_BLOCK_ROWS, seq_len if per_batch else m)
    return pl.pallas_call(
        _modulate_kernel,
        grid=(m // tt,),
        in_specs=[
            pl.BlockSpec((tt, d), lambda i: (i, 0)),
            pl.BlockSpec((None, 1, d), lambda i: (layer, 0, 0)),
            pl.BlockSpec((None, None, 1, d), _mod_row_map(layer, 0, seq_len, tt, row0, per_batch, None)),
            pl.BlockSpec((None, None, 1, d), _mod_row_map(layer, 1, seq_len, tt, row0, per_batch, None)),
        ],
        out_specs=pl.BlockSpec((tt, d), lambda i: (i, 0)),
        out_shape=jax.ShapeDtypeStruct((m, d), BF16),
        scratch_shapes=[pltpu.VMEM((tt, LANES), F32), pltpu.VMEM((NORM_ROWS, d), F32),
                        pltpu.VMEM((NORM_ROWS, d), F32)],
        compiler_params=_cparams("parallel"),
        name="modulate",
    )(x2d, g_row, mod4, mod4)


def _dot(a_ref, w_ref):
    return jnp.dot(a_ref[...], w_ref[...], preferred_element_type=F32)


def _mm_cast_kernel(a_ref, w_ref, o_ref, *, scale):
    acc = _dot(a_ref, w_ref)
    if scale is not None:
        acc = acc * scale
    o_ref[...] = acc.astype(o_ref.dtype)


def _window_sum(u, window):
    t, n = u.shape
    zeros = jnp.zeros((POOL_PAD, n), F32)
    up = jnp.concatenate([zeros, u, zeros], axis=0)
    size = t + 2 * POOL_PAD

    def ahead(x, k):
        return pltpu.roll(x, (-k) % size, axis=0)

    fwd, span = up, 1
    while 2 * span < window:
        fwd = fwd + ahead(fwd, span)
        span *= 2
    s = fwd + ahead(fwd, -span)
    return s[POOL_PAD:POOL_PAD + t, :]


def _mm_pool_kernel(a_ref, w_ref, o_ref, *, seq_len, cols_per_group):
    tm, tn = o_ref.shape
    group = (pl.program_id(1) * tn) // cols_per_group
    t = lax.broadcasted_iota(jnp.int32, (seq_len, 1), 0)
    for gi, window in enumerate(POOL_WINDOWS):
        @pl.when(group == gi)
        def _(window=window):
            acc = _dot(a_ref, w_ref)
            half = window // 2
            count = jnp.minimum(t + half, seq_len) - jnp.maximum(t - half, 0)
            inv_count = 1.0 / count.astype(F32)
            for s in range(tm // seq_len):
                u = acc[s * seq_len:(s + 1) * seq_len, :]
                d = _window_sum(u, window) * inv_count - u
                o_ref[s * seq_len:(s + 1) * seq_len, :] = d.astype(o_ref.dtype)


def _mm_gate_group_kernel(h_ref, wz_ref, d_ref, wg_ref, ps_ref, o_ref):
    z = _dot(h_ref, wz_ref)
    y = _dot(d_ref, wg_ref)
    o_ref[...] = (y * ps_ref[...] * _silu(z)).astype(o_ref.dtype)


def _mm_resid_norm_kernel(*refs, n_cols, modulated):
    if modulated:
        a_ref, w_ref, x_ref, gate_ref, g_ref, sh_ref, sc_ref = refs[:7]
        xo_ref, n_ref, stash, rinv_ref, gain_ref, shift_ref = refs[7:]
    else:
        a_ref, w_ref, x_ref, gate_ref, g_ref = refs[:5]
        n_ref, rinv_ref, gain_ref = refs[5:]
        stash = n_ref
    j = pl.program_id(1)
    tm, tn = x_ref.shape
    d = n_cols * tn
    xn = x_ref[...] + gate_ref[...] * _dot(a_ref, w_ref)
    if modulated:
        xo_ref[...] = xn
    sumsq = jnp.broadcast_to(jnp.sum(xn * xn, axis=-1, keepdims=True), rinv_ref.shape)
    stash[:, pl.ds(pl.multiple_of(j * tn, tn), tn)] = xn

    @pl.when(j == 0)
    def _():
        rinv_ref[...] = sumsq

    @pl.when(j > 0)
    def _():
        rinv_ref[...] += sumsq

    @pl.when(j == n_cols - 1)
    def _():
        rinv_ref[...] = lax.rsqrt(rinv_ref[...] * (1.0 / d) + RMS_EPS)
        gain = g_ref[...] * (1.0 + sc_ref[...]) if modulated else g_ref[...]
        gain_ref[...] = jnp.broadcast_to(gain, gain_ref.shape)
        if modulated:
            shift_ref[...] = jnp.broadcast_to(sh_ref[...], shift_ref.shape)

        def chunk(i, carry):
            rows = pl.ds(pl.multiple_of(i * NORM_ROWS, NORM_ROWS), NORM_ROWS)
            r = rinv_ref[rows, :]
            for c in range(d // LANES):
                cols = slice(c * LANES, (c + 1) * LANES)
                y = stash[rows, cols] * r * gain_ref[:, cols]
                if modulated:
                    y = y + shift_ref[:, cols]
                n_ref[rows, cols] = y.astype(n_ref.dtype)
            return carry

        lax.fori_loop(0, tm // NORM_ROWS, chunk, 0)


def _tiles(m, n):
    return min(MM_TILE, m), min(MM_TILE, n)


class _Weight(NamedTuple):
    stack: jax.Array
    layer: int


def _mm_with_jobs_kernel(*refs, kern, n_in, jobs, n_steps):
    n_jobs = len(jobs)
    ins, srcs = refs[:n_in], refs[n_in:n_in + n_jobs]
    o_ref, dsts = refs[n_in + n_jobs], refs[n_in + n_jobs + 1:n_in + 2 * n_jobs + 1]
    scratch = refs[n_in + 2 * n_jobs + 1:]
    step = pl.program_id(0) * pl.num_programs(1) + pl.program_id(1)
    bufs = [scratch[2 * j:2 * j + 2] for j in range(n_jobs)]
    _cast_jobs_step(jobs, srcs, dsts, bufs, scratch[-1], step, n_steps)
    kern(*ins, o_ref)


def _matmul_steps(m, n_out):
    tm, tn = _tiles(m, n_out)
    return (m // tm) * (n_out // tn)


def _matmul(kern, a, w, w_col0, n_out, out_dtype, name, tn_max=None, jobs=()):
    m, k = a.shape
    tm, tn = _tiles(m, n_out)
    if tn_max is not None:
        tn = min(tn, tn_max)
    nb0 = w_col0 // tn
    grid = (m // tm, n_out // tn)
    in_specs = [
        pl.BlockSpec((tm, k), lambda i, j: (i, 0)),
        pl.BlockSpec((None, k, tn), lambda i, j: (w.layer, 0, nb0 + j)),
    ]
    out_spec = pl.BlockSpec((tm, tn), lambda i, j: (i, j))
    out_shape = jax.ShapeDtypeStruct((m, n_out), out_dtype)
    if not jobs:
        return pl.pallas_call(
            kern, grid=grid, in_specs=in_specs, out_specs=out_spec, out_shape=out_shape,
            compiler_params=_cparams("parallel", "parallel"), name=name,
        )(a, w.stack)
    n_steps = grid[0] * grid[1]
    any_spec = pl.BlockSpec(memory_space=pl.ANY)
    out, *converted = pl.pallas_call(
        functools.partial(_mm_with_jobs_kernel, kern=kern, n_in=len(in_specs),
                          jobs=tuple(job._replace(src=None) for job in jobs), n_steps=n_steps),
        grid=grid,
        in_specs=in_specs + [any_spec] * len(jobs),
        out_specs=[out_spec] + [any_spec] * len(jobs),
        out_shape=[out_shape] + _job_out_shapes(jobs),
        scratch_shapes=_job_scratch(jobs, n_steps),
        compiler_params=_cparams("arbitrary", "arbitrary"),
        name=name + "_cast",
    )(a, w.stack, *[job.src for job in jobs])
    return out, converted


def _cache_tile_copies(stage, zeros, sem, cache_hbm, slot, layer, fill_layers, batch0, head0):
    n_batch, _, tn = stage.shape[1:]
    copies = []
    for hh in range(tn // HEAD_DIM):
        dst = lambda lyr: cache_hbm.at[pl.ds(batch0, n_batch), lyr, :, head0 + hh, :]
        copies.append(pltpu.make_async_copy(stage.at[slot, :, :, pl.ds(hh * HEAD_DIM, HEAD_DIM)],
                                            dst(layer), sem.at[slot, 0, hh]))
        for f, lyr in enumerate(fill_layers):
            copies.append(pltpu.make_async_copy(zeros, dst(lyr), sem.at[slot, 1 + f, hh]))
    return copies


def _mm_cache_kernel(*refs, layer, fill_layers, n_steps):
    a_ref, w_ref = refs[:2]
    lo_ref, cache_hbm, stage, zeros, sem = refs[-5:]
    n_cols = pl.num_programs(1)
    step = pl.program_id(0) * n_cols + pl.program_id(1)
    slot = step % 2
    n_batch, _, tn = stage.shape[1:]

    def copies(s, which):
        return _cache_tile_copies(stage, zeros, sem, cache_hbm, which, layer, fill_layers,
                                  (s // n_cols) * n_batch, (s % n_cols) * (tn // HEAD_DIM))

    if fill_layers:
        @pl.when(step == 0)
        def _():
            zeros[...] = jnp.zeros(zeros.shape, zeros.dtype)

    @pl.when(step >= 2)
    def _():
        for cp in copies(step - 2, slot):
            cp.wait()

    acc = _dot(a_ref, w_ref)
    lo_ref[...] = acc.astype(lo_ref.dtype)
    stage[slot] = acc.reshape(stage.shape[1:])
    for cp in copies(step, slot):
        cp.start()

    @pl.when(step == n_steps - 1)
    def _():
        if n_steps > 1:
            for cp in copies(step - 1, 1 - slot):
                cp.wait()
        for cp in copies(step, slot):
            cp.wait()


def _matmul_to_cache(a, w, w_col0, cache, cache_shape, layer, name):
    m, k = a.shape
    _, n_layers, seq_len, n_heads, _ = cache_shape
    d = n_heads * HEAD_DIM
    tm, tn = _tiles(m, d)
    assert tm % seq_len == 0 and tn % HEAD_DIM == 0
    grid = (m // tm, d // tn)
    nb0 = w_col0 // tn
    any_spec = pl.BlockSpec(memory_space=pl.ANY)
    fresh = cache is None
    fill_layers = tuple(l for l in range(n_layers) if l != layer) if fresh else ()
    n_batch = tm // seq_len
    return pl.pallas_call(
        functools.partial(_mm_cache_kernel, layer=layer, fill_layers=fill_layers,
                          n_steps=grid[0] * grid[1]),
        grid=grid,
        in_specs=[pl.BlockSpec((tm, k), lambda i, j: (i, 0)),
                  pl.BlockSpec((None, k, tn), lambda i, j: (w.layer, 0, nb0 + j))]
                 + ([] if fresh else [any_spec]),
        out_specs=[pl.BlockSpec((tm, tn), lambda i, j: (i, j)), any_spec],
        out_shape=[jax.ShapeDtypeStruct((m, d), BF16), jax.ShapeDtypeStruct(cache_shape, F32)],
        scratch_shapes=[pltpu.VMEM((2, n_batch, seq_len, tn), F32),
                        pltpu.VMEM((n_batch, seq_len, HEAD_DIM), F32),
                        pltpu.SemaphoreType.DMA((2, 1 + len(fill_layers), tn // HEAD_DIM))],
        input_output_aliases={} if fresh else {2: 1},
        compiler_params=_cparams("arbitrary", "arbitrary"),
        name=name,
    )(a, w.stack, *([] if fresh else [cache]))


def _gate_group_matmul(h, w_in, z_col0, d, w_grp, pool_scale_row):
    m, dm = d.shape
    k = h.shape[1]
    _, g, dg, _ = w_grp.stack.shape
    tm = min(MM_TILE, m)
    zb0 = z_col0 // dg
    return pl.pallas_call(
        _mm_gate_group_kernel,
        grid=(m // tm, g),
        in_specs=[
            pl.BlockSpec((tm, k), lambda i, j: (i, 0)),
            pl.BlockSpec((None, k, dg), lambda i, j: (w_in.layer, 0, zb0 + j)),
            pl.BlockSpec((tm, dg), lambda i, j: (i, j)),
            pl.BlockSpec((None, None, dg, dg), lambda i, j: (w_grp.layer, j, 0, 0)),
            pl.BlockSpec((1, dg), lambda i, j: (0, j)),
        ],
        out_specs=pl.BlockSpec((tm, dg), lambda i, j: (i, j)),
        out_shape=jax.ShapeDtypeStruct((m, dm), BF16),
        compiler_params=_cparams("parallel", "parallel"),
        name="pool_gate_group",
    )(h, w_in.stack, d, w_grp.stack, pool_scale_row)


def _out_proj_norm(a, w_out, x2d, mod4, layer, seq_len, row0, per_batch, norm_gain, next_layer):
    m, d = x2d.shape
    k = a.shape[1]
    tm = min(RESID_NORM_ROWS, seq_len if per_batch else m)
    tn = min(MM_TILE, d)
    n_cols = d // tn
    modulated = next_layer is not None

    def mod_row(lyr, part, cols):
        return _mod_row_map(lyr, part, seq_len, tm, row0, per_batch, cols)

    in_specs = [
        pl.BlockSpec((tm, k), lambda i, j: (i, 0)),
        pl.BlockSpec((None, k, tn), lambda i, j: (w_out.layer, 0, j)),
        pl.BlockSpec((tm, tn), lambda i, j: (i, j)),
        pl.BlockSpec((None, None, 1, tn), mod_row(layer, 2, (1, n_cols))),
    ]
    row_block = pl.BlockSpec((tm, d), lambda i, j: (i, 0))
    small = [pltpu.VMEM((tm, LANES), F32), pltpu.VMEM((NORM_ROWS, d), F32)]
    if modulated:
        in_specs += [pl.BlockSpec((None, 1, d), lambda i, j: (next_layer, 0, 0)),
                     pl.BlockSpec((None, None, 1, d), mod_row(next_layer, 0, None)),
                     pl.BlockSpec((None, None, 1, d), mod_row(next_layer, 1, None))]
        operands = (a, w_out.stack, x2d, mod4, norm_gain, mod4, mod4)
        out_specs = [pl.BlockSpec((tm, tn), lambda i, j: (i, j)), row_block]
        out_shape = [jax.ShapeDtypeStruct((m, d), F32), jax.ShapeDtypeStruct((m, d), BF16)]
        scratch = [pltpu.VMEM((tm, d), F32)] + small + [pltpu.VMEM((NORM_ROWS, d), F32)]
    else:
        in_specs += [pl.BlockSpec((1, d), lambda i, j: (0, 0))]
        operands = (a, w_out.stack, x2d, mod4, norm_gain)
        out_specs = row_block
        out_shape = jax.ShapeDtypeStruct((m, d), F32)
        scratch = small
    return pl.pallas_call(
        functools.partial(_mm_resid_norm_kernel, n_cols=n_cols, modulated=modulated),
        grid=(m // tm, n_cols),
        in_specs=in_specs, out_specs=out_specs, out_shape=out_shape, scratch_shapes=scratch,
        compiler_params=_cparams("parallel", "arbitrary"),
        name="out_proj_norm" if modulated else "out_proj_final_norm",
    )(*operands)


def _softmax_exps(scores):
    m = scores[0].max(axis=-1, keepdims=True)
    for s in scores[1:]:
        m = jnp.maximum(m, s.max(axis=-1, keepdims=True))
    return [jnp.exp(s - m).astype(BF16) for s in scores]


def _qk(q, k):
    return lax.dot_general(q, k, (((1,), (1,)), ((), ())), preferred_element_type=F32)


def _with_ones(v):
    return jnp.concatenate([v, jnp.ones(v.shape, v.dtype)], axis=1)


def _normalise_gate(o_ext, z):
    o = o_ext[:, :HEAD_DIM] / o_ext[:, HEAD_DIM:]
    return o * _silu(z.astype(F32))


def _attn_ctx_kernel(q_ref, k_ref, v_ref, z_ref, o_ref, *, heads):
    for hh in range(heads):
        sl = slice(hh * HEAD_DIM, (hh + 1) * HEAD_DIM)
        k = k_ref[:, sl].astype(BF16)
        v = _with_ones(v_ref[:, sl].astype(BF16))
        (e,) = _softmax_exps([_qk(q_ref[:, sl], k)])
        o_ext = jnp.dot(e, v, preferred_element_type=F32)
        o_ref[:, sl] = _normalise_gate(o_ext, z_ref[:, sl]).astype(o_ref.dtype)


def _heads_per_step(n_heads, most=8):
    return min(most, n_heads)


def _attn_context(q, k, v, z, seq_len):
    m, d = q.shape
    hg = _heads_per_step(d // HEAD_DIM, most=CTX_HEADS_PER_STEP)
    wblk = hg * HEAD_DIM
    spec = pl.BlockSpec((seq_len, wblk), lambda b, h: (b, h))
    return pl.pallas_call(
        functools.partial(_attn_ctx_kernel, heads=hg),
        grid=(m // seq_len, d // wblk),
        in_specs=[spec, spec, spec, spec],
        out_specs=spec,
        out_shape=jax.ShapeDtypeStruct((m, d), BF16),
        compiler_params=_cparams("parallel", "parallel"),
        name="attn_context",
    )(q, k, v, z)


def _rel_bias_kernel(rpb_ref, o_ref):
    rows, lanes = o_ref.shape
    table = rpb_ref[...]
    kc = lax.broadcasted_iota(jnp.int32, (rows, LANES), 1) % GRID_W
    row = lax.broadcasted_iota(jnp.int32, (rows, LANES), 0)
    live_row = (row % N_REL_ROWS) != (N_REL_ROWS - 1)
    for qc in range(lanes // LANES):
        rel = jnp.clip(kc - qc, -(WIN_W - 1), WIN_W - 1) + (WIN_W - 1)
        start = min(max(qc - WIN_W // 2, 0), GRID_W - WIN_W)
        in_window = (kc >= start) & (kc < start + WIN_W)
        bias = jnp.take_along_axis(table, rel, axis=1)
        o_ref[:, qc * LANES:(qc + 1) * LANES] = jnp.where(in_window & live_row, bias, NEG_INF)


def _rel_bias_table(rpb_layer):
    h = rpb_layer.shape[0]
    assert 2 * GRID_W == LANES and N_REL_COLS <= LANES
    padded = jnp.pad(rpb_layer, ((0, 0), (0, 1), (0, LANES - N_REL_COLS))).reshape(h * N_REL_ROWS, LANES)
    rows = h * N_REL_ROWS
    tr = min(64, rows)
    lanes = GRID_W * 2 * GRID_W
    flat = pl.pallas_call(
        _rel_bias_kernel,
        grid=(rows // tr,),
        in_specs=[pl.BlockSpec((tr, LANES), lambda i: (i, 0))],
        out_specs=pl.BlockSpec((tr, lanes), lambda i: (i, 0)),
        out_shape=jax.ShapeDtypeStruct((rows, lanes), F32),
        compiler_params=_cparams("parallel"),
        name="rel_bias_table",
    )(padded)
    return flat.reshape(h, N_REL_ROWS, GRID_W, 2 * GRID_W)


def _cache_copies(cache_hbm, buf, sem, layer, b, head0, slot, heads):
    return [pltpu.make_async_copy(cache_hbm.at[b, layer, :, head0 + hh, :], buf.at[slot, hh],
                                  sem.at[slot, hh]) for hh in range(heads)]


class _CastJob(NamedTuple):
    src: jax.Array
    row0: int
    rows: int
    col0: int
    cols: int


def _plan_cast(src, row0, rows, col0, cols, n_steps):
    even = rows > 0 and rows % (n_steps * PACKED_ROWS) == 0 and cols % LANES == 0 and col0 % LANES == 0
    return _CastJob(src, row0, rows, col0, cols) if even else None


def _cast_now(src, row0, rows, col0, cols):
    return src[row0:row0 + rows, col0:col0 + cols].astype(BF16)


def _cast_block_kernel(x_ref, o_ref):
    o_ref[...] = x_ref[...].astype(o_ref.dtype)


def _cast_columns(stack, layer, col0, cols):
    _, k, _ = stack.shape
    tr = min(512, k)
    assert col0 % cols == 0
    return pl.pallas_call(
        _cast_block_kernel,
        grid=(k // tr,),
        in_specs=[pl.BlockSpec((None, tr, cols), lambda i: (layer, i, col0 // cols))],
        out_specs=pl.BlockSpec((tr, cols), lambda i: (i, 0)),
        out_shape=jax.ShapeDtypeStruct((k, cols), BF16),
        compiler_params=_cparams("parallel"),
        name="cast_columns",
    )(stack)


def _job_scratch(jobs, n_steps):
    scratch = []
    for job in jobs:
        chunk = (2, job.rows // n_steps, job.cols)
        scratch += [pltpu.VMEM(chunk, F32), pltpu.VMEM(chunk, BF16)]
    if jobs:
        scratch.append(pltpu.SemaphoreType.DMA((len(jobs), 2, 2)))
    return scratch


def _job_out_shapes(jobs):
    return [jax.ShapeDtypeStruct((job.rows, job.cols), BF16) for job in jobs]


def _cast_jobs_step(jobs, srcs, dsts, bufs, sem, step, n_steps):
    slot = step % 2
    per_step = [job.rows // n_steps for job in jobs]

    def load(j, s, which):
        rows = pl.ds(jobs[j].row0 + s * per_step[j], per_step[j])
        cols = pl.ds(jobs[j].col0, jobs[j].cols)
        return pltpu.make_async_copy(srcs[j].at[rows, cols], bufs[j][0].at[which], sem.at[j, 0, which])

    def store(j, s, which):
        rows = pl.ds(s * per_step[j], per_step[j])
        return pltpu.make_async_copy(bufs[j][1].at[which], dsts[j].at[rows, :], sem.at[j, 1, which])

    n_jobs = range(len(jobs))

    @pl.when(step == 0)
    def _():
        for j in n_jobs:
            load(j, step, slot).start()

    @pl.when(step + 1 < n_steps)
    def _():
        for j in n_jobs:
            load(j, step + 1, 1 - slot).start()

    @pl.when(step >= 2)
    def _():
        for j in n_jobs:
            store(j, step - 2, slot).wait()

    for j in n_jobs:
        load(j, step, slot).wait()
        bufs[j][1][slot] = bufs[j][0][slot].astype(BF16)
        store(j, step, slot).start()

    @pl.when(step == n_steps - 1)
    def _():
        for j in n_jobs:
            if n_steps > 1:
                store(j, step - 1, 1 - slot).wait()
            store(j, step, slot).wait()


def _attn_lat_kernel(*refs, heads, grid_rows, layer, jobs, n_steps):
    n_jobs = len(jobs)
    (q_ref, k0_ref, k1_ref, k2_ref, v0_ref, v1_ref, v2_ref, kc_hbm, vc_hbm, z_ref, tab_ref) = refs[:11]
    job_srcs = refs[11:11 + n_jobs]
    o_ref = refs[11 + n_jobs]
    job_dsts = refs[12 + n_jobs:12 + 2 * n_jobs]
    bias_ref, kc_buf, vc_buf, kc_sem, vc_sem = refs[12 + 2 * n_jobs:17 + 2 * n_jobs]
    job_scratch = refs[17 + 2 * n_jobs:]
    hgi, rb, b = pl.program_id(0), pl.program_id(1), pl.program_id(2)
    n_rb, n_b = pl.num_programs(1), pl.num_programs(2)
    step = (hgi * n_rb + rb) * n_b + b
    slot = step % 2

    if n_jobs:
        bufs = [job_scratch[2 * j:2 * j + 2] for j in range(n_jobs)]
        _cast_jobs_step(jobs, job_srcs, job_dsts, bufs, job_scratch[-1], step, n_steps)

    def cache_copies(head_group, batch, which):
        return (_cache_copies(kc_hbm, kc_buf, kc_sem, layer, batch, head_group * heads, which, heads)
                + _cache_copies(vc_hbm, vc_buf, vc_sem, layer, batch, head_group * heads, which, heads))

    @pl.when(step == 0)
    def _():
        for cp in cache_copies(hgi, b, slot):
            cp.start()

    @pl.when(step + 1 < n_steps)
    def _():
        last_b = b == n_b - 1
        next_b = jnp.where(last_b, 0, b + 1)
        next_hgi = jnp.where(last_b & (rb == n_rb - 1), hgi + 1, hgi)
        for cp in cache_copies(next_hgi, next_b, 1 - slot):
            cp.start()

    for cp in cache_copies(hgi, b, slot):
        cp.wait()

    slab0 = Q_ROWS * jnp.clip(rb - 1, 0, grid_rows // Q_ROWS - SLAB_BLOCKS)

    @pl.when(pl.program_id(2) == 0)
    def _():
        left = lax.broadcasted_iota(jnp.int32, (GRID_W, 2 * GRID_W), 1) < GRID_W
        for rq in range(Q_ROWS):
            r = rb * Q_ROWS + rq
            win0 = jnp.clip(r - WIN_H // 2, 0, grid_rows - WIN_H)

            def entry(kk):
                kr = slab0 + kk
                ok = (kr >= win0) & (kr < win0 + WIN_H)
                return jnp.where(ok, kr - r + (WIN_H - 1), N_REL_ROWS - 1)

            for kk in range(0, SLAB_ROWS, 2):
                i_left, i_right = entry(kk), entry(kk + 1)
                for hh in range(heads):
                    tile = jnp.where(left, tab_ref[hh, i_left], tab_ref[hh, i_right])
                    bias_ref[hh, rq * GRID_W:(rq + 1) * GRID_W, kk * GRID_W:(kk + 2) * GRID_W] = tile

    k_refs = (k0_ref, k1_ref, k2_ref)
    v_refs = (v0_ref, v1_ref, v2_ref)

    def attend(pieces):
        for hh in range(heads):
            sl = slice(hh * HEAD_DIM, (hh + 1) * HEAD_DIM)
            q = q_ref[:, sl]
            scores = [_qk(q, k_refs[p][:, sl]) + bias_ref[hh, :, p * Q_BLOCK:(p + 1) * Q_BLOCK]
                      for p in pieces]
            scores.append(_qk(q, kc_buf[slot, hh].astype(BF16)))
            exps = _softmax_exps(scores)
            o_ext = jnp.dot(exps[-1], _with_ones(vc_buf[slot, hh].astype(BF16)),
                            preferred_element_type=F32)
            for e, p in zip(exps, pieces):
                o_ext = o_ext + jnp.dot(e, _with_ones(v_refs[p][:, sl]), preferred_element_type=F32)
            o_ref[:, sl] = _normalise_gate(o_ext, z_ref[:, sl]).astype(o_ref.dtype)

    all_pieces = tuple(range(SLAB_BLOCKS))
    pl.when(rb == 0)(lambda: attend(all_pieces[:-1]))
    pl.when(rb == n_rb - 1)(lambda: attend(all_pieces[1:]))
    pl.when((rb > 0) & (rb < n_rb - 1))(lambda: attend(all_pieces))


def _latent_steps(m, d, seq_len):
    return (d // (_heads_per_step(d // HEAD_DIM) * HEAD_DIM)) * (seq_len // Q_BLOCK) * (m // seq_len)


def _attn_latent(q, k, v, z, k_ctx, v_ctx, layer, bias_tab, seq_len, jobs=()):
    m, d = q.shape
    batch = m // seq_len
    grid_rows = seq_len // GRID_W
    n_rb = grid_rows // Q_ROWS
    assert grid_rows % Q_ROWS == 0 and n_rb >= SLAB_BLOCKS
    past = k_ctx.shape[2]
    hg = _heads_per_step(d // HEAD_DIM)
    wblk = hg * HEAD_DIM

    q_spec = pl.BlockSpec((Q_BLOCK, wblk), lambda h, rb, b: (b * n_rb + rb, h))

    def slab_spec(p):
        return pl.BlockSpec(
            (Q_BLOCK, wblk),
            lambda h, rb, b: (b * n_rb + jnp.clip(rb - 1, 0, n_rb - SLAB_BLOCKS) + p, h))

    any_spec = pl.BlockSpec(memory_space=pl.ANY)
    tab_spec = pl.BlockSpec((hg, N_REL_ROWS, GRID_W, 2 * GRID_W), lambda h, rb, b: (h, 0, 0, 0))
    n_steps = _latent_steps(m, d, seq_len)
    out, *converted = pl.pallas_call(
        functools.partial(_attn_lat_kernel, heads=hg, grid_rows=grid_rows, layer=layer,
                          jobs=tuple(job._replace(src=None) for job in jobs), n_steps=n_steps),
        grid=(d // wblk, n_rb, batch),
        in_specs=[q_spec, slab_spec(0), slab_spec(1), slab_spec(2),
                  slab_spec(0), slab_spec(1), slab_spec(2),
                  any_spec, any_spec, q_spec, tab_spec] + [any_spec] * len(jobs),
        out_specs=[q_spec] + [any_spec] * len(jobs),
        out_shape=[jax.ShapeDtypeStruct((m, d), BF16)]
                  + _job_out_shapes(jobs),
        scratch_shapes=[pltpu.VMEM((hg, Q_BLOCK, SLAB_BLOCKS * Q_BLOCK), F32),
                        pltpu.VMEM((2, hg, past, HEAD_DIM), F32),
                        pltpu.VMEM((2, hg, past, HEAD_DIM), F32),
                        pltpu.SemaphoreType.DMA((2, hg)),
                        pltpu.SemaphoreType.DMA((2, hg))] + _job_scratch(jobs, n_steps),
        compiler_params=_cparams("arbitrary", "arbitrary", "arbitrary"),
        name="attn_latent",
    )(q, k, k, k, v, v, v, k_ctx, v_ctx, z, bias_tab, *[job.src for job in jobs])
    return out, converted


class _CastLedger:
    def __init__(self):
        self.ready = {}
        self.pending = {}

    def want(self, name, src, row0, rows, col0, cols, host, host_steps):
        job = _plan_cast(src, row0, rows, col0, cols, host_steps)
        if job is None:
            self.ready[name] = _cast_now(src, row0, rows, col0, cols)
        else:
            self.pending.setdefault(host, []).append((name, job))

    def jobs(self, host):
        return tuple(job for _, job in self.pending.get(host, ()))

    def done(self, host, converted):
        for (name, _), array in zip(self.pending.pop(host), converted):
            self.ready[name] = array


def _attn_projections(h, weight_of, d, tag, ledger, new_caches=None, cache_shape=None, layer=0):
    def project(part, kern):
        w, col0 = weight_of(part)
        host = f"{part}_{tag}"
        jobs = ledger.jobs(host)
        out = _matmul(kern, h, w, col0, d, BF16, "attn_in_" + part, jobs=jobs)
        if jobs:
            out, converted = out
            ledger.done(host, converted)
        return out

    plain = functools.partial(_mm_cast_kernel, scale=None)
    q = project("q", functools.partial(_mm_cast_kernel, scale=HEAD_DIM ** -0.5))
    if cache_shape is None:
        k, v = project("k", plain), project("v", plain)
    else:
        (wk, ck), (wv, cv) = weight_of("k"), weight_of("v")
        k, new_k = _matmul_to_cache(h, wk, ck, new_caches[0], cache_shape, layer, "attn_in_k_cache")
        v, new_v = _matmul_to_cache(h, wv, cv, new_caches[1], cache_shape, layer, "attn_in_v_cache")
        new_caches = (new_k, new_v)
    z = project("z", plain)
    return q, k, v, z, new_caches


def _pool_branch(h, w_in, w_grp, pool_scale_row, d, seq_len):
    dg = d // len(POOL_WINDOWS)
    pooled = _matmul(functools.partial(_mm_pool_kernel, seq_len=seq_len, cols_per_group=dg),
                     h, w_in, 0, d, BF16, "pool_in_u", tn_max=dg)
    return _gate_group_matmul(h, w_in, d, pooled, w_grp, pool_scale_row)


def kernel(x_prompt, x_sample, c, cache_k, cache_v, c_ctx, norm_g, w_ada, b_ada, w_in_attn, rpb,
           w_out_attn, w_in_pool, w_grp_pool, pool_scale, w_out_pool, final_norm_g):
    bc, tc, d = x_prompt.shape
    bl, tl, _ = x_sample.shape
    depth = norm_g.shape[0]
    n_heads = cache_k.shape[3]
    n_attn = w_in_attn.shape[0]
    assert n_heads * HEAD_DIM == d and tl % (GRID_W * Q_ROWS) == 0

    n_cond = -(-(1 + bl) // 8) * 8
    cond = jnp.concatenate([c_ctx[None, :], c, jnp.zeros((n_cond - 1 - bl, d), F32)], axis=0)
    mod = _ada_modulation(cond, w_ada, b_ada)
    mod4 = mod.reshape(depth, n_cond, 1, 3 * d)

    xc = x_prompt.reshape(bc * tc, d)
    xl = x_sample.reshape(bl * tl, d)
    g_rows = norm_g.reshape(depth, 1, d)

    n_pool = w_in_pool.shape[0]
    flat = lambda w: w.reshape(-1, w.shape[-1])
    steps_c, steps_l = _matmul_steps(bc * tc, d), _matmul_steps(bl * tl, d)
    ledger = _CastLedger()
    ledger.ready["attn0_q"] = _cast_columns(w_in_attn, 0, 0, d)
    ledger.ready["attn0_z"] = _cast_columns(w_in_attn, 0, 3 * d, d)
    ledger.want("attn0_k", flat(w_in_attn), 0, d, d, d, "q_c", steps_c)
    ledger.want("attn0_v", flat(w_in_attn), 0, d, 2 * d, d, "q_c", steps_c)
    ledger.want("out_attn", flat(w_out_attn), 0, n_attn * d, 0, d, "z_c", steps_c)
    pool_a = (n_pool // 2) * d if n_pool > 1 else n_pool * d
    ledger.want("in_pool_a", flat(w_in_pool), 0, pool_a, 0, 2 * d, "q_l", steps_l)
    ledger.want("in_pool_b", flat(w_in_pool), pool_a, n_pool * d - pool_a, 0, 2 * d, "k_l", steps_l)
    ledger.want("attn_rest_a", flat(w_in_attn), d, (n_attn - 1) * d, 0, 2 * d, "v_l", steps_l)
    ledger.want("attn_rest_b", flat(w_in_attn), d, (n_attn - 1) * d, 2 * d, 2 * d, "z_l", steps_l)
    lat_steps = _latent_steps(bl * tl, d, tl)
    ledger.want("out_pool", flat(w_out_pool), 0, n_pool * d, 0, d, "lat", lat_steps)
    ledger.want("grp_pool", flat(w_grp_pool), 0, flat(w_grp_pool).shape[0], 0, w_grp_pool.shape[-1],
                "lat", lat_steps)

    def attn_weight_of(j):
        def weight_of(part):
            if j == 0:
                return _Weight(ledger.ready["attn0_" + part][None], 0), 0
            half = ledger.ready["attn_rest_a" if part in "qk" else "attn_rest_b"]
            return _Weight(half.reshape(n_attn - 1, d, 2 * d), j - 1), (0 if part in "qv" else d)
        return weight_of

    def pool_weight(j):
        if j * d < pool_a:
            return _Weight(ledger.ready["in_pool_a"].reshape(-1, d, 2 * d), j)
        return _Weight(ledger.ready["in_pool_b"].reshape(-1, d, 2 * d), j - pool_a // d)

    cache_shape = (bc, n_attn, tc, n_heads, HEAD_DIM)
    new_caches = (None, None)
    hc = _modulate(xc, g_rows, mod4, 0, tc, 0, False)
    hl = _modulate(xl, g_rows, mod4, 0, tl, 1, True)
    for i in range(depth):
        j = i // 2
        if i % 2 == 0:
            weight_of = attn_weight_of(j)
            qc, kc, vc, zc, new_caches = _attn_projections(hc, weight_of, d, "c", ledger, new_caches,
                                                           cache_shape, j)
            oc = _attn_context(qc, kc, vc, zc, tc)
            ql, kl, vl, zl, _ = _attn_projections(hl, weight_of, d, "l", ledger)
            bias_tab = _rel_bias_table(rpb[j])
            jobs = ledger.jobs("lat")
            ol, converted = _attn_latent(ql, kl, vl, zl, cache_k, cache_v, j, bias_tab, tl, jobs=jobs)
            if jobs:
                ledger.done("lat", converted)
            w_out = _Weight(ledger.ready["out_attn"].reshape(n_attn, d, d), j)
        else:
            w_in = pool_weight(j)
            w_grp = _Weight(ledger.ready["grp_pool"].reshape(w_grp_pool.shape), j)
            ps_row = pool_scale[j].reshape(1, d)
            oc = _pool_branch(hc, w_in, w_grp, ps_row, d, tc)
            ol = _pool_branch(hl, w_in, w_grp, ps_row, d, tl)
            w_out = _Weight(ledger.ready["out_pool"].reshape(n_pool, d, d), j)
        if i + 1 < depth:
            xc, hc = _out_proj_norm(oc, w_out, xc, mod4, i, tc, 0, False, g_rows, i + 1)
            xl, hl = _out_proj_norm(ol, w_out, xl, mod4, i, tl, 1, True, g_rows, i + 1)
        else:
            final_gain = final_norm_g.reshape(1, d)
            xc = _out_proj_norm(oc, w_out, xc, mod4, i, tc, 0, False, final_gain, None)
            xl = _out_proj_norm(ol, w_out, xl, mod4, i, tl, 1, True, final_gain, None)

    return (xc.reshape(bc, tc, d), xl.reshape(bl, tl, d), new_caches[0], new_caches[1])
```
